```python
import math
import jax
import jax.numpy as jnp
from jax import lax
import numpy as np

D_MODEL = 2048
BATCH = 16
SEQ = 2048
DEPTH = 4
DEC_BATCH = 32
DEC_SEQ = 16
PAST_LEN = 4096

CHUNK = 64
N_BRANCH = 4
BRANCH_W = D_MODEL // 2
SSM_P = 64
SSM_H = BRANCH_W // SSM_P
SSM_G = 4
SSM_N = 128
SSM_D = SSM_H * SSM_P
CONV_W = 4
SSD_BLOCK = 64
XBC_W = SSM_D + 2 * SSM_G * SSM_N
DT_MIN = 0.001
DT_MAX = 0.1
POOL_WINDOWS = (2, 4, 8, 16)
POOL_GW = BRANCH_W // len(POOL_WINDOWS)
POOL_BUF = max(POOL_WINDOWS) - 1
ATT_H = 8
ATT_KV = 4
ATT_R = ATT_H // ATT_KV
ATT_D = 64
ATT_VD = 2 * ATT_D
Q_BLOCK = 128
N_BUCKETS = 32
MAX_DIST = 128
GM_CHUNK = 128
GM_G = 4
GM_GW = BRANCH_W // GM_G
MEM_LEN = 256
X_H = 4
X_D = 128
D_FF = 5632
EPS = 1e-6

OFF_Z = 0
OFF_XBC = OFF_Z + SSM_D
OFF_DT = OFF_XBC + XBC_W
OFF_POOL = OFF_DT + SSM_H
OFF_Q = OFF_POOL + BRANCH_W
OFF_K = OFF_Q + ATT_H * 2 * ATT_D
OFF_V = OFF_K + ATT_KV * 2 * ATT_D
OFF_GU = OFF_V + ATT_KV * ATT_VD
OFF_GV = OFF_GU + BRANCH_W
IN_W = OFF_GV + BRANCH_W

kernel_name = 'hybrid_streaming_encoder_step'


def rmsnorm(x, g):
    xf = x.astype(jnp.float32)
    y = xf * lax.rsqrt(jnp.mean(xf * xf, axis=-1, keepdims=True) + EPS)
    return (y * g.astype(jnp.float32)).astype(x.dtype)


def swiglu_ffn(h, g, w_in, w_out):
    a = rmsnorm(h, g) @ w_in
    gate, up = jnp.split(a, 2, axis=-1)
    return (jax.nn.silu(gate) * up) @ w_out


def t5_bucket(rel):
    nb = N_BUCKETS // 2
    max_exact = nb // 2
    ret = jnp.where(rel > 0, nb, 0)
    n = jnp.abs(rel)
    large = max_exact + (jnp.log(jnp.maximum(n, 1).astype(jnp.float32) / max_exact)
                         / math.log(MAX_DIST / max_exact) * (nb - max_exact)).astype(jnp.int32)
    large = jnp.minimum(large, nb - 1)
    return ret + jnp.where(n < max_exact, n, large)


def causal_conv(u, buf, w, b):
    L = u.shape[1]
    full = jnp.concatenate([buf.astype(u.dtype), u], axis=1)
    y = b + sum(full[:, k:k + L] * w[k] for k in range(CONV_W))
    return y, full[:, -(CONV_W - 1):]


def ssd_scan(x, dt, A, Bm, Cm, s0):
    b, l, h, p = x.shape
    g, n = Bm.shape[2], Bm.shape[3]
    r = h // g
    q = min(SSD_BLOCK, l)
    c = l // q
    xs = (x * dt[..., None]).reshape(b, c, q, g, r, p)
    a = (dt * A).reshape(b, c, q, g, r)
    cum = jnp.cumsum(a, axis=2)
    Bc = Bm.reshape(b, c, q, g, n)
    Cc = Cm.reshape(b, c, q, g, n)
    seg = cum[:, :, :, None] - cum[:, :, None]
    causal = jnp.tril(jnp.ones((q, q), bool))[:, :, None, None]
    decay = jnp.exp(jnp.where(causal, seg, -jnp.inf))
    cb = jnp.einsum('bcign,bcjgn->bcijg', Cc, Bc)
    y_diag = jnp.einsum('bcijg,bcijgr,bcjgrp->bcigrp', cb, decay, xs)
    decay_end = jnp.exp(cum[:, :, -1:] - cum)
    chunk_states = jnp.einsum('bcjgn,bcjgr,bcjgrp->bcgrpn', Bc, decay_end, xs)
    chunk_decay = jnp.exp(cum[:, :, -1])

    def step(s, inp):
        cs, cd = inp
        return cd[..., None, None] * s + cs, s

    s_final, s_before = lax.scan(step, s0.reshape(b, g, r, p, n),
                                 (jnp.moveaxis(chunk_states, 1, 0), jnp.moveaxis(chunk_decay, 1, 0)))
    s_before = jnp.moveaxis(s_before, 0, 1)
    y_off = jnp.einsum('bcign,bcgrpn,bcigr->bcigrp', Cc, s_before, jnp.exp(cum))
    return (y_diag + y_off).reshape(b, l, h, p), s_final.reshape(b, h, p, n)


def ssd_mixer(z, xbc, dt_raw, conv_buf, s0, conv_w, conv_b, dt_bias, a_log, d_skip, g_ssd):
    b, L, _ = z.shape
    xbc_c, new_buf = causal_conv(xbc, conv_buf, conv_w, conv_b)
    xbc_c = jax.nn.silu(xbc_c)
    xs, Bm, Cm = jnp.split(xbc_c, [SSM_D, SSM_D + SSM_G * SSM_N], axis=-1)
    xs = xs.reshape(b, L, SSM_H, SSM_P).astype(jnp.float32)
    Bm = Bm.reshape(b, L, SSM_G, SSM_N).astype(jnp.float32)
    Cm = Cm.reshape(b, L, SSM_G, SSM_N).astype(jnp.float32)
    dt = jax.nn.softplus((dt_raw + dt_bias).astype(jnp.float32))
    A = -jnp.exp(a_log.astype(jnp.float32))
    y, s_new = ssd_scan(xs, dt, A, Bm, Cm, s0.astype(jnp.float32))
    y = y + d_skip.astype(jnp.float32)[:, None] * xs
    y = y.reshape(b, L, SSM_D) * jax.nn.silu(z.astype(jnp.float32))
    y = rmsnorm(y.reshape(b, L, SSM_G, SSM_D // SSM_G), g_ssd.reshape(SSM_G, -1)).reshape(b, L, SSM_D)
    return y.astype(z.dtype), s_new, new_buf


def pool_mixer(xp, buf, pos0, w_pool, scale):
    b, L, _ = xp.shape
    full = jnp.concatenate([buf.astype(xp.dtype), xp], axis=1)
    cs = jnp.pad(jnp.cumsum(full.astype(jnp.float32), axis=1), ((0, 0), (1, 0), (0, 0)))
    pos = pos0 + jnp.arange(L)
    pooled = []
    for gi, w in enumerate(POOL_WINDOWS):
        c0 = gi * POOL_GW
        s = (cs[:, POOL_BUF + 1:POOL_BUF + 1 + L, c0:c0 + POOL_GW]
             - cs[:, POOL_BUF + 1 - w:POOL_BUF + 1 - w + L, c0:c0 + POOL_GW])
        cnt = jnp.minimum(w, pos + 1).astype(jnp.float32)
        pooled.append(s / cnt[None, :, None])
    pooled = jnp.concatenate(pooled, axis=-1) - xp.astype(jnp.float32)
    y = jnp.einsum('blgc,gcd->blgd', pooled.reshape(b, L, len(POOL_WINDOWS), POOL_GW).astype(xp.dtype), w_pool)
    return y.reshape(b, L, BRANCH_W) * scale, full[:, -POOL_BUF:]


def diff_attn_mixer(q_raw, k_raw, v_raw, past_k, past_v, g_q, g_k, lam_p, g_subln, rel_bias, lam_init):
    b, L, _ = q_raw.shape
    q = rmsnorm(q_raw.reshape(b, L, ATT_KV, ATT_R, 2, ATT_D), g_q)
    k = rmsnorm(k_raw.reshape(b, L, ATT_KV, 2, ATT_D), g_k)
    v = v_raw.reshape(b, L, ATT_KV, ATT_VD)
    k_all = jnp.concatenate([past_k.astype(k.dtype), k], axis=1)
    v_all = jnp.concatenate([past_v.astype(v.dtype), v], axis=1)
    T = k_all.shape[1]
    past = T - L
    lp = lam_p.astype(jnp.float32)
    lam = jnp.exp(jnp.sum(lp[0] * lp[1])) - jnp.exp(jnp.sum(lp[2] * lp[3])) + lam_init
    blk = min(Q_BLOCK, L)
    scale = ATT_D ** -0.5
    outs = []
    for qs in range(0, L, blk):
        kend = past + qs + blk
        qb = q[:, qs:qs + blk]
        kb = k_all[:, :kend]
        vb = v_all[:, :kend]
        qpos = past + qs + jnp.arange(blk)
        kpos = jnp.arange(kend)
        s = jnp.einsum('bqkrmd,bskmd->bkrmqs', qb, kb).astype(jnp.float32) * scale
        bias = rel_bias[t5_bucket(kpos[None, :] - qpos[:, None])].astype(jnp.float32)
        bias = bias.transpose(2, 0, 1).reshape(ATT_KV, ATT_R, 1, blk, kend)
        mask = (kpos[None, :] // CHUNK) <= (qpos[:, None] // CHUNK)
        p = jax.nn.softmax(jnp.where(mask, s + bias, -jnp.inf), axis=-1)
        a = p[:, :, :, 0] - lam * p[:, :, :, 1]
        outs.append(jnp.einsum('bkrqs,bskd->bqkrd', a.astype(vb.dtype), vb))
    o = jnp.concatenate(outs, axis=1)
    o = rmsnorm(o, g_subln) * (1.0 - lam_init)
    return o.reshape(b, L, ATT_H * ATT_VD), k, v


def gmlp_mixer(u_raw, v_raw, g_gv, w_sp, b_sp):
    b, L, _ = u_raw.shape
    u = jax.nn.gelu(u_raw)
    vn = rmsnorm(jax.nn.gelu(v_raw), g_gv)
    cl = min(GM_CHUNK, L)
    nc = L // cl
    w = w_sp[:, :cl, :cl] * jnp.tril(jnp.ones((cl, cl), w_sp.dtype))
    vc = vn.reshape(b, nc, cl, GM_G, GM_GW)
    s = jnp.einsum('gij,bcjgd->bcigd', w, vc) + b_sp[:, :cl].T[None, None, :, :, None]
    return u * s.reshape(b, L, BRANCH_W), vn


def mem_kv(mem, g_mem, w_mk, w_mv, g_xk):
    b, M, _ = mem.shape
    mn = rmsnorm(mem, g_mem)
    k = rmsnorm((mn @ w_mk).reshape(b, M, X_H, X_D), g_xk)
    v = (mn @ w_mv).reshape(b, M, X_H, X_D)
    return k, v


def cross_attn(h, mk, mv, g_x, w_xq, g_xq, w_xo):
    b, L, _ = h.shape
    q = rmsnorm((rmsnorm(h, g_x) @ w_xq).reshape(b, L, X_H, X_D), g_xq)
    s = jnp.einsum('blhd,bmhd->bhlm', q, mk.astype(q.dtype)).astype(jnp.float32) * (X_D ** -0.5)
    p = jax.nn.softmax(s, axis=-1).astype(h.dtype)
    o = jnp.einsum('bhlm,bmhd->blhd', p, mv.astype(h.dtype)).reshape(b, L, X_H * X_D)
    return o @ w_xo


def run_trunk(x, past_k, past_v, mem_k, mem_v, ssm0, conv0, pool0, pos0, P, keep_gmlp_rows):
    h = x
    new_k, new_v, new_ssm, new_conv, new_pool, new_gv = [], [], [], [], [], []
    for l in range(DEPTH):
        lam_init = 0.8 - 0.6 * math.exp(-0.3 * l)
        h = h + 0.5 * swiglu_ffn(h, P['g_ffn1'][l], P['w_ffn1_in'][l], P['w_ffn1_out'][l])
        hn = rmsnorm(h, P['g_mix'][l])
        proj = hn @ P['w_in'][l]
        y_a, s_ssm, s_conv = ssd_mixer(proj[..., OFF_Z:OFF_XBC], proj[..., OFF_XBC:OFF_DT],
                                       proj[..., OFF_DT:OFF_POOL], conv0[l], ssm0[l],
                                       P['conv_w'][l], P['conv_b'][l], P['dt_bias'][l],
                                       P['a_log'][l], P['d_skip'][l], P['g_ssd'][l])
        y_b, s_pool = pool_mixer(proj[..., OFF_POOL:OFF_Q], pool0[l], pos0, P['w_pool'][l], P['pool_scale'][l])
        y_c, k_rows, v_rows = diff_attn_mixer(proj[..., OFF_Q:OFF_K], proj[..., OFF_K:OFF_V],
                                              proj[..., OFF_V:OFF_GU], past_k[l], past_v[l],
                                              P['g_q'][l], P['g_k'][l], P['lam'][l], P['g_subln'][l],
                                              P['rel_bias'], lam_init)
        y_d, v_gm = gmlp_mixer(proj[..., OFF_GU:OFF_GV], proj[..., OFF_GV:IN_W],
                               P['g_gv'][l], P['w_sp'][l], P['b_sp'][l])
        merged = 0.0
        for gi, y_br in enumerate((y_a, y_b, y_c, y_d)):
            gate = jax.nn.sigmoid(hn @ P['w_gate'][l, gi] + P['b_gate'][l, gi])
            merged = merged + gate * (y_br @ P['w_branch'][l, gi])
        h = h + merged @ P['w_out'][l]
        h = h + cross_attn(h, mem_k[l], mem_v[l], P['g_x'][l], P['w_xq'][l], P['g_xq'][l], P['w_xo'][l])
        h = h + 0.5 * swiglu_ffn(h, P['g_ffn2'][l], P['w_ffn2_in'][l], P['w_ffn2_out'][l])
        h = rmsnorm(h, P['g_post'][l])
        new_k.append(k_rows)
        new_v.append(v_rows)
        new_ssm.append(s_ssm)
        new_conv.append(s_conv)
        new_pool.append(s_pool)
        if keep_gmlp_rows:
            new_gv.append(v_gm)
    gv = jnp.stack(new_gv) if keep_gmlp_rows else None
    return (h, jnp.stack(new_k), jnp.stack(new_v), jnp.stack(new_ssm), jnp.stack(new_conv),
            jnp.stack(new_pool), gv)


def setup_inputs(seed: int = 0) -> dict:
    key = jax.random.key(seed)
    keys = iter(jax.random.split(key, 64))
    f32 = jnp.float32

    def nrm(shape, scale):
        return jax.random.normal(next(keys), shape, f32) * scale

    def gain(shape):
        return 1.0 + 0.05 * jax.random.normal(next(keys), shape, f32)

    L = DEPTH
    d = D_MODEL
    inp = {}
    inp['x_prompt'] = nrm((BATCH, SEQ, d), 1.0)
    inp['x_sample'] = nrm((DEC_BATCH, DEC_SEQ, d), 1.0)
    inp['cache_attn_k'] = nrm((L, DEC_BATCH, PAST_LEN, ATT_KV, 2, ATT_D), 1.0)
    inp['cache_attn_v'] = nrm((L, DEC_BATCH, PAST_LEN, ATT_KV, ATT_VD), 1.0)
    inp['cache_mem_k'] = nrm((L, DEC_BATCH, MEM_LEN, X_H, X_D), 1.0)
    inp['cache_mem_v'] = nrm((L, DEC_BATCH, MEM_LEN, X_H, X_D), 1.0)
    inp['state_ssm'] = nrm((L, DEC_BATCH, SSM_H, SSM_P, SSM_N), 0.1)
    inp['state_conv'] = nrm((L, DEC_BATCH, CONV_W - 1, XBC_W), 1.0)
    inp['state_pool'] = nrm((L, DEC_BATCH, POOL_BUF, BRANCH_W), 1.0)
    inp['mem_prompt'] = nrm((BATCH, MEM_LEN, d), 1.0)
    inp['g_ffn1'] = gain((L, d))
    inp['w_ffn1_in'] = nrm((L, d, 2 * D_FF), d ** -0.5)
    inp['w_ffn1_out'] = nrm((L, D_FF, d), D_FF ** -0.5)
    inp['g_mix'] = gain((L, d))
    inp['w_in'] = nrm((L, d, IN_W), d ** -0.5)
    inp['conv_w'] = nrm((L, CONV_W, XBC_W), CONV_W ** -0.5)
    inp['conv_b'] = nrm((L, XBC_W), 0.02)
    u = jax.random.uniform(next(keys), (L, SSM_H), f32)
    dt0 = jnp.exp(u * (math.log(DT_MAX) - math.log(DT_MIN)) + math.log(DT_MIN))
    inp['dt_bias'] = dt0 + jnp.log(-jnp.expm1(-dt0))
    inp['a_log'] = jnp.log(jax.random.uniform(next(keys), (L, SSM_H), f32, 1.0, 16.0))
    inp['d_skip'] = gain((L, SSM_H))
    inp['g_ssd'] = gain((L, SSM_D))
    inp['w_pool'] = nrm((L, len(POOL_WINDOWS), POOL_GW, POOL_GW), POOL_GW ** -0.5)
    inp['pool_scale'] = gain((L, BRANCH_W))
    inp['g_q'] = gain((L, ATT_D))
    inp['g_k'] = gain((L, ATT_D))
    inp['lam'] = nrm((L, 4, ATT_D), 0.1)
    inp['g_subln'] = gain((L, ATT_VD))
    inp['rel_bias'] = nrm((N_BUCKETS, ATT_H), 0.5)
    inp['g_gv'] = gain((L, BRANCH_W))
    inp['w_sp'] = nrm((L, GM_G, GM_CHUNK, GM_CHUNK), GM_CHUNK ** -0.5)
    inp['b_sp'] = gain((L, GM_G, GM_CHUNK))
    inp['w_gate'] = nrm((L, N_BRANCH, d, d), d ** -0.5)
    inp['b_gate'] = nrm((L, N_BRANCH, d), 0.02)
    inp['w_branch'] = nrm((L, N_BRANCH, BRANCH_W, d), BRANCH_W ** -0.5)
    inp['w_out'] = nrm((L, d, d), d ** -0.5)
    inp['g_x'] = gain((L, d))
    inp['w_xq'] = nrm((L, d, X_H * X_D), d ** -0.5)
    inp['g_xq'] = gain((L, X_D))
    inp['g_mem'] = gain((L, d))
    inp['w_mk'] = nrm((L, d, X_H * X_D), d ** -0.5)
    inp['w_mv'] = nrm((L, d, X_H * X_D), d ** -0.5)
    inp['g_xk'] = gain((L, X_D))
    inp['w_xo'] = nrm((L, X_H * X_D, d), (X_H * X_D) ** -0.5)
    inp['g_ffn2'] = gain((L, d))
    inp['w_ffn2_in'] = nrm((L, d, 2 * D_FF), d ** -0.5)
    inp['w_ffn2_out'] = nrm((L, D_FF, d), D_FF ** -0.5)
    inp['g_post'] = gain((L, d))
    return inp


def reference(x_prompt, x_sample, cache_attn_k, cache_attn_v, cache_mem_k, cache_mem_v, state_ssm,
              state_conv, state_pool, mem_prompt, g_ffn1, w_ffn1_in, w_ffn1_out, g_mix, w_in, conv_w,
              conv_b, dt_bias, a_log, d_skip, g_ssd, w_pool, pool_scale, g_q, g_k, lam, g_subln, rel_bias,
              g_gv, w_sp, b_sp, w_gate, b_gate, w_branch, w_out, g_x, w_xq, g_xq, g_mem, w_mk, w_mv, g_xk,
              w_xo, g_ffn2, w_ffn2_in, w_ffn2_out, g_post):
    P = dict(g_ffn1=g_ffn1, w_ffn1_in=w_ffn1_in, w_ffn1_out=w_ffn1_out, g_mix=g_mix, w_in=w_in,
             conv_w=conv_w, conv_b=conv_b, dt_bias=dt_bias, a_log=a_log, d_skip=d_skip, g_ssd=g_ssd,
             w_pool=w_pool, pool_scale=pool_scale, g_q=g_q, g_k=g_k, lam=lam, g_subln=g_subln,
             rel_bias=rel_bias, g_gv=g_gv, w_sp=w_sp, b_sp=b_sp, w_gate=w_gate, b_gate=b_gate,
             w_branch=w_branch, w_out=w_out, g_x=g_x, w_xq=w_xq, g_xq=g_xq, w_xo=w_xo,
             g_ffn2=g_ffn2, w_ffn2_in=w_ffn2_in, w_ffn2_out=w_ffn2_out, g_post=g_post)

    mks, mvs = [], []
    for l in range(DEPTH):
        mk_l, mv_l = mem_kv(mem_prompt, g_mem[l], w_mk[l], w_mv[l], g_xk[l])
        mks.append(mk_l)
        mvs.append(mv_l)
    p_mem_k = jnp.stack(mks)
    p_mem_v = jnp.stack(mvs)
    b = x_prompt.shape[0]
    dt = x_prompt.dtype
    empty_k = jnp.zeros((DEPTH, b, 0, ATT_KV, 2, ATT_D), dt)
    empty_v = jnp.zeros((DEPTH, b, 0, ATT_KV, ATT_VD), dt)
    ssm_zero = jnp.zeros((DEPTH, b, SSM_H, SSM_P, SSM_N), jnp.float32)
    conv_zero = jnp.zeros((DEPTH, b, CONV_W - 1, XBC_W), dt)
    pool_zero = jnp.zeros((DEPTH, b, POOL_BUF, BRANCH_W), dt)
    y_prompt, p_attn_k, p_attn_v, p_ssm, p_conv, p_pool, _ = run_trunk(
        x_prompt, empty_k, empty_v, p_mem_k, p_mem_v, ssm_zero, conv_zero, pool_zero, 0, P, False)

    past_len = cache_attn_k.shape[2]
    y_sample, s_attn_k, s_attn_v, s_ssm, s_conv, s_pool, s_gmlp_v = run_trunk(
        x_sample, cache_attn_k, cache_attn_v, cache_mem_k, cache_mem_v, state_ssm, state_conv,
        state_pool, past_len, P, True)

    return (y_prompt, y_sample, p_attn_k, p_attn_v, p_mem_k, p_mem_v, p_ssm, p_conv, p_pool,
            s_attn_k, s_attn_v, s_ssm, s_conv, s_pool, s_gmlp_v)
```

```python
import functools
import math

import jax
import jax.numpy as jnp
import numpy as np
from jax import lax
from jax.experimental import pallas as pl
from jax.experimental.pallas import tpu as pltpu

F32 = jnp.float32
BF16 = jnp.bfloat16
EPS = 1e-6
NEG = -1e30

VMEM_LIMIT_BYTES = 56 * 1024 * 1024
LANES = 128

BRANCH_W = 1024
SSM_P = 64
SSM_H = 16
SSM_G = 4
SSM_N = 128
SSM_D = SSM_H * SSM_P
CONV_W = 4
XBC_W = SSM_D + 2 * SSM_G * SSM_N
POOL_WINDOWS = (2, 4, 8, 16)
POOL_GW = BRANCH_W // len(POOL_WINDOWS)
POOL_BUF = max(POOL_WINDOWS) - 1
ATT_H = 8
ATT_KV = 4
ATT_D = 64
ATT_VD = 128
CHUNK = 64
N_BUCKETS = 32
MAX_DIST = 128
GM_CHUNK = 128
GM_G = 4
GM_GW = BRANCH_W // GM_G
X_H = 4
X_D = 128

P_XBC = 0
P_Z = 2048
P_POOL = 3072
P_Q = 4096
P_GU = 5120
P_GV = 6144
P_K = 7168
P_V = 7680
P_DT = 8192
P_W = 8320


def _cparams(sem):
    return pltpu.CompilerParams(dimension_semantics=sem, vmem_limit_bytes=VMEM_LIMIT_BYTES)


def _rms(xf, g):
    ms = jnp.mean(xf * xf, axis=-1, keepdims=True)
    return xf * lax.rsqrt(ms + EPS) * g


def _silu(x):
    return x * jax.nn.sigmoid(x)


def _mm_kernel(*refs, norm, head_norm, residual):
    it = iter(refs)
    x_ref = next(it)
    g_ref = next(it) if norm else None
    w_ref = next(it)
    gh_ref = next(it) if head_norm else None
    res_ref = next(it) if residual else None
    o_ref = next(it)
    xn_ref = next(it) if norm else None

    if norm:
        @pl.when(pl.program_id(1) == 0)
        def _():
            xn_ref[...] = _rms(x_ref[...].astype(F32), g_ref[...]).astype(BF16)
        xb = xn_ref[...]
    else:
        xb = x_ref[...].astype(BF16)
    acc = jnp.dot(xb, w_ref[...], preferred_element_type=F32)
    if head_norm:
        parts = []
        for c in range(acc.shape[1] // LANES):
            parts.append(_rms(acc[:, c * LANES:(c + 1) * LANES], gh_ref[...]))
        acc = jnp.concatenate(parts, axis=-1)
    if residual:
        acc = res_ref[...] + acc
    o_ref[...] = acc.astype(o_ref.dtype)


def _mm(x, w, *, g=None, gh=None, res=None, out_dtype, tm, tn, name):
    m, k = x.shape
    n = w.shape[1]
    tm = min(tm, m)
    tn = min(tn, n)
    assert m % tm == 0 and n % tn == 0
    norm, head_norm, residual = g is not None, gh is not None, res is not None
    args, specs = [x], [pl.BlockSpec((tm, k), lambda i, j: (i, 0))]
    if norm:
        args.append(g.reshape(1, k))
        specs.append(pl.BlockSpec((1, k), lambda i, j: (0, 0)))
    args.append(w)
    specs.append(pl.BlockSpec((k, tn), lambda i, j: (0, j)))
    if head_norm:
        args.append(gh.reshape(1, LANES))
        specs.append(pl.BlockSpec((1, LANES), lambda i, j: (0, 0)))
    if residual:
        args.append(res)
        specs.append(pl.BlockSpec((tm, tn), lambda i, j: (i, j)))
    scratch = [pltpu.VMEM((tm, k), BF16)] if norm else []
    return pl.pallas_call(
        functools.partial(_mm_kernel, norm=norm, head_norm=head_norm, residual=residual),
        out_shape=jax.ShapeDtypeStruct((m, n), out_dtype),
        grid=(m // tm, n // tn),
        in_specs=specs,
        out_specs=pl.BlockSpec((tm, tn), lambda i, j: (i, j)),
        scratch_shapes=scratch,
        compiler_params=_cparams(("parallel", "arbitrary")),
        name=name,
    )(*args)


def _ffn_kernel(*refs, post):
    if post:
        x_ref, g_ref, wg_ref, wu_ref, wo_ref, gp_ref, o_ref, xn_ref, acc_ref = refs
    else:
        x_ref, g_ref, wg_ref, wu_ref, wo_ref, o_ref, xn_ref, acc_ref = refs
    f = pl.program_id(1)

    @pl.when(f == 0)
    def _():
        xn_ref[...] = _rms(x_ref[...], g_ref[...]).astype(BF16)
        acc_ref[...] = jnp.zeros_like(acc_ref)

    xb = xn_ref[...]
    gate = jnp.dot(xb, wg_ref[...], preferred_element_type=F32)
    up = jnp.dot(xb, wu_ref[...], preferred_element_type=F32)
    mid = (_silu(gate) * up).astype(BF16)
    acc_ref[...] += jnp.dot(mid, wo_ref[...], preferred_element_type=F32)

    @pl.when(f == pl.num_programs(1) - 1)
    def _():
        h = x_ref[...] + 0.5 * acc_ref[...]
        if post:
            h = _rms(h, gp_ref[...])
        o_ref[...] = h


def _ffn(x, g, w_in, w_out, *, post_g=None, tm=512, tf=512):
    m, d = x.shape
    ff = w_out.shape[0]
    tm = min(tm, m)
    assert m % tm == 0 and ff % tf == 0
    nf = ff // tf
    post = post_g is not None
    args = [x, g.reshape(1, d), w_in, w_in, w_out]
    specs = [
        pl.BlockSpec((tm, d), lambda i, f: (i, 0)),
        pl.BlockSpec((1, d), lambda i, f: (0, 0)),
        pl.BlockSpec((d, tf), lambda i, f: (0, f)),
        pl.BlockSpec((d, tf), lambda i, f: (0, nf + f)),
        pl.BlockSpec((tf, d), lambda i, f: (f, 0)),
    ]
    if post:
        args.append(post_g.reshape(1, d))
        specs.append(pl.BlockSpec((1, d), lambda i, f: (0, 0)))
    return pl.pallas_call(
        functools.partial(_ffn_kernel, post=post),
        out_shape=jax.ShapeDtypeStruct((m, d), F32),
        grid=(m // tm, nf),
        in_specs=specs,
        out_specs=pl.BlockSpec((tm, d), lambda i, f: (i, 0)),
        scratch_shapes=[pltpu.VMEM((tm, d), BF16), pltpu.VMEM((tm, d), F32)],
        compiler_params=_cparams(("parallel", "arbitrary")),
        name="swiglu_ffn",
    )(*args)


def _merge_kernel(h_ref, g_ref, ya_ref, yb_ref, yc_ref, yd_ref, wg_ref, bg_ref, wb_ref, o_ref,
                  hn_ref, acc_ref):
    j = pl.program_id(1)
    br = pl.program_id(2)

    @pl.when((j == 0) & (br == 0))
    def _():
        hn_ref[...] = _rms(h_ref[...], g_ref[...]).astype(BF16)

    gate = jax.nn.sigmoid(jnp.dot(hn_ref[...], wg_ref[...], preferred_element_type=F32) + bg_ref[...])
    for k, y_ref in enumerate((ya_ref, yb_ref, yc_ref, yd_ref)):
        @pl.when(br == k)
        def _(k=k, y_ref=y_ref):
            contrib = gate * jnp.dot(y_ref[...], wb_ref[...], preferred_element_type=F32)
            if k == 0:
                acc_ref[...] = contrib
            else:
                acc_ref[...] += contrib

    @pl.when(br == pl.num_programs(2) - 1)
    def _():
        o_ref[...] = acc_ref[...].astype(o_ref.dtype)


def _merge(h, g, ys, w_gate, b_gate, w_branch, *, tm=512, tn=1024):
    m, d = h.shape
    nb, bw, _ = w_branch.shape
    tm = min(tm, m)
    assert m % tm == 0 and d % tn == 0 and nb == 4
    yspec = pl.BlockSpec((tm, bw), lambda i, j, b: (i, 0))
    return pl.pallas_call(
        _merge_kernel,
        out_shape=jax.ShapeDtypeStruct((m, d), BF16),
        grid=(m // tm, d // tn, nb),
        in_specs=[
            pl.BlockSpec((tm, d), lambda i, j, b: (i, 0)),
            pl.BlockSpec((1, d), lambda i, j, b: (0, 0)),
            yspec, yspec, yspec, yspec,
            pl.BlockSpec((None, d, tn), lambda i, j, b: (b, 0, j)),
            pl.BlockSpec((None, 1, tn), lambda i, j, b: (b, 0, j)),
            pl.BlockSpec((None, bw, tn), lambda i, j, b: (b, 0, j)),
        ],
        out_specs=pl.BlockSpec((tm, tn), lambda i, j, b: (i, j)),
        scratch_shapes=[pltpu.VMEM((tm, d), BF16), pltpu.VMEM((tm, tn), F32)],
        compiler_params=_cparams(("parallel", "arbitrary", "arbitrary")),
        name="branch_merge",
    )(h, g.reshape(1, d), *ys, w_gate, b_gate.reshape(nb, 1, d), w_branch)


def _ssd_kernel(*refs, q, has_state):
    it = iter(refs)
    xbc_ref, z_ref, dt_ref = next(it), next(it), next(it)
    conv0_ref = next(it) if has_state else None
    s0_ref = next(it) if has_state else None
    cw_ref, cb_ref, dtb_ref, alog_ref, dskip_ref, gssd_ref = (next(it) for _ in range(6))
    y_ref, s_ref, cbuf = next(it), next(it), next(it)
    hist = 8

    @pl.when(pl.program_id(1) == 0)
    def _():
        cbuf[0:hist, :] = jnp.zeros((hist, XBC_W), F32)
        if has_state:
            cbuf[hist - (CONV_W - 1):hist, :] = conv0_ref[0]
            s_ref[0] = s0_ref[0]
        else:
            s_ref[0] = jnp.zeros(s_ref.shape[1:], F32)

    cbuf[hist:hist + q, :] = xbc_ref[0]
    conv = cb_ref[...]
    for k in range(CONV_W):
        lo = hist - (CONV_W - 1) + k
        conv = conv + cbuf[lo:lo + q, :] * cw_ref[k:k + 1, :]
    cbuf[0:hist, :] = cbuf[q:q + hist, :]
    xc = _silu(conv)
    xs = xc[:, :SSM_D]
    bm = xc[:, SSM_D:SSM_D + SSM_G * SSM_N]
    cm = xc[:, SSM_D + SSM_G * SSM_N:]

    dtl = dt_ref[0] + dtb_ref[...]
    dt = jnp.maximum(dtl, 0.0) + jnp.log1p(jnp.exp(-jnp.abs(dtl)))
    a = dt * (-jnp.exp(alog_ref[...]))
    ri = lax.broadcasted_iota(jnp.int32, (q, q), 0)
    ci = lax.broadcasted_iota(jnp.int32, (q, q), 1)
    tril = ri >= ci
    cum = jnp.dot(tril.astype(F32), a, preferred_element_type=F32,
                  precision=lax.Precision.HIGHEST)
    cum_t = cum.T
    dt_t = dt.T
    cum_last = cum[q - 1:q, :]
    w_state = jnp.exp(cum_last - cum) * dt
    ecum = jnp.exp(cum)
    ecl = jnp.exp(cum_last)

    lane_lo = lax.broadcasted_iota(jnp.int32, (q, LANES), 1) < SSM_P
    row_lo_t = lax.broadcasted_iota(jnp.int32, (LANES, q), 0) < SSM_P
    row_lo_s = lax.broadcasted_iota(jnp.int32, (LANES, SSM_N), 0) < SSM_P
    dn_t = (((1,), (1,)), ((), ()))

    y_parts = []
    cb = None
    heads_per_group = SSM_H // SSM_G
    for pp in range(SSM_H // 2):
        grp = (2 * pp) // heads_per_group
        xp = xs[:, pp * LANES:(pp + 1) * LANES]
        bg = bm[:, grp * SSM_N:(grp + 1) * SSM_N]
        cg = cm[:, grp * SSM_N:(grp + 1) * SSM_N]
        cg_b = cg.astype(BF16)
        if (2 * pp) % heads_per_group == 0:
            cb = lax.dot_general(cg_b, bg.astype(BF16), dn_t, preferred_element_type=F32)
        xp_t = xp.T
        s_pair = s_ref[0, pp * LANES:(pp + 1) * LANES, :]
        y_pair = None
        s_new = None
        for hh in range(2):
            hd = 2 * pp + hh
            seg = cum[:, hd:hd + 1] - cum_t[hd:hd + 1, :]
            mat = jnp.exp(jnp.where(tril, seg, NEG)) * cb * dt_t[hd:hd + 1, :]
            sel = lane_lo if hh == 0 else jnp.logical_not(lane_lo)
            x_h = jnp.where(sel, xp, 0.0).astype(BF16)
            yd = jnp.dot(mat.astype(BF16), x_h, preferred_element_type=F32)
            y_pair = yd if y_pair is None else y_pair + yd
            sel_t = row_lo_t if hh == 0 else jnp.logical_not(row_lo_t)
            xt_h = jnp.where(sel_t, xp_t, 0.0).astype(BF16)
            bs = (bg * w_state[:, hd:hd + 1]).astype(BF16)
            sn = jnp.dot(xt_h, bs, preferred_element_type=F32)
            s_new = sn if s_new is None else s_new + sn
        y_off = lax.dot_general(cg_b, s_pair.astype(BF16), dn_t, preferred_element_type=F32)
        e_pair = jnp.where(lane_lo, ecum[:, 2 * pp:2 * pp + 1], ecum[:, 2 * pp + 1:2 * pp + 2])
        y_parts.append(y_pair + y_off * e_pair)
        dec = jnp.where(row_lo_s, ecl[:, 2 * pp:2 * pp + 1], ecl[:, 2 * pp + 1:2 * pp + 2])
        s_ref[0, pp * LANES:(pp + 1) * LANES, :] = dec * s_pair + s_new

    y = jnp.concatenate(y_parts, axis=-1)
    y = y + dskip_ref[...] * xs
    y = y * _silu(z_ref[0])
    gw = SSM_D // SSM_G
    outs = [_rms(y[:, gi * gw:(gi + 1) * gw], gssd_ref[:, gi * gw:(gi + 1) * gw]) for gi in range(SSM_G)]
    y_ref[0] = jnp.concatenate(outs, axis=-1).astype(y_ref.dtype)


def _ssd(proj, conv0, s0, cw, cb, dtb, alog, dskip, gssd, *, q):
    b, l, _ = proj.shape
    q = min(q, l)
    assert l % q == 0 and q >= 8
    has_state = s0 is not None
    args = [proj, proj, proj]
    specs = [
        pl.BlockSpec((1, q, XBC_W), lambda i, c: (i, c, P_XBC // XBC_W)),
        pl.BlockSpec((1, q, SSM_D), lambda i, c: (i, c, P_Z // SSM_D)),
        pl.BlockSpec((1, q, LANES), lambda i, c: (i, c, P_DT // LANES)),
    ]
    if has_state:
        args += [conv0, s0.reshape(b, SSM_H * SSM_P, SSM_N)]
        specs += [
            pl.BlockSpec((1, CONV_W - 1, XBC_W), lambda i, c: (i, 0, 0)),
            pl.BlockSpec((1, SSM_H * SSM_P, SSM_N), lambda i, c: (i, 0, 0)),
        ]
    args += [cw, cb, dtb, alog, dskip, gssd]
    specs += [
        pl.BlockSpec((CONV_W, XBC_W), lambda i, c: (0, 0)),
        pl.BlockSpec((1, XBC_W), lambda i, c: (0, 0)),
        pl.BlockSpec((1, LANES), lambda i, c: (0, 0)),
        pl.BlockSpec((1, LANES), lambda i, c: (0, 0)),
        pl.BlockSpec((1, SSM_D), lambda i, c: (0, 0)),
        pl.BlockSpec((1, SSM_D), lambda i, c: (0, 0)),
    ]
    y, s = pl.pallas_call(
        functools.partial(_ssd_kernel, q=q, has_state=has_state),
        out_shape=(jax.ShapeDtypeStruct((b, l, SSM_D), BF16),
                   jax.ShapeDtypeStruct((b, SSM_H * SSM_P, SSM_N), F32)),
        grid=(b, l // q),
        in_specs=specs,
        out_specs=(pl.BlockSpec((1, q, SSM_D), lambda i, c: (i, c, 0)),
                   pl.BlockSpec((1, SSM_H * SSM_P, SSM_N), lambda i, c: (i, 0, 0))),
        scratch_shapes=[pltpu.VMEM((q + 8, XBC_W), F32)],
        compiler_params=_cparams(("parallel", "arbitrary")),
        name="ssd_mixer",
    )(*args)
    return y, s.reshape(b, SSM_H, SSM_P, SSM_N)


def _pool_kernel(*refs, t, pos0, has_state):
    it = iter(refs)
    xp_ref = next(it)
    p0_ref = next(it) if has_state else None
    w_ref, sc_ref, y_ref, buf = next(it), next(it), next(it), next(it)
    hist = POOL_BUF + 1
    ti = pl.program_id(1)

    @pl.when(ti == 0)
    def _():
        buf[0:hist, :] = jnp.zeros((hist, BRANCH_W), F32)
        if has_state:
            buf[1:hist, :] = p0_ref[0]

    xp = xp_ref[0]
    buf[hist:hist + t, :] = xp
    pos = pos0 + ti * t + lax.broadcasted_iota(jnp.int32, (t, POOL_GW), 0)
    outs = []
    for gi, w in enumerate(POOL_WINDOWS):
        c0 = gi * POOL_GW
        s = xp[:, c0:c0 + POOL_GW]
        for k in range(1, w):
            s = s + buf[hist - k:hist - k + t, c0:c0 + POOL_GW]
        cnt = jnp.minimum(w, pos + 1).astype(F32)
        pooled = s / cnt - xp[:, c0:c0 + POOL_GW]
        y = jnp.dot(pooled.astype(BF16), w_ref[gi], preferred_element_type=F32)
        outs.append(y * sc_ref[:, c0:c0 + POOL_GW])
    y_ref[0] = jnp.concatenate(outs, axis=-1).astype(y_ref.dtype)
    buf[0:hist, :] = buf[t:t + hist, :]


def _pool(proj, pool0, pos0, w_pool, scale, *, t):
    b, l, _ = proj.shape
    t = min(t, l)
    assert l % t == 0 and t >= POOL_BUF + 1
    has_state = pool0 is not None
    args = [proj]
    specs = [pl.BlockSpec((1, t, BRANCH_W), lambda i, c: (i, c, P_POOL // BRANCH_W))]
    if has_state:
        args.append(pool0)
        specs.append(pl.BlockSpec((1, POOL_BUF, BRANCH_W), lambda i, c: (i, 0, 0)))
    args += [w_pool, scale]
    specs += [
        pl.BlockSpec((len(POOL_WINDOWS), POOL_GW, POOL_GW), lambda i, c: (0, 0, 0)),
        pl.BlockSpec((1, BRANCH_W), lambda i, c: (0, 0)),
    ]
    return pl.pallas_call(
        functools.partial(_pool_kernel, t=t, pos0=pos0, has_state=has_state),
        out_shape=jax.ShapeDtypeStruct((b, l, BRANCH_W), BF16),
        grid=(b, l // t),
        in_specs=specs,
        out_specs=pl.BlockSpec((1, t, BRANCH_W), lambda i, c: (i, c, 0)),
        scratch_shapes=[pltpu.VMEM((t + POOL_BUF + 1, BRANCH_W), F32)],
        compiler_params=_cparams(("parallel", "arbitrary")),
        name="pool_mixer",
    )(*args)


def _gmlp_kernel(u_ref, v_ref, g_ref, w_ref, b_ref, y_ref, vn_ref, *, cl):
    u = jax.nn.gelu(u_ref[0], approximate=True)
    vn = _rms(jax.nn.gelu(v_ref[0], approximate=True), g_ref[...])
    vn_ref[0] = vn
    tril = lax.broadcasted_iota(jnp.int32, (cl, cl), 0) >= lax.broadcasted_iota(jnp.int32, (cl, cl), 1)
    outs = []
    for gi in range(GM_G):
        w = jnp.where(tril, w_ref[gi], 0.0).astype(BF16)
        s = jnp.dot(w, vn[:, gi * GM_GW:(gi + 1) * GM_GW].astype(BF16), preferred_element_type=F32)
        s = s + b_ref[:, gi:gi + 1]
        outs.append(u[:, gi * GM_GW:(gi + 1) * GM_GW] * s)
    y_ref[0] = jnp.concatenate(outs, axis=-1).astype(y_ref.dtype)


def _gmlp(proj, g_gv, w_sp, b_sp):
    b, l, _ = proj.shape
    cl = min(GM_CHUNK, l)
    assert l % cl == 0
    w = w_sp[:, :cl, :cl]
    bt = b_sp[:, :cl].T
    return pl.pallas_call(
        functools.partial(_gmlp_kernel, cl=cl),
        out_shape=(jax.ShapeDtypeStruct((b, l, BRANCH_W), BF16),
                   jax.ShapeDtypeStruct((b, l, BRANCH_W), F32)),
        grid=(b, l // cl),
        in_specs=[
            pl.BlockSpec((1, cl, BRANCH_W), lambda i, c: (i, c, P_GU // BRANCH_W)),
            pl.BlockSpec((1, cl, BRANCH_W), lambda i, c: (i, c, P_GV // BRANCH_W)),
            pl.BlockSpec((1, BRANCH_W), lambda i, c: (0, 0)),
            pl.BlockSpec((GM_G, cl, cl), lambda i, c: (0, 0, 0)),
            pl.BlockSpec((cl, GM_G), lambda i, c: (0, 0)),
        ],
        out_specs=(pl.BlockSpec((1, cl, BRANCH_W), lambda i, c: (i, c, 0)),
                   pl.BlockSpec((1, cl, BRANCH_W), lambda i, c: (i, c, 0))),
        compiler_params=_cparams(("parallel", "parallel")),
        name="gmlp_mixer",
    )(proj, proj, g_gv, w, bt)


def _norm_halves(x, g):
    lo = lax.broadcasted_iota(jnp.int32, x.shape, 1) < ATT_D
    xx = x * x
    s_lo = jnp.sum(jnp.where(lo, xx, 0.0), axis=-1, keepdims=True)
    s_hi = jnp.sum(jnp.where(lo, 0.0, xx), axis=-1, keepdims=True)
    ms = jnp.where(lo, s_lo, s_hi) * (1.0 / ATT_D)
    return x * lax.rsqrt(ms + EPS) * g


def _qknorm_kernel(q_ref, k_ref, v_ref, gq_ref, gk_ref, qn_ref, kn_ref, knb_ref, vb_ref):
    scale = ATT_D ** -0.5
    q = q_ref[0]
    qs = [_norm_halves(q[:, c * LANES:(c + 1) * LANES], gq_ref[...]) * scale
          for c in range(q.shape[1] // LANES)]
    qn_ref[0] = jnp.concatenate(qs, axis=-1).astype(qn_ref.dtype)
    k = k_ref[0]
    ks = [_norm_halves(k[:, c * LANES:(c + 1) * LANES], gk_ref[...]) for c in range(k.shape[1] // LANES)]
    kn = jnp.concatenate(ks, axis=-1)
    kn_ref[0] = kn
    knb_ref[0] = kn.astype(BF16)
    vb_ref[0] = v_ref[0].astype(BF16)


def _qknorm(proj, g_q, g_k, *, t):
    b, l, _ = proj.shape
    t = min(t, l)
    assert l % t == 0
    qw, kw = ATT_H * 2 * ATT_D, ATT_KV * 2 * ATT_D
    gq = jnp.tile(g_q, 2).reshape(1, LANES)
    gk = jnp.tile(g_k, 2).reshape(1, LANES)
    return pl.pallas_call(
        _qknorm_kernel,
        out_shape=(jax.ShapeDtypeStruct((b, l, qw), BF16), jax.ShapeDtypeStruct((b, l, kw), F32),
                   jax.ShapeDtypeStruct((b, l, kw), BF16), jax.ShapeDtypeStruct((b, l, kw), BF16)),
        grid=(b, l // t),
        in_specs=[
            pl.BlockSpec((1, t, qw), lambda i, c: (i, c, P_Q // qw)),
            pl.BlockSpec((1, t, kw), lambda i, c: (i, c, P_K // kw)),
            pl.BlockSpec((1, t, kw), lambda i, c: (i, c, P_V // kw)),
            pl.BlockSpec((1, LANES), lambda i, c: (0, 0)),
            pl.BlockSpec((1, LANES), lambda i, c: (0, 0)),
        ],
        out_specs=(pl.BlockSpec((1, t, qw), lambda i, c: (i, c, 0)),
                   pl.BlockSpec((1, t, kw), lambda i, c: (i, c, 0)),
                   pl.BlockSpec((1, t, kw), lambda i, c: (i, c, 0)),
                   pl.BlockSpec((1, t, kw), lambda i, c: (i, c, 0))),
        compiler_params=_cparams(("parallel", "parallel")),
        name="qk_norm",
    )(proj, proj, proj, gq, gk)


def _bias_kernel(off_ref, tbl_ref, o_ref, *, tq, tk):
    c = pl.program_id(0)
    h = pl.program_id(1)
    off = off_ref[c]
    row = lax.broadcasted_iota(jnp.int32, (tq, tk), 0)
    col = lax.broadcasted_iota(jnp.int32, (tq, tk), 1)
    rel = off + col - row
    nb = N_BUCKETS // 2
    max_exact = nb // 2
    n = jnp.abs(rel)
    large = max_exact + (jnp.log(jnp.maximum(n, 1).astype(F32) / max_exact)
                         / math.log(MAX_DIST / max_exact) * (nb - max_exact)).astype(jnp.int32)
    large = jnp.minimum(large, nb - 1)
    bucket = jnp.where(rel > 0, nb, 0) + jnp.where(n < max_exact, n, large)
    val = jnp.zeros((tq, tk), F32)
    for bk in range(N_BUCKETS):
        val = jnp.where(bucket == bk, tbl_ref[bk * ATT_H + h], val)
    visible = jnp.right_shift(off + col, 6) <= jnp.right_shift(row, 6)
    o_ref[0, 0] = jnp.where(visible, val, NEG)


def _bias_tiles(rel_bias, offsets, tq, tk):
    assert CHUNK == 64
    offs = jnp.asarray(np.asarray(offsets, np.int32))
    n_cls = len(offsets)
    return pl.pallas_call(
        functools.partial(_bias_kernel, tq=tq, tk=tk),
        out_shape=jax.ShapeDtypeStruct((n_cls, ATT_H, tq, tk), F32),
        grid=(n_cls, ATT_H),
        in_specs=[pl.BlockSpec(memory_space=pltpu.SMEM), pl.BlockSpec(memory_space=pltpu.SMEM)],
        out_specs=pl.BlockSpec((1, 1, tq, tk), lambda c, h: (c, h, 0, 0)),
        compiler_params=_cparams(("parallel", "parallel")),
        name="rel_bias_tiles",
    )(offs, rel_bias.reshape(-1))


FAR_OFFSET = -(1 << 20)


def _is_far(off, tq, tk):
    nb = N_BUCKETS // 2
    max_rel = off + tk - 1
    if max_rel >= 0:
        return False
    n_min = -max_rel
    sat = (nb // 2) * (MAX_DIST / (nb // 2)) ** ((nb - 1 - nb // 2) / (nb - nb // 2))
    return n_min >= math.ceil(sat) + 1


def _q4(q2):
    lo = lax.broadcasted_iota(jnp.int32, (q2.shape[0], LANES), 1) < ATT_D
    parts = []
    for r in range(2):
        qr = q2[:, r * LANES:(r + 1) * LANES]
        parts.append(jnp.where(lo, qr, jnp.zeros_like(qr)))
        parts.append(jnp.where(lo, jnp.zeros_like(qr), qr))
    return jnp.concatenate(parts, axis=0)


def _osm_step(q4, k, v, b0, b1, m_ref, l_ref, acc_ref):
    dn_t = (((1,), (1,)), ((), ()))
    s = lax.dot_general(q4, k, dn_t, preferred_element_type=F32)
    s = s + jnp.concatenate([b0, b0, b1, b1], axis=0)
    m_prev = m_ref[...]
    m_new = jnp.maximum(m_prev, jnp.max(s, axis=-1, keepdims=True))
    alpha = jnp.exp(m_prev - m_new)
    p = jnp.exp(s - m_new[:, 0:1])
    l_ref[...] = alpha * l_ref[...] + jnp.sum(p, axis=-1, keepdims=True)
    acc_ref[...] = alpha * acc_ref[...] + jnp.dot(p.astype(BF16), v, preferred_element_type=F32)
    m_ref[...] = m_new


def _osm_finish(l_ref, acc_ref, lam_ref, gs_ref, lam_init, tq):
    lp = lam_ref[...]
    lam = (jnp.exp(jnp.sum(lp[0:1] * lp[1:2], axis=-1, keepdims=True))
           - jnp.exp(jnp.sum(lp[2:3] * lp[3:4], axis=-1, keepdims=True)) + lam_init)
    o = acc_ref[...] / l_ref[...]
    outs = []
    for r in range(2):
        d = o[(2 * r) * tq:(2 * r + 1) * tq] - lam * o[(2 * r + 1) * tq:(2 * r + 2) * tq]
        outs.append(_rms(d, gs_ref[...]) * (1.0 - lam_init))
    return jnp.concatenate(outs, axis=-1)


def _attn_self_kernel(q_ref, k_ref, v_ref, bias_ref, lam_ref, gs_ref, o_ref, m_ref, l_ref, acc_ref,
                      *, tq, tk, lam_init, n_cls):
    i = pl.program_id(2)
    m_ref[...] = jnp.full(m_ref.shape, NEG, F32)
    l_ref[...] = jnp.zeros(l_ref.shape, F32)
    acc_ref[...] = jnp.zeros(acc_ref.shape, F32)
    q4 = _q4(q_ref[0])

    def body(j, carry):
        ks = pl.multiple_of(j * tk, tk)
        cls = jnp.minimum(i - j, n_cls - 1)
        _osm_step(q4, k_ref[0, pl.ds(ks, tk), :], v_ref[0, pl.ds(ks, tk), :],
                  bias_ref[cls, 0], bias_ref[cls, 1], m_ref, l_ref, acc_ref)
        return carry

    lax.fori_loop(0, i + 1, body, 0)
    o_ref[0] = _osm_finish(l_ref, acc_ref, lam_ref, gs_ref, lam_init, tq).astype(o_ref.dtype)


def _attn_self(qn, knb, vb, bias, lam_p, g_subln, lam_init, *, tq):
    b, l, _ = qn.shape
    n_cls = bias.shape[0]
    hw = 2 * ATT_VD
    return pl.pallas_call(
        functools.partial(_attn_self_kernel, tq=tq, tk=tq, lam_init=lam_init, n_cls=n_cls),
        out_shape=jax.ShapeDtypeStruct((b, l, ATT_H * ATT_VD), BF16),
        grid=(b, ATT_KV, l // tq),
        in_specs=[
            pl.BlockSpec((1, tq, hw), lambda bi, kv, i: (bi, i, kv)),
            pl.BlockSpec((1, l, LANES), lambda bi, kv, i: (bi, 0, kv)),
            pl.BlockSpec((1, l, LANES), lambda bi, kv, i: (bi, 0, kv)),
            pl.BlockSpec((n_cls, 2, tq, tq), lambda bi, kv, i: (0, kv, 0, 0)),
            pl.BlockSpec((4, ATT_D), lambda bi, kv, i: (0, 0)),
            pl.BlockSpec((1, ATT_VD), lambda bi, kv, i: (0, 0)),
        ],
        out_specs=pl.BlockSpec((1, tq, hw), lambda bi, kv, i: (bi, i, kv)),
        scratch_shapes=[pltpu.VMEM((4 * tq, LANES), F32), pltpu.VMEM((4 * tq, LANES), F32),
                        pltpu.VMEM((4 * tq, ATT_VD), F32)],
        compiler_params=_cparams(("parallel", "parallel", "arbitrary")),
        name="diff_attn_self",
    )(qn, knb, vb, bias, lam_p, g_subln.reshape(1, ATT_VD))


def _attn_cached_kernel(cls_ref, q_ref, ck_ref, cv_ref, kn_ref, vn_ref, bp_ref, bn_ref, lam_ref, gs_ref,
                        o_ref, m_ref, l_ref, acc_ref, *, tq, nt, lam_init):
    del cls_ref
    j = pl.program_id(1)

    @pl.when(j == 0)
    def _():
        m_ref[...] = jnp.full(m_ref.shape, NEG, F32)
        l_ref[...] = jnp.zeros(l_ref.shape, F32)
        acc_ref[...] = jnp.zeros(acc_ref.shape, F32)

    def update(k_all, v_all, bias_at):
        for kv in range(ATT_KV):
            q4 = _q4(q_ref[0, :, kv * 2 * LANES:(kv + 1) * 2 * LANES])
            _osm_step(q4, k_all[:, kv * LANES:(kv + 1) * LANES].astype(BF16),
                      v_all[:, kv * LANES:(kv + 1) * LANES].astype(BF16),
                      bias_at(2 * kv), bias_at(2 * kv + 1), m_ref.at[kv], l_ref.at[kv], acc_ref.at[kv])

    @pl.when(j < nt)
    def _():
        update(ck_ref[...], cv_ref[...], lambda h: bp_ref[0, h])

    @pl.when(j == nt)
    def _():
        update(kn_ref[0], vn_ref[0], lambda h: bn_ref[0, h])
        outs = [_osm_finish(l_ref.at[kv], acc_ref.at[kv], lam_ref, gs_ref, lam_init, tq)
                for kv in range(ATT_KV)]
        o_ref[0] = jnp.concatenate(outs, axis=-1).astype(o_ref.dtype)


def _attn_cached(qn, knb, vb, cache_k, cache_v, layer, bias_past, cls_tbl, bias_new, lam_p, g_subln,
                 lam_init, *, tk):
    b, l, _ = qn.shape
    past = cache_k.shape[2]
    nt = past // tk
    kw = ATT_KV * 2 * ATT_D
    ck = cache_k.reshape(cache_k.shape[0], b, past, kw)
    cv = cache_v.reshape(cache_v.shape[0], b, past, kw)
    grid_spec = pltpu.PrefetchScalarGridSpec(
        num_scalar_prefetch=1,
        grid=(b, nt + 1),
        in_specs=[
            pl.BlockSpec((1, l, ATT_H * 2 * ATT_D), lambda bi, j, cls: (bi, 0, 0)),
            pl.BlockSpec((None, None, tk, kw), lambda bi, j, cls: (layer, bi, jnp.minimum(j, nt - 1), 0)),
            pl.BlockSpec((None, None, tk, kw), lambda bi, j, cls: (layer, bi, jnp.minimum(j, nt - 1), 0)),
            pl.BlockSpec((1, l, kw), lambda bi, j, cls: (bi, 0, 0)),
            pl.BlockSpec((1, l, kw), lambda bi, j, cls: (bi, 0, 0)),
            pl.BlockSpec((1, ATT_H, l, tk), lambda bi, j, cls: (cls[jnp.minimum(j, nt - 1)], 0, 0, 0)),
            pl.BlockSpec((1, ATT_H, l, l), lambda bi, j, cls: (0, 0, 0, 0)),
            pl.BlockSpec((4, ATT_D), lambda bi, j, cls: (0, 0)),
            pl.BlockSpec((1, ATT_VD), lambda bi, j, cls: (0, 0)),
        ],
        out_specs=pl.BlockSpec((1, l, ATT_H * ATT_VD), lambda bi, j, cls: (bi, 0, 0)),
        scratch_shapes=[pltpu.VMEM((ATT_KV, 4 * l, LANES), F32), pltpu.VMEM((ATT_KV, 4 * l, LANES), F32),
                        pltpu.VMEM((ATT_KV, 4 * l, ATT_VD), F32)],
    )
    return pl.pallas_call(
        functools.partial(_attn_cached_kernel, tq=l, nt=nt, lam_init=lam_init),
        out_shape=jax.ShapeDtypeStruct((b, l, ATT_H * ATT_VD), BF16),
        grid_spec=grid_spec,
        compiler_params=_cparams(("parallel", "arbitrary")),
        name="diff_attn_cached",
    )(cls_tbl, qn, ck, cv, knb, vb, bias_past, bias_new, lam_p, g_subln.reshape(1, ATT_VD))


def _xattn_kernel(q_ref, mk_ref, mv_ref, o_ref):
    dn_t = (((1,), (1,)), ((), ()))
    outs = []
    for h in range(X_H):
        sl = slice(h * X_D, (h + 1) * X_D)
        s = lax.dot_general(q_ref[0, :, sl], mk_ref[0, :, sl].astype(BF16), dn_t,
                            preferred_element_type=F32) * (X_D ** -0.5)
        s = s - jnp.max(s, axis=-1, keepdims=True)
        p = jnp.exp(s)
        p = p / jnp.sum(p, axis=-1, keepdims=True)
        outs.append(jnp.dot(p.astype(BF16), mv_ref[0, :, sl].astype(BF16), preferred_element_type=F32))
    o_ref[0] = jnp.concatenate(outs, axis=-1).astype(o_ref.dtype)


def _xattn(q, mk, mv, *, tq):
    b, l, w = q.shape
    mlen = mk.shape[1]
    tq = min(tq, l)
    assert l % tq == 0
    return pl.pallas_call(
        _xattn_kernel,
        out_shape=jax.ShapeDtypeStruct((b, l, w), BF16),
        grid=(b, l // tq),
        in_specs=[
            pl.BlockSpec((1, tq, w), lambda i, c: (i, c, 0)),
            pl.BlockSpec((1, mlen, w), lambda i, c: (i, 0, 0)),
            pl.BlockSpec((1, mlen, w), lambda i, c: (i, 0, 0)),
        ],
        out_specs=pl.BlockSpec((1, tq, w), lambda i, c: (i, c, 0)),
        compiler_params=_cparams(("parallel", "parallel")),
        name="mem_cross_attn",
    )(q, mk, mv)


def _prep_weights(p, depth):
    layers = []
    off_z, off_xbc = 0, SSM_D
    off_dt = off_xbc + XBC_W
    off_pool = off_dt + SSM_H
    off_q = off_pool + BRANCH_W
    off_k = off_q + ATT_H * 2 * ATT_D
    off_v = off_k + ATT_KV * 2 * ATT_D
    off_gu = off_v + ATT_KV * ATT_VD
    off_gv = off_gu + BRANCH_W
    in_w = off_gv + BRANCH_W
    for l in range(depth):
        w = p['w_in'][l]
        d = w.shape[0]
        w_in = jnp.concatenate([
            w[:, off_xbc:off_dt], w[:, off_z:off_xbc], w[:, off_pool:off_q], w[:, off_q:off_k],
            w[:, off_gu:off_gv], w[:, off_gv:in_w], w[:, off_k:off_v], w[:, off_v:off_gu],
            w[:, off_dt:off_pool], jnp.zeros((d, LANES - SSM_H), w.dtype)], axis=1).astype(BF16)
        pad = lambda v: jnp.pad(v, (0, LANES - SSM_H)).reshape(1, LANES)
        layers.append(dict(
            g_ffn1=p['g_ffn1'][l], ffn1_in=p['w_ffn1_in'][l].astype(BF16), ffn1_out=p['w_ffn1_out'][l].astype(BF16),
            g_mix=p['g_mix'][l], w_in=w_in,
            conv_w=p['conv_w'][l], conv_b=p['conv_b'][l].reshape(1, XBC_W),
            dt_bias=pad(p['dt_bias'][l]), a_log=pad(p['a_log'][l]),
            d_skip=jnp.repeat(p['d_skip'][l], SSM_P).reshape(1, SSM_D), g_ssd=p['g_ssd'][l].reshape(1, SSM_D),
            w_pool=p['w_pool'][l].astype(BF16), pool_scale=p['pool_scale'][l].reshape(1, BRANCH_W),
            g_q=p['g_q'][l], g_k=p['g_k'][l], lam=p['lam'][l], g_subln=p['g_subln'][l],
            g_gv=p['g_gv'][l].reshape(1, BRANCH_W), w_sp=p['w_sp'][l], b_sp=p['b_sp'][l],
            w_gate=p['w_gate'][l].astype(BF16), b_gate=p['b_gate'][l], w_branch=p['w_branch'][l].astype(BF16),
            w_out=p['w_out'][l].astype(BF16),
            g_x=p['g_x'][l], w_xq=p['w_xq'][l].astype(BF16), g_xq=p['g_xq'][l], w_xo=p['w_xo'][l].astype(BF16),
            g_ffn2=p['g_ffn2'][l], ffn2_in=p['w_ffn2_in'][l].astype(BF16), ffn2_out=p['w_ffn2_out'][l].astype(BF16),
            g_post=p['g_post'][l],
            g_mem=p['g_mem'][l], w_mk=p['w_mk'][l].astype(BF16), w_mv=p['w_mv'][l].astype(BF16), g_xk=p['g_xk'][l],
        ))
    return layers


def _run_trunk(x, cache_k, cache_v, mem_k, mem_v, ssm0, conv0, pool0, layers, rel_bias):
    b, l, d = x.shape
    m = b * l
    depth = len(layers)
    cached = cache_k is not None
    assert l >= POOL_BUF and l >= CONV_W - 1
    if cached:
        past = cache_k.shape[2]
        tk = min(512, past)
        assert past % tk == 0 and past % CHUNK == 0 and l <= CHUNK and (past % GM_CHUNK == 0)
        nt = past // tk
        offs, cls = [FAR_OFFSET], []
        for j in range(nt):
            off = j * tk - past
            if _is_far(off, l, tk):
                cls.append(0)
            else:
                offs.append(off)
                cls.append(len(offs) - 1)
        bias_past = _bias_tiles(rel_bias, offs, l, tk)
        cls_tbl = jnp.asarray(np.asarray(cls, np.int32))
        bias_new = _bias_tiles(rel_bias, [0], l, l)
    else:
        past = 0
        tq = min(256, l)
        assert l % tq == 0 and tq % CHUNK == 0
        offs = [0]
        if l > tq:
            offs.append(-tq)
        if l > 2 * tq:
            assert _is_far(-2 * tq, tq, tq)
            offs.append(FAR_OFFSET)
        bias_self = _bias_tiles(rel_bias, offs, tq, tq)

    h = x.reshape(m, d)
    new_k, new_v, new_ssm, new_conv, new_pool, new_gv = [], [], [], [], [], []
    for li, w in enumerate(layers):
        lam_init = 0.8 - 0.6 * math.exp(-0.3 * li)
        h = _ffn(h, w['g_ffn1'], w['ffn1_in'], w['ffn1_out'])
        proj = _mm(h, w['w_in'], g=w['g_mix'], out_dtype=F32, tm=512, tn=1664, name="in_proj")
        proj = proj.reshape(b, l, P_W)

        y_a, s_ssm = _ssd(proj, conv0[li] if cached else None, ssm0[li] if cached else None,
                          w['conv_w'], w['conv_b'], w['dt_bias'], w['a_log'], w['d_skip'], w['g_ssd'], q=128)
        y_b = _pool(proj, pool0[li] if cached else None, past, w['w_pool'], w['pool_scale'], t=256)
        qn, kn, knb, vb = _qknorm(proj, w['g_q'], w['g_k'], t=512)
        if cached:
            y_c = _attn_cached(qn, knb, vb, cache_k, cache_v, li, bias_past, cls_tbl, bias_new,
                               w['lam'], w['g_subln'], lam_init, tk=tk)
        else:
            y_c = _attn_self(qn, knb, vb, bias_self, w['lam'], w['g_subln'], lam_init, tq=tq)
        y_d, v_gm = _gmlp(proj, w['g_gv'], w['w_sp'], w['b_sp'])

        ys = [y.reshape(m, BRANCH_W) for y in (y_a, y_b, y_c, y_d)]
        merged = _merge(h, w['g_mix'], ys, w['w_gate'], w['b_gate'], w['w_branch'])
        h = _mm(merged, w['w_out'], res=h, out_dtype=F32, tm=512, tn=1024, name="out_proj")

        qx = _mm(h, w['w_xq'], g=w['g_x'], gh=w['g_xq'], out_dtype=BF16, tm=512, tn=512, name="xattn_q")
        o = _xattn(qx.reshape(b, l, X_H * X_D), mem_k[li].reshape(b, -1, X_H * X_D),
                   mem_v[li].reshape(b, -1, X_H * X_D), tq=512)
        h = _mm(o.reshape(m, X_H * X_D), w['w_xo'], res=h, out_dtype=F32, tm=512, tn=1024, name="xattn_o")
        h = _ffn(h, w['g_ffn2'], w['ffn2_in'], w['ffn2_out'], post_g=w['g_post'])

        new_k.append(kn.reshape(b, l, ATT_KV, 2, ATT_D))
        new_v.append(proj[:, :, P_V:P_V + ATT_KV * ATT_VD].reshape(b, l, ATT_KV, ATT_VD))
        new_ssm.append(s_ssm)
        new_conv.append(proj[:, l - (CONV_W - 1):, P_XBC:P_XBC + XBC_W])
        new_pool.append(proj[:, l - POOL_BUF:, P_POOL:P_POOL + BRANCH_W])
        new_gv.append(v_gm)
    return (h.reshape(b, l, d), jnp.stack(new_k), jnp.stack(new_v), jnp.stack(new_ssm),
            jnp.stack(new_conv), jnp.stack(new_pool), jnp.stack(new_gv))


def kernel(x_prompt, x_sample, cache_attn_k, cache_attn_v, cache_mem_k, cache_mem_v, state_ssm, state_conv, state_pool, mem_prompt, g_ffn1, w_ffn1_in, w_ffn1_out, g_mix, w_in, conv_w, conv_b, dt_bias, a_log, d_skip, g_ssd, w_pool, pool_scale, g_q, g_k, lam, g_subln, rel_bias, g_gv, w_sp, b_sp, w_gate, b_gate, w_branch, w_out, g_x, w_xq, g_xq, g_mem, w_mk, w_mv, g_xk, w_xo, g_ffn2, w_ffn2_in, w_ffn2_out, g_post):
    p = dict(g_ffn1=g_ffn1, w_ffn1_in=w_ffn1_in, w_ffn1_out=w_ffn1_out, g_mix=g_mix, w_in=w_in,
             conv_w=conv_w, conv_b=conv_b, dt_bias=dt_bias, a_log=a_log, d_skip=d_skip, g_ssd=g_ssd,
             w_pool=w_pool, pool_scale=pool_scale, g_q=g_q, g_k=g_k, lam=lam, g_subln=g_subln,
             g_gv=g_gv, w_sp=w_sp, b_sp=b_sp, w_gate=w_gate, b_gate=b_gate, w_branch=w_branch,
             w_out=w_out, g_x=g_x, w_xq=w_xq, g_xq=g_xq, g_mem=g_mem, w_mk=w_mk, w_mv=w_mv, g_xk=g_xk,
             w_xo=w_xo, g_ffn2=g_ffn2, w_ffn2_in=w_ffn2_in, w_ffn2_out=w_ffn2_out, g_post=g_post)
    depth = w_in.shape[0]
    layers = _prep_weights(p, depth)

    bp, mlen, d = mem_prompt.shape
    mem2 = mem_prompt.reshape(bp * mlen, d)
    mks, mvs = [], []
    for w in layers:
        mk = _mm(mem2, w['w_mk'], g=w['g_mem'], gh=w['g_xk'], out_dtype=F32, tm=512, tn=512, name="mem_k")
        mv = _mm(mem2, w['w_mv'], g=w['g_mem'], out_dtype=F32, tm=512, tn=512, name="mem_v")
        mks.append(mk.reshape(bp, mlen, X_H, X_D))
        mvs.append(mv.reshape(bp, mlen, X_H, X_D))
    p_mem_k = jnp.stack(mks)
    p_mem_v = jnp.stack(mvs)

    y_prompt, p_attn_k, p_attn_v, p_ssm, p_conv, p_pool, _ = _run_trunk(
        x_prompt, None, None, p_mem_k, p_mem_v, None, None, None, layers, rel_bias)
    y_sample, s_attn_k, s_attn_v, s_ssm, s_conv, s_pool, s_gmlp_v = _run_trunk(
        x_sample, cache_attn_k, cache_attn_v, cache_mem_k, cache_mem_v, state_ssm, state_conv,
        state_pool, layers, rel_bias)

    return (y_prompt, y_sample, p_attn_k, p_attn_v, p_mem_k, p_mem_v, p_ssm, p_conv, p_pool,
            s_attn_k, s_attn_v, s_ssm, s_conv, s_pool, s_gmlp_v)
```

```python
import functools
import math

import jax
import jax.numpy as jnp
import numpy as np
from jax import lax
from jax.experimental import pallas as pl
from jax.experimental.pallas import tpu as pltpu

F32 = jnp.float32
BF16 = jnp.bfloat16
EPS = 1e-6
NEG = -1e30

VMEM_LIMIT_BYTES = 56 * 1024 * 1024
LANES = 128

BRANCH_W = 1024
SSM_P = 64
SSM_H = 16
SSM_G = 4
SSM_N = 128
SSM_D = SSM_H * SSM_P
CONV_W = 4
XBC_W = SSM_D + 2 * SSM_G * SSM_N
POOL_WINDOWS = (2, 4, 8, 16)
POOL_GW = BRANCH_W // len(POOL_WINDOWS)
POOL_BUF = max(POOL_WINDOWS) - 1
ATT_H = 8
ATT_KV = 4
ATT_D = 64
ATT_VD = 128
CHUNK = 64
N_BUCKETS = 32
MAX_DIST = 128
GM_CHUNK = 128
GM_G = 4
GM_GW = BRANCH_W // GM_G
X_H = 4
X_D = 128

P_XBC = 0
P_Z = 2048
P_POOL = 3072
P_Q = 4096
P_GU = 5120
P_GV = 6144
P_K = 7168
P_V = 7680
P_DT = 8192
P_W = 8320


def _cparams(sem):
    return pltpu.CompilerParams(dimension_semantics=sem, vmem_limit_bytes=VMEM_LIMIT_BYTES)


N_BRANCH = 4


def _skip_ref(fn, pos):
    def wrapped(*refs):
        return fn(*refs[:pos], *refs[pos + 1:])
    return wrapped


def _branch_out(ybuf, slab, b, l, args, specs, kernel_fn, n_prefetch=0):
    shape = jax.ShapeDtypeStruct((N_BRANCH, b, l, BRANCH_W), BF16)
    if ybuf is None:
        return shape, kernel_fn, {}
    assert slab > 0 and ybuf.shape == shape.shape and ybuf.dtype == shape.dtype
    pos = n_prefetch + len(args)
    args.append(ybuf)
    specs.append(pl.BlockSpec(memory_space=pl.ANY))
    return shape, _skip_ref(kernel_fn, pos), {pos: 0}


def _rms(xf, g):
    ms = jnp.mean(xf * xf, axis=-1, keepdims=True)
    return xf * lax.rsqrt(ms + EPS) * g


def _silu(x):
    return x * jax.nn.sigmoid(x)


def _mm_kernel(*refs, norm, head_norm, residual):
    it = iter(refs)
    x_ref = next(it)
    g_ref = next(it) if norm else None
    w_ref = next(it)
    gh_ref = next(it) if head_norm else None
    res_ref = next(it) if residual else None
    o_ref = next(it)
    xn_ref = next(it) if norm else None

    if norm:
        @pl.when(pl.program_id(1) == 0)
        def _():
            xn_ref[...] = _rms(x_ref[...].astype(F32), g_ref[...]).astype(BF16)
        xb = xn_ref[...]
    else:
        xb = x_ref[...].astype(BF16)
    acc = jnp.dot(xb, w_ref[...], preferred_element_type=F32)
    if head_norm:
        parts = []
        for c in range(acc.shape[1] // LANES):
            parts.append(_rms(acc[:, c * LANES:(c + 1) * LANES], gh_ref[...]))
        acc = jnp.concatenate(parts, axis=-1)
    if residual:
        acc = res_ref[...] + acc
    o_ref[...] = acc.astype(o_ref.dtype)


def _mm(x, w, *, g=None, gh=None, res=None, out_dtype, tm, tn, name, w_resident=False):
    m, k = x.shape
    n = w.shape[1]
    tm = min(tm, m)
    tn = min(tn, n)
    assert m % tm == 0 and n % tn == 0
    norm, head_norm, residual = g is not None, gh is not None, res is not None
    assert not (norm and w_resident)
    if w_resident:
        grid = (n // tn, m // tm)
        ij = lambda a, b: (b, a)
    else:
        grid = (m // tm, n // tn)
        ij = lambda a, b: (a, b)
    args, specs = [x], [pl.BlockSpec((tm, k), lambda a, b: (ij(a, b)[0], 0))]
    if norm:
        args.append(g.reshape(1, k))
        specs.append(pl.BlockSpec((1, k), lambda a, b: (0, 0)))
    args.append(w)
    specs.append(pl.BlockSpec((k, tn), lambda a, b: (0, ij(a, b)[1])))
    if head_norm:
        args.append(gh.reshape(1, LANES))
        specs.append(pl.BlockSpec((1, LANES), lambda a, b: (0, 0)))
    if residual:
        args.append(res)
        specs.append(pl.BlockSpec((tm, tn), lambda a, b: ij(a, b)))
    scratch = [pltpu.VMEM((tm, k), BF16)] if norm else []
    return pl.pallas_call(
        functools.partial(_mm_kernel, norm=norm, head_norm=head_norm, residual=residual),
        out_shape=jax.ShapeDtypeStruct((m, n), out_dtype),
        grid=grid,
        in_specs=specs,
        out_specs=pl.BlockSpec((tm, tn), lambda a, b: ij(a, b)),
        scratch_shapes=scratch,
        compiler_params=_cparams(("parallel", "arbitrary")),
        name=name,
    )(*args)


def _ffn_kernel(*refs, post):
    if post:
        x_ref, g_ref, wg_ref, wu_ref, wo_ref, gp_ref, o_ref, xn_ref, acc_ref = refs
    else:
        x_ref, g_ref, wg_ref, wu_ref, wo_ref, o_ref, xn_ref, acc_ref = refs
    f = pl.program_id(1)

    @pl.when(f == 0)
    def _():
        xn_ref[...] = _rms(x_ref[...], g_ref[...]).astype(BF16)
        acc_ref[...] = jnp.zeros_like(acc_ref)

    xb = xn_ref[...]
    gate = jnp.dot(xb, wg_ref[...], preferred_element_type=F32)
    up = jnp.dot(xb, wu_ref[...], preferred_element_type=F32)
    mid = (_silu(gate) * up).astype(BF16)
    acc_ref[...] += jnp.dot(mid, wo_ref[...], preferred_element_type=F32)

    @pl.when(f == pl.num_programs(1) - 1)
    def _():
        h = x_ref[...] + 0.5 * acc_ref[...]
        if post:
            h = _rms(h, gp_ref[...])
        o_ref[...] = h


def _ffn(x, g, w_in, w_out, *, post_g=None, tm=512, tf=512):
    m, d = x.shape
    ff = w_out.shape[0]
    tm = min(tm, m)
    assert m % tm == 0 and ff % tf == 0
    nf = ff // tf
    post = post_g is not None
    args = [x, g.reshape(1, d), w_in, w_in, w_out]
    specs = [
        pl.BlockSpec((tm, d), lambda i, f: (i, 0)),
        pl.BlockSpec((1, d), lambda i, f: (0, 0)),
        pl.BlockSpec((d, tf), lambda i, f: (0, f)),
        pl.BlockSpec((d, tf), lambda i, f: (0, nf + f)),
        pl.BlockSpec((tf, d), lambda i, f: (f, 0)),
    ]
    if post:
        args.append(post_g.reshape(1, d))
        specs.append(pl.BlockSpec((1, d), lambda i, f: (0, 0)))
    return pl.pallas_call(
        functools.partial(_ffn_kernel, post=post),
        out_shape=jax.ShapeDtypeStruct((m, d), F32),
        grid=(m // tm, nf),
        in_specs=specs,
        out_specs=pl.BlockSpec((tm, d), lambda i, f: (i, 0)),
        scratch_shapes=[pltpu.VMEM((tm, d), BF16), pltpu.VMEM((tm, d), F32)],
        compiler_params=_cparams(("parallel", "arbitrary")),
        name="swiglu_ffn",
    )(*args)


def _merge_kernel(h_ref, g_ref, y_ref, wg_ref, bg_ref, wb_ref, o_ref, hn_ref, acc_ref):
    j = pl.program_id(1)
    br = pl.program_id(2)

    @pl.when((j == 0) & (br == 0))
    def _():
        hn_ref[...] = _rms(h_ref[...], g_ref[...]).astype(BF16)

    @pl.when(br == 0)
    def _():
        acc_ref[...] = jnp.zeros_like(acc_ref)

    gate = jax.nn.sigmoid(jnp.dot(hn_ref[...], wg_ref[...], preferred_element_type=F32) + bg_ref[...])
    acc_ref[...] += gate * jnp.dot(y_ref[...], wb_ref[...], preferred_element_type=F32)

    @pl.when(br == pl.num_programs(2) - 1)
    def _():
        o_ref[...] = acc_ref[...].astype(o_ref.dtype)


def _merge(h, g, ys, w_gate, b_gate, w_branch, *, tm=512, tn=1024):
    m, d = h.shape
    nb, bw, _ = w_branch.shape
    tm = min(tm, m)
    assert m % tm == 0 and d % tn == 0 and ys.shape == (nb, m, bw)
    return pl.pallas_call(
        _merge_kernel,
        out_shape=jax.ShapeDtypeStruct((m, d), BF16),
        grid=(m // tm, d // tn, nb),
        in_specs=[
            pl.BlockSpec((tm, d), lambda i, j, b: (i, 0)),
            pl.BlockSpec((1, d), lambda i, j, b: (0, 0)),
            pl.BlockSpec((None, tm, bw), lambda i, j, b: (b, i, 0)),
            pl.BlockSpec((None, d, tn), lambda i, j, b: (b, 0, j)),
            pl.BlockSpec((None, 1, tn), lambda i, j, b: (b, 0, j)),
            pl.BlockSpec((None, bw, tn), lambda i, j, b: (b, 0, j)),
        ],
        out_specs=pl.BlockSpec((tm, tn), lambda i, j, b: (i, j)),
        scratch_shapes=[pltpu.VMEM((tm, d), BF16), pltpu.VMEM((tm, tn), F32)],
        compiler_params=_cparams(("parallel", "arbitrary", "arbitrary")),
        name="branch_merge",
    )(h, g.reshape(1, d), ys, w_gate, b_gate.reshape(nb, 1, d), w_branch)


def _ssd_kernel(*refs, q, has_state):
    it = iter(refs)
    xbc_ref, z_ref, dt_ref = next(it), next(it), next(it)
    conv0_ref = next(it) if has_state else None
    s0_ref = next(it) if has_state else None
    cw_ref, cb_ref, dtb_ref, alog_ref, dskip_ref, gssd_ref = (next(it) for _ in range(6))
    y_ref, s_ref, cbuf = next(it), next(it), next(it)
    hist = 8

    @pl.when(pl.program_id(1) == 0)
    def _():
        cbuf[0:hist, :] = jnp.zeros((hist, XBC_W), F32)
        if has_state:
            cbuf[hist - (CONV_W - 1):hist, :] = conv0_ref[0]
            s_ref[0] = s0_ref[0]
        else:
            s_ref[0] = jnp.zeros(s_ref.shape[1:], F32)

    cbuf[hist:hist + q, :] = xbc_ref[0]
    conv = cb_ref[...]
    for k in range(CONV_W):
        lo = hist - (CONV_W - 1) + k
        conv = conv + cbuf[lo:lo + q, :] * cw_ref[k:k + 1, :]
    cbuf[0:hist, :] = cbuf[q:q + hist, :]
    xc = _silu(conv)
    xs = xc[:, :SSM_D]
    bm = xc[:, SSM_D:SSM_D + SSM_G * SSM_N]
    cm = xc[:, SSM_D + SSM_G * SSM_N:]

    dtl = dt_ref[0] + dtb_ref[...]
    dt = jnp.maximum(dtl, 0.0) + jnp.log1p(jnp.exp(-jnp.abs(dtl)))
    a = dt * (-jnp.exp(alog_ref[...]))
    ri = lax.broadcasted_iota(jnp.int32, (q, q), 0)
    ci = lax.broadcasted_iota(jnp.int32, (q, q), 1)
    tril = ri >= ci
    cum = jnp.dot(tril.astype(F32), a, preferred_element_type=F32,
                  precision=lax.Precision.HIGHEST)
    cum_t = cum.T
    dt_t = dt.T
    cum_last = cum[q - 1:q, :]
    w_state = jnp.exp(cum_last - cum) * dt
    ecum = jnp.exp(cum)
    ecl = jnp.exp(cum_last)

    lane_lo = lax.broadcasted_iota(jnp.int32, (q, LANES), 1) < SSM_P
    row_lo_t = lax.broadcasted_iota(jnp.int32, (LANES, q), 0) < SSM_P
    row_lo_s = lax.broadcasted_iota(jnp.int32, (LANES, SSM_N), 0) < SSM_P
    dn_t = (((1,), (1,)), ((), ()))

    y_parts = []
    cb = None
    heads_per_group = SSM_H // SSM_G
    for pp in range(SSM_H // 2):
        grp = (2 * pp) // heads_per_group
        xp = xs[:, pp * LANES:(pp + 1) * LANES]
        bg = bm[:, grp * SSM_N:(grp + 1) * SSM_N]
        cg = cm[:, grp * SSM_N:(grp + 1) * SSM_N]
        cg_b = cg.astype(BF16)
        if (2 * pp) % heads_per_group == 0:
            cb = lax.dot_general(cg_b, bg.astype(BF16), dn_t, preferred_element_type=F32)
        xp_t = xp.T
        s_pair = s_ref[0, pp * LANES:(pp + 1) * LANES, :]
        y_pair = None
        s_new = None
        for hh in range(2):
            hd = 2 * pp + hh
            seg = cum[:, hd:hd + 1] - cum_t[hd:hd + 1, :]
            mat = jnp.exp(jnp.where(tril, seg, NEG)) * cb * dt_t[hd:hd + 1, :]
            sel = lane_lo if hh == 0 else jnp.logical_not(lane_lo)
            x_h = jnp.where(sel, xp, 0.0).astype(BF16)
            yd = jnp.dot(mat.astype(BF16), x_h, preferred_element_type=F32)
            y_pair = yd if y_pair is None else y_pair + yd
            sel_t = row_lo_t if hh == 0 else jnp.logical_not(row_lo_t)
            xt_h = jnp.where(sel_t, xp_t, 0.0).astype(BF16)
            bs = (bg * w_state[:, hd:hd + 1]).astype(BF16)
            sn = jnp.dot(xt_h, bs, preferred_element_type=F32)
            s_new = sn if s_new is None else s_new + sn
        y_off = lax.dot_general(cg_b, s_pair.astype(BF16), dn_t, preferred_element_type=F32)
        e_pair = jnp.where(lane_lo, ecum[:, 2 * pp:2 * pp + 1], ecum[:, 2 * pp + 1:2 * pp + 2])
        y_parts.append(y_pair + y_off * e_pair)
        dec = jnp.where(row_lo_s, ecl[:, 2 * pp:2 * pp + 1], ecl[:, 2 * pp + 1:2 * pp + 2])
        s_ref[0, pp * LANES:(pp + 1) * LANES, :] = dec * s_pair + s_new

    y = jnp.concatenate(y_parts, axis=-1)
    y = y + dskip_ref[...] * xs
    y = y * _silu(z_ref[0])
    gw = SSM_D // SSM_G
    outs = [_rms(y[:, gi * gw:(gi + 1) * gw], gssd_ref[:, gi * gw:(gi + 1) * gw]) for gi in range(SSM_G)]
    y_ref[0] = jnp.concatenate(outs, axis=-1).astype(y_ref.dtype)


def _ssd(proj, conv0, s0, cw, cb, dtb, alog, dskip, gssd, *, q, ybuf=None, slab=0):
    b, l, _ = proj.shape
    q = min(q, l)
    assert l % q == 0 and q >= 8
    has_state = s0 is not None
    args = [proj, proj, proj]
    specs = [
        pl.BlockSpec((1, q, XBC_W), lambda i, c: (i, c, P_XBC // XBC_W)),
        pl.BlockSpec((1, q, SSM_D), lambda i, c: (i, c, P_Z // SSM_D)),
        pl.BlockSpec((1, q, LANES), lambda i, c: (i, c, P_DT // LANES)),
    ]
    if has_state:
        args += [conv0, s0.reshape(b, SSM_H * SSM_P, SSM_N)]
        specs += [
            pl.BlockSpec((1, CONV_W - 1, XBC_W), lambda i, c: (i, 0, 0)),
            pl.BlockSpec((1, SSM_H * SSM_P, SSM_N), lambda i, c: (i, 0, 0)),
        ]
    args += [cw, cb, dtb, alog, dskip, gssd]
    specs += [
        pl.BlockSpec((CONV_W, XBC_W), lambda i, c: (0, 0)),
        pl.BlockSpec((1, XBC_W), lambda i, c: (0, 0)),
        pl.BlockSpec((1, LANES), lambda i, c: (0, 0)),
        pl.BlockSpec((1, LANES), lambda i, c: (0, 0)),
        pl.BlockSpec((1, SSM_D), lambda i, c: (0, 0)),
        pl.BlockSpec((1, SSM_D), lambda i, c: (0, 0)),
    ]
    yshape, kern, aliases = _branch_out(ybuf, slab, b, l, args, specs,
                                        functools.partial(_ssd_kernel, q=q, has_state=has_state))
    y, s = pl.pallas_call(
        kern,
        out_shape=(yshape, jax.ShapeDtypeStruct((b, SSM_H * SSM_P, SSM_N), F32)),
        grid=(b, l // q),
        in_specs=specs,
        out_specs=(pl.BlockSpec((None, 1, q, SSM_D), lambda i, c: (slab, i, c, 0)),
                   pl.BlockSpec((1, SSM_H * SSM_P, SSM_N), lambda i, c: (i, 0, 0))),
        scratch_shapes=[pltpu.VMEM((q + 8, XBC_W), F32)],
        input_output_aliases=aliases,
        compiler_params=_cparams(("parallel", "arbitrary")),
        name="ssd_mixer",
    )(*args)
    return y, s.reshape(b, SSM_H, SSM_P, SSM_N)


def _pool_kernel(*refs, t, pos0, has_state):
    it = iter(refs)
    xp_ref = next(it)
    p0_ref = next(it) if has_state else None
    w_ref, sc_ref, y_ref, buf = next(it), next(it), next(it), next(it)
    hist = POOL_BUF + 1
    ti = pl.program_id(1)

    @pl.when(ti == 0)
    def _():
        buf[0:hist, :] = jnp.zeros((hist, BRANCH_W), F32)
        if has_state:
            buf[1:hist, :] = p0_ref[0]

    xp = xp_ref[0]
    buf[hist:hist + t, :] = xp
    pos = pos0 + ti * t + lax.broadcasted_iota(jnp.int32, (t, POOL_GW), 0)
    outs = []
    for gi, w in enumerate(POOL_WINDOWS):
        c0 = gi * POOL_GW
        s = xp[:, c0:c0 + POOL_GW]
        for k in range(1, w):
            s = s + buf[hist - k:hist - k + t, c0:c0 + POOL_GW]
        cnt = jnp.minimum(w, pos + 1).astype(F32)
        pooled = s / cnt - xp[:, c0:c0 + POOL_GW]
        y = jnp.dot(pooled.astype(BF16), w_ref[gi], preferred_element_type=F32)
        outs.append(y * sc_ref[:, c0:c0 + POOL_GW])
    y_ref[0] = jnp.concatenate(outs, axis=-1).astype(y_ref.dtype)
    buf[0:hist, :] = buf[t:t + hist, :]


def _pool(proj, pool0, pos0, w_pool, scale, *, t, ybuf=None, slab=0):
    b, l, _ = proj.shape
    t = min(t, l)
    assert l % t == 0 and t >= POOL_BUF + 1
    has_state = pool0 is not None
    args = [proj]
    specs = [pl.BlockSpec((1, t, BRANCH_W), lambda i, c: (i, c, P_POOL // BRANCH_W))]
    if has_state:
        args.append(pool0)
        specs.append(pl.BlockSpec((1, POOL_BUF, BRANCH_W), lambda i, c: (i, 0, 0)))
    args += [w_pool, scale]
    specs += [
        pl.BlockSpec((len(POOL_WINDOWS), POOL_GW, POOL_GW), lambda i, c: (0, 0, 0)),
        pl.BlockSpec((1, BRANCH_W), lambda i, c: (0, 0)),
    ]
    yshape, kern, aliases = _branch_out(
        ybuf, slab, b, l, args, specs, functools.partial(_pool_kernel, t=t, pos0=pos0, has_state=has_state))
    return pl.pallas_call(
        kern,
        out_shape=yshape,
        grid=(b, l // t),
        in_specs=specs,
        out_specs=pl.BlockSpec((None, 1, t, BRANCH_W), lambda i, c: (slab, i, c, 0)),
        scratch_shapes=[pltpu.VMEM((t + POOL_BUF + 1, BRANCH_W), F32)],
        input_output_aliases=aliases,
        compiler_params=_cparams(("parallel", "arbitrary")),
        name="pool_mixer",
    )(*args)


def _gmlp_kernel(u_ref, v_ref, g_ref, w_ref, b_ref, y_ref, *rest, cl, nsub, keep_vn):
    tril = lax.broadcasted_iota(jnp.int32, (cl, cl), 0) >= lax.broadcasted_iota(jnp.int32, (cl, cl), 1)
    ws = [jnp.where(tril, w_ref[gi], 0.0).astype(BF16) for gi in range(GM_G)]
    for sub in range(nsub):
        rows = slice(sub * cl, (sub + 1) * cl)
        u = jax.nn.gelu(u_ref[0, rows, :], approximate=True)
        vn = _rms(jax.nn.gelu(v_ref[0, rows, :], approximate=True), g_ref[...])
        if keep_vn:
            rest[0][0, rows, :] = vn
        outs = []
        for gi in range(GM_G):
            s = jnp.dot(ws[gi], vn[:, gi * GM_GW:(gi + 1) * GM_GW].astype(BF16), preferred_element_type=F32)
            s = s + b_ref[:, gi:gi + 1]
            outs.append(u[:, gi * GM_GW:(gi + 1) * GM_GW] * s)
        y_ref[0, rows, :] = jnp.concatenate(outs, axis=-1).astype(y_ref.dtype)


def _gmlp(proj, g_gv, w_sp, b_sp, *, keep_vn, ybuf=None, slab=0):
    b, l, _ = proj.shape
    cl = min(GM_CHUNK, l)
    assert l % cl == 0
    nsub = math.gcd(l // cl, 4)
    t = cl * nsub
    w = w_sp[:, :cl, :cl]
    bt = b_sp[:, :cl].T
    args = [proj, proj, g_gv, w, bt]
    specs = [
        pl.BlockSpec((1, t, BRANCH_W), lambda i, c: (i, c, P_GU // BRANCH_W)),
        pl.BlockSpec((1, t, BRANCH_W), lambda i, c: (i, c, P_GV // BRANCH_W)),
        pl.BlockSpec((1, BRANCH_W), lambda i, c: (0, 0)),
        pl.BlockSpec((GM_G, cl, cl), lambda i, c: (0, 0, 0)),
        pl.BlockSpec((cl, GM_G), lambda i, c: (0, 0)),
    ]
    yshape, kern, aliases = _branch_out(
        ybuf, slab, b, l, args, specs, functools.partial(_gmlp_kernel, cl=cl, nsub=nsub, keep_vn=keep_vn))
    out_shape = [yshape]
    out_specs = [pl.BlockSpec((None, 1, t, BRANCH_W), lambda i, c: (slab, i, c, 0))]
    if keep_vn:
        out_shape.append(jax.ShapeDtypeStruct((b, l, BRANCH_W), F32))
        out_specs.append(pl.BlockSpec((1, t, BRANCH_W), lambda i, c: (i, c, 0)))
    outs = pl.pallas_call(
        kern,
        out_shape=tuple(out_shape),
        grid=(b, l // t),
        in_specs=specs,
        out_specs=tuple(out_specs),
        input_output_aliases=aliases,
        compiler_params=_cparams(("parallel", "parallel")),
        name="gmlp_mixer",
    )(*args)
    return outs[0], (outs[1] if keep_vn else None)


def _norm_halves(x, g):
    lo = lax.broadcasted_iota(jnp.int32, x.shape, 1) < ATT_D
    xx = x * x
    s_lo = jnp.sum(jnp.where(lo, xx, 0.0), axis=-1, keepdims=True)
    s_hi = jnp.sum(jnp.where(lo, 0.0, xx), axis=-1, keepdims=True)
    ms = jnp.where(lo, s_lo, s_hi) * (1.0 / ATT_D)
    return x * lax.rsqrt(ms + EPS) * g


def _qknorm_kernel(q_ref, k_ref, v_ref, gq_ref, gk_ref, qn_ref, kn_ref, knb_ref, vb_ref):
    scale = ATT_D ** -0.5
    q = q_ref[0]
    qs = [_norm_halves(q[:, c * LANES:(c + 1) * LANES], gq_ref[...]) * scale
          for c in range(q.shape[1] // LANES)]
    qn_ref[0] = jnp.concatenate(qs, axis=-1).astype(qn_ref.dtype)
    k = k_ref[0]
    ks = [_norm_halves(k[:, c * LANES:(c + 1) * LANES], gk_ref[...]) for c in range(k.shape[1] // LANES)]
    kn = jnp.concatenate(ks, axis=-1)
    kn_ref[0] = kn
    knb_ref[0] = kn.astype(BF16)
    vb_ref[0] = v_ref[0].astype(BF16)


def _qknorm(proj, g_q, g_k, *, t):
    b, l, _ = proj.shape
    t = min(t, l)
    assert l % t == 0
    qw, kw = ATT_H * 2 * ATT_D, ATT_KV * 2 * ATT_D
    gq = jnp.tile(g_q, 2).reshape(1, LANES)
    gk = jnp.tile(g_k, 2).reshape(1, LANES)
    return pl.pallas_call(
        _qknorm_kernel,
        out_shape=(jax.ShapeDtypeStruct((b, l, qw), BF16), jax.ShapeDtypeStruct((b, l, kw), F32),
                   jax.ShapeDtypeStruct((b, l, kw), BF16), jax.ShapeDtypeStruct((b, l, kw), BF16)),
        grid=(b, l // t),
        in_specs=[
            pl.BlockSpec((1, t, qw), lambda i, c: (i, c, P_Q // qw)),
            pl.BlockSpec((1, t, kw), lambda i, c: (i, c, P_K // kw)),
            pl.BlockSpec((1, t, kw), lambda i, c: (i, c, P_V // kw)),
            pl.BlockSpec((1, LANES), lambda i, c: (0, 0)),
            pl.BlockSpec((1, LANES), lambda i, c: (0, 0)),
        ],
        out_specs=(pl.BlockSpec((1, t, qw), lambda i, c: (i, c, 0)),
                   pl.BlockSpec((1, t, kw), lambda i, c: (i, c, 0)),
                   pl.BlockSpec((1, t, kw), lambda i, c: (i, c, 0)),
                   pl.BlockSpec((1, t, kw), lambda i, c: (i, c, 0))),
        compiler_params=_cparams(("parallel", "parallel")),
        name="qk_norm",
    )(proj, proj, proj, gq, gk)


def _bias_kernel(off_ref, tbl_ref, o_ref, *, tq, tk, keys_major):
    c = pl.program_id(0)
    h = pl.program_id(1)
    off = off_ref[c]
    shape = (tk, tq) if keys_major else (tq, tk)
    row = lax.broadcasted_iota(jnp.int32, shape, 1 if keys_major else 0)
    col = lax.broadcasted_iota(jnp.int32, shape, 0 if keys_major else 1)
    rel = off + col - row
    nb = N_BUCKETS // 2
    max_exact = nb // 2
    n = jnp.abs(rel)
    large = max_exact + (jnp.log(jnp.maximum(n, 1).astype(F32) / max_exact)
                         / math.log(MAX_DIST / max_exact) * (nb - max_exact)).astype(jnp.int32)
    large = jnp.minimum(large, nb - 1)
    bucket = jnp.where(rel > 0, nb, 0) + jnp.where(n < max_exact, n, large)
    val = jnp.zeros(shape, F32)
    for bk in range(N_BUCKETS):
        val = jnp.where(bucket == bk, tbl_ref[bk * ATT_H + h], val)
    visible = jnp.right_shift(off + col, 6) <= jnp.right_shift(row, 6)
    o_ref[0, 0] = jnp.where(visible, val, NEG)


def _bias_tiles(rel_bias, offsets, tq, tk, keys_major=False):
    assert CHUNK == 64
    offs = jnp.asarray(np.asarray(offsets, np.int32))
    n_cls = len(offsets)
    shape = (tk, tq) if keys_major else (tq, tk)
    return pl.pallas_call(
        functools.partial(_bias_kernel, tq=tq, tk=tk, keys_major=keys_major),
        out_shape=jax.ShapeDtypeStruct((n_cls, ATT_H) + shape, F32),
        grid=(n_cls, ATT_H),
        in_specs=[pl.BlockSpec(memory_space=pltpu.SMEM), pl.BlockSpec(memory_space=pltpu.SMEM)],
        out_specs=pl.BlockSpec((1, 1) + shape, lambda c, h: (c, h, 0, 0)),
        compiler_params=_cparams(("parallel", "parallel")),
        name="rel_bias_tiles",
    )(offs, rel_bias.reshape(-1))


FAR_OFFSET = -(1 << 20)


def _is_far(off, tq, tk):
    nb = N_BUCKETS // 2
    max_rel = off + tk - 1
    if max_rel >= 0:
        return False
    n_min = -max_rel
    sat = (nb // 2) * (MAX_DIST / (nb // 2)) ** ((nb - 1 - nb // 2) / (nb - nb // 2))
    return n_min >= math.ceil(sat) + 1


def _q4(q2):
    lo = lax.broadcasted_iota(jnp.int32, (q2.shape[0], LANES), 1) < ATT_D
    parts = []
    for r in range(2):
        qr = q2[:, r * LANES:(r + 1) * LANES]
        parts.append(jnp.where(lo, qr, jnp.zeros_like(qr)))
        parts.append(jnp.where(lo, jnp.zeros_like(qr), qr))
    return jnp.concatenate(parts, axis=0)


def _osm_step(q4, k, v, b0, b1, m_ref, l_ref, acc_ref):
    dn_t = (((1,), (1,)), ((), ()))
    s = lax.dot_general(q4, k, dn_t, preferred_element_type=F32)
    s = s + jnp.concatenate([b0, b0, b1, b1], axis=0)
    m_prev = m_ref[...]
    m_new = jnp.maximum(m_prev, jnp.max(s, axis=-1, keepdims=True))
    alpha = jnp.exp(m_prev - m_new)
    p = jnp.exp(s - m_new[:, 0:1])
    l_ref[...] = alpha * l_ref[...] + jnp.sum(p, axis=-1, keepdims=True)
    acc_ref[...] = alpha * acc_ref[...] + jnp.dot(p.astype(BF16), v, preferred_element_type=F32)
    m_ref[...] = m_new


def _osm_finish(l_ref, acc_ref, lam_ref, gs_ref, lam_init, tq):
    lp = lam_ref[...]
    lam = (jnp.exp(jnp.sum(lp[0:1] * lp[1:2], axis=-1, keepdims=True))
           - jnp.exp(jnp.sum(lp[2:3] * lp[3:4], axis=-1, keepdims=True)) + lam_init)
    o = acc_ref[...] / l_ref[...]
    outs = []
    for r in range(2):
        d = o[(2 * r) * tq:(2 * r + 1) * tq] - lam * o[(2 * r + 1) * tq:(2 * r + 2) * tq]
        outs.append(_rms(d, gs_ref[...]) * (1.0 - lam_init))
    return jnp.concatenate(outs, axis=-1)


def _qknorm_t_kernel(q_ref, k_ref, v_ref, gq_ref, gk_ref, qt_ref, kn_ref, knb_ref, vt_ref):
    scale = ATT_D ** -0.5
    q = q_ref[0]
    t = q.shape[0]
    lo = lax.broadcasted_iota(jnp.int32, (t, LANES), 1) < ATT_D
    for kv in range(ATT_KV):
        cols = []
        for r in range(2):
            c = kv * 2 + r
            qr = _norm_halves(q[:, c * LANES:(c + 1) * LANES], gq_ref[...]) * scale
            cols.append(jnp.where(lo, qr, 0.0).T)
            cols.append(jnp.where(lo, 0.0, qr).T)
        qt_ref[0, kv] = jnp.concatenate(cols, axis=1).astype(qt_ref.dtype)
    k = k_ref[0]
    ks = [_norm_halves(k[:, c * LANES:(c + 1) * LANES], gk_ref[...]) for c in range(k.shape[1] // LANES)]
    kn = jnp.concatenate(ks, axis=-1)
    kn_ref[0] = kn
    knb_ref[0] = kn.astype(BF16)
    v = v_ref[0]
    for kv in range(ATT_KV):
        vt_ref[0, kv, 0] = v[:, kv * LANES:(kv + 1) * LANES].T.astype(vt_ref.dtype)


def _qknorm_t(proj, g_q, g_k, *, t):
    b, l, _ = proj.shape
    assert l % t == 0
    qw, kw = ATT_H * 2 * ATT_D, ATT_KV * 2 * ATT_D
    gq = jnp.tile(g_q, 2).reshape(1, LANES)
    gk = jnp.tile(g_k, 2).reshape(1, LANES)
    return pl.pallas_call(
        _qknorm_t_kernel,
        out_shape=(jax.ShapeDtypeStruct((b, ATT_KV, LANES, 4 * l), BF16),
                   jax.ShapeDtypeStruct((b, l, kw), F32),
                   jax.ShapeDtypeStruct((b, l, kw), BF16),
                   jax.ShapeDtypeStruct((b, ATT_KV, l // t, ATT_VD, t), BF16)),
        grid=(b, l // t),
        in_specs=[
            pl.BlockSpec((1, t, qw), lambda i, c: (i, c, P_Q // qw)),
            pl.BlockSpec((1, t, kw), lambda i, c: (i, c, P_K // kw)),
            pl.BlockSpec((1, t, kw), lambda i, c: (i, c, P_V // kw)),
            pl.BlockSpec((1, LANES), lambda i, c: (0, 0)),
            pl.BlockSpec((1, LANES), lambda i, c: (0, 0)),
        ],
        out_specs=(pl.BlockSpec((1, ATT_KV, LANES, 4 * t), lambda i, c: (i, 0, 0, c)),
                   pl.BlockSpec((1, t, kw), lambda i, c: (i, c, 0)),
                   pl.BlockSpec((1, t, kw), lambda i, c: (i, c, 0)),
                   pl.BlockSpec((1, ATT_KV, 1, ATT_VD, t), lambda i, c: (i, 0, c, 0, 0))),
        compiler_params=_cparams(("parallel", "parallel")),
        name="qk_norm_t",
    )(proj, proj, proj, gq, gk)


def _attn_self_kernel(qt_ref, k_ref, vt_ref, bias_ref, lam_ref, gs_ref, o_ref, m_ref, l_ref, acc_ref,
                      *, tq, tk, lam_init, n_cls, group):
    i = pl.program_id(2)
    m_ref[...] = jnp.full(m_ref.shape, NEG, F32)
    l_ref[...] = jnp.zeros(l_ref.shape, F32)
    acc_ref[...] = jnp.zeros(acc_ref.shape, F32)

    def step(j, far):
        kt = k_ref[0, pl.ds(pl.multiple_of(j * tk, tk), tk), :]
        vt = vt_ref[0, 0, j]
        cls = jnp.minimum(i - j, n_cls - 1)
        for c0 in range(0, 4, group):
            update(kt, vt, cls, far, range(c0, c0 + group))

    def update(kt, vt, cls, far, combos):
        cols = {c: slice(c * tq, (c + 1) * tq) for c in combos}
        ss = {c: jnp.dot(kt, qt_ref[0, 0, :, cs], preferred_element_type=F32) for c, cs in cols.items()}
        ps, alphas = {}, {}
        for c, cs in cols.items():
            s = ss[c]
            m_prev = m_ref[:, cs]
            if far:
                crow = bias_ref[cls, c // 2, 0:1, :]
                m_new = jnp.maximum(m_prev, jnp.max(s, axis=0, keepdims=True) + crow)
                p = jnp.exp(s - (m_new - crow))
            else:
                s = s + bias_ref[cls, c // 2]
                m_new = jnp.maximum(m_prev, jnp.max(s, axis=0, keepdims=True))
                p = jnp.exp(s - m_new)
            alpha = jnp.exp(m_prev - m_new)
            l_ref[:, cs] = alpha * l_ref[:, cs] + jnp.sum(p, axis=0, keepdims=True)
            m_ref[:, cs] = m_new
            ps[c] = p.astype(BF16)
            alphas[c] = alpha
        for c, cs in cols.items():
            acc_ref[:, cs] = alphas[c] * acc_ref[:, cs] + jnp.dot(vt, ps[c], preferred_element_type=F32)

    n_far = jnp.maximum(i - (n_cls - 2), 0) if n_cls == 3 else 0

    def far_body(j, carry):
        step(j, True)
        return carry

    def near_body(j, carry):
        step(j, False)
        return carry

    if n_cls == 3:
        lax.fori_loop(0, n_far, far_body, 0)
    lax.fori_loop(n_far, i + 1, near_body, 0)

    lp = lam_ref[...]
    lam = (jnp.exp(jnp.sum(lp[0:1] * lp[1:2], axis=-1, keepdims=True))
           - jnp.exp(jnp.sum(lp[2:3] * lp[3:4], axis=-1, keepdims=True)) + lam_init)
    o = acc_ref[...] / l_ref[...]
    outs = []
    for r in range(2):
        d = o[:, (2 * r) * tq:(2 * r + 1) * tq] - lam * o[:, (2 * r + 1) * tq:(2 * r + 2) * tq]
        ms = jnp.mean(d * d, axis=0, keepdims=True)
        y = d * lax.rsqrt(ms + EPS) * gs_ref[...] * (1.0 - lam_init)
        outs.append(y.T)
    o_ref[0] = jnp.concatenate(outs, axis=-1).astype(o_ref.dtype)


def _attn_self(qt, knb, vt, bias, lam_p, g_subln, lam_init, *, tq, ybuf=None, slab=0):
    b, l, _ = knb.shape
    n_cls = bias.shape[0]
    nk = l // tq
    hw = 2 * ATT_VD
    assert ATT_H * ATT_VD == BRANCH_W
    args = [qt, knb, vt, bias, lam_p, g_subln.reshape(ATT_VD, 1)]
    specs = [
        pl.BlockSpec((1, 1, LANES, 4 * tq), lambda bi, kv, i: (bi, kv, 0, i)),
        pl.BlockSpec((1, l, LANES), lambda bi, kv, i: (bi, 0, kv)),
        pl.BlockSpec((1, 1, nk, ATT_VD, tq), lambda bi, kv, i: (bi, kv, 0, 0, 0)),
        pl.BlockSpec((n_cls, 2, tq, tq), lambda bi, kv, i: (0, kv, 0, 0)),
        pl.BlockSpec((4, ATT_D), lambda bi, kv, i: (0, 0)),
        pl.BlockSpec((ATT_VD, 1), lambda bi, kv, i: (0, 0)),
    ]
    yshape, kern, aliases = _branch_out(
        ybuf, slab, b, l, args, specs,
        functools.partial(_attn_self_kernel, tq=tq, tk=tq, lam_init=lam_init, n_cls=n_cls, group=4))
    return pl.pallas_call(
        kern,
        out_shape=yshape,
        grid=(b, ATT_KV, l // tq),
        in_specs=specs,
        out_specs=pl.BlockSpec((None, 1, tq, hw), lambda bi, kv, i: (slab, bi, i, kv)),
        scratch_shapes=[pltpu.VMEM((1, 4 * tq), F32), pltpu.VMEM((1, 4 * tq), F32),
                        pltpu.VMEM((ATT_VD, 4 * tq), F32)],
        input_output_aliases=aliases,
        compiler_params=_cparams(("parallel", "parallel", "arbitrary")),
        name="diff_attn_self",
    )(*args)


def _attn_cached_kernel(cls_ref, q_ref, ck_ref, cv_ref, kn_ref, vn_ref, bp_ref, bn_ref, lam_ref, gs_ref,
                        o_ref, m_ref, l_ref, acc_ref, *, tq, nt, lam_init):
    del cls_ref
    j = pl.program_id(1)

    @pl.when(j == 0)
    def _():
        m_ref[...] = jnp.full(m_ref.shape, NEG, F32)
        l_ref[...] = jnp.zeros(l_ref.shape, F32)
        acc_ref[...] = jnp.zeros(acc_ref.shape, F32)

    def update(k_all, v_all, bias_at):
        for kv in range(ATT_KV):
            q4 = _q4(q_ref[0, :, kv * 2 * LANES:(kv + 1) * 2 * LANES])
            _osm_step(q4, k_all[:, kv * LANES:(kv + 1) * LANES].astype(BF16),
                      v_all[:, kv * LANES:(kv + 1) * LANES].astype(BF16),
                      bias_at(2 * kv), bias_at(2 * kv + 1), m_ref.at[kv], l_ref.at[kv], acc_ref.at[kv])

    @pl.when(j < nt)
    def _():
        update(ck_ref[...], cv_ref[...], lambda h: bp_ref[0, h])

    @pl.when(j == nt)
    def _():
        update(kn_ref[0], vn_ref[0], lambda h: bn_ref[0, h])
        outs = [_osm_finish(l_ref.at[kv], acc_ref.at[kv], lam_ref, gs_ref, lam_init, tq)
                for kv in range(ATT_KV)]
        o_ref[0] = jnp.concatenate(outs, axis=-1).astype(o_ref.dtype)


def _attn_cached(qn, knb, vb, cache_k, cache_v, layer, bias_past, cls_tbl, bias_new, lam_p, g_subln,
                 lam_init, *, tk, ybuf=None, slab=0):
    b, l, _ = qn.shape
    past = cache_k.shape[2]
    nt = past // tk
    kw = ATT_KV * 2 * ATT_D
    ck = cache_k.reshape(cache_k.shape[0], b, past, kw).astype(BF16)
    cv = cache_v.reshape(cache_v.shape[0], b, past, kw).astype(BF16)
    args = [qn, ck, cv, knb, vb, bias_past, bias_new, lam_p, g_subln.reshape(1, ATT_VD)]
    specs = [
        pl.BlockSpec((1, l, ATT_H * 2 * ATT_D), lambda bi, j, cls: (bi, 0, 0)),
        pl.BlockSpec((None, None, tk, kw), lambda bi, j, cls: (layer, bi, jnp.minimum(j, nt - 1), 0)),
        pl.BlockSpec((None, None, tk, kw), lambda bi, j, cls: (layer, bi, jnp.minimum(j, nt - 1), 0)),
        pl.BlockSpec((1, l, kw), lambda bi, j, cls: (bi, 0, 0)),
        pl.BlockSpec((1, l, kw), lambda bi, j, cls: (bi, 0, 0)),
        pl.BlockSpec((1, ATT_H, l, tk), lambda bi, j, cls: (cls[jnp.minimum(j, nt - 1)], 0, 0, 0)),
        pl.BlockSpec((1, ATT_H, l, l), lambda bi, j, cls: (0, 0, 0, 0)),
        pl.BlockSpec((4, ATT_D), lambda bi, j, cls: (0, 0)),
        pl.BlockSpec((1, ATT_VD), lambda bi, j, cls: (0, 0)),
    ]
    yshape, kern, aliases = _branch_out(
        ybuf, slab, b, l, args, specs,
        functools.partial(_attn_cached_kernel, tq=l, nt=nt, lam_init=lam_init), n_prefetch=1)
    grid_spec = pltpu.PrefetchScalarGridSpec(
        num_scalar_prefetch=1,
        grid=(b, nt + 1),
        in_specs=specs,
        out_specs=pl.BlockSpec((None, 1, l, ATT_H * ATT_VD), lambda bi, j, cls: (slab, bi, 0, 0)),
        scratch_shapes=[pltpu.VMEM((ATT_KV, 4 * l, LANES), F32), pltpu.VMEM((ATT_KV, 4 * l, LANES), F32),
                        pltpu.VMEM((ATT_KV, 4 * l, ATT_VD), F32)],
    )
    return pl.pallas_call(
        kern,
        out_shape=yshape,
        grid_spec=grid_spec,
        input_output_aliases=aliases,
        compiler_params=_cparams(("parallel", "arbitrary")),
        name="diff_attn_cached",
    )(cls_tbl, *args)


def _xattn_kernel(q_ref, mk_ref, mv_ref, o_ref):
    dn_t = (((1,), (1,)), ((), ()))
    outs = []
    for h in range(X_H):
        sl = slice(h * X_D, (h + 1) * X_D)
        s = lax.dot_general(q_ref[0, :, sl], mk_ref[0, :, sl].astype(BF16), dn_t,
                            preferred_element_type=F32) * (X_D ** -0.5)
        s = s - jnp.max(s, axis=-1, keepdims=True)
        p = jnp.exp(s)
        p = p / jnp.sum(p, axis=-1, keepdims=True)
        outs.append(jnp.dot(p.astype(BF16), mv_ref[0, :, sl].astype(BF16), preferred_element_type=F32))
    o_ref[0] = jnp.concatenate(outs, axis=-1).astype(o_ref.dtype)


def _xattn(q, mk, mv, *, tq):
    b, l, w = q.shape
    mlen = mk.shape[1]
    tq = min(tq, l)
    assert l % tq == 0
    return pl.pallas_call(
        _xattn_kernel,
        out_shape=jax.ShapeDtypeStruct((b, l, w), BF16),
        grid=(b, l // tq),
        in_specs=[
            pl.BlockSpec((1, tq, w), lambda i, c: (i, c, 0)),
            pl.BlockSpec((1, mlen, w), lambda i, c: (i, 0, 0)),
            pl.BlockSpec((1, mlen, w), lambda i, c: (i, 0, 0)),
        ],
        out_specs=pl.BlockSpec((1, tq, w), lambda i, c: (i, c, 0)),
        compiler_params=_cparams(("parallel", "parallel")),
        name="mem_cross_attn",
    )(q, mk, mv)


def _prep_weights(p, depth):
    layers = []
    off_z, off_xbc = 0, SSM_D
    off_dt = off_xbc + XBC_W
    off_pool = off_dt + SSM_H
    off_q = off_pool + BRANCH_W
    off_k = off_q + ATT_H * 2 * ATT_D
    off_v = off_k + ATT_KV * 2 * ATT_D
    off_gu = off_v + ATT_KV * ATT_VD
    off_gv = off_gu + BRANCH_W
    in_w = off_gv + BRANCH_W
    for l in range(depth):
        w = p['w_in'][l]
        d = w.shape[0]
        w_in = jnp.concatenate([
            w[:, off_xbc:off_dt], w[:, off_z:off_xbc], w[:, off_pool:off_q], w[:, off_q:off_k],
            w[:, off_gu:off_gv], w[:, off_gv:in_w], w[:, off_k:off_v], w[:, off_v:off_gu],
            w[:, off_dt:off_pool], jnp.zeros((d, LANES - SSM_H), w.dtype)], axis=1).astype(BF16)
        pad = lambda v: jnp.pad(v, (0, LANES - SSM_H)).reshape(1, LANES)
        layers.append(dict(
            g_ffn1=p['g_ffn1'][l], ffn1_in=p['w_ffn1_in'][l].astype(BF16), ffn1_out=p['w_ffn1_out'][l].astype(BF16),
            g_mix=p['g_mix'][l], w_in=w_in,
            conv_w=p['conv_w'][l], conv_b=p['conv_b'][l].reshape(1, XBC_W),
            dt_bias=pad(p['dt_bias'][l]), a_log=pad(p['a_log'][l]),
            d_skip=jnp.repeat(p['d_skip'][l], SSM_P).reshape(1, SSM_D), g_ssd=p['g_ssd'][l].reshape(1, SSM_D),
            w_pool=p['w_pool'][l].astype(BF16), pool_scale=p['pool_scale'][l].reshape(1, BRANCH_W),
            g_q=p['g_q'][l], g_k=p['g_k'][l], lam=p['lam'][l], g_subln=p['g_subln'][l],
            g_gv=p['g_gv'][l].reshape(1, BRANCH_W), w_sp=p['w_sp'][l], b_sp=p['b_sp'][l],
            w_gate=p['w_gate'][l].astype(BF16), b_gate=p['b_gate'][l], w_branch=p['w_branch'][l].astype(BF16),
            w_out=p['w_out'][l].astype(BF16),
            g_x=p['g_x'][l], w_xq=p['w_xq'][l].astype(BF16), g_xq=p['g_xq'][l], w_xo=p['w_xo'][l].astype(BF16),
            g_ffn2=p['g_ffn2'][l], ffn2_in=p['w_ffn2_in'][l].astype(BF16), ffn2_out=p['w_ffn2_out'][l].astype(BF16),
            g_post=p['g_post'][l],
            g_mem=p['g_mem'][l], w_mk=p['w_mk'][l].astype(BF16), w_mv=p['w_mv'][l].astype(BF16), g_xk=p['g_xk'][l],
        ))
    return layers


def _run_trunk(x, cache_k, cache_v, mem_k, mem_v, ssm0, conv0, pool0, layers, rel_bias):
    b, l, d = x.shape
    m = b * l
    depth = len(layers)
    cached = cache_k is not None
    assert l >= POOL_BUF and l >= CONV_W - 1
    if cached:
        past = cache_k.shape[2]
        tk = min(512, past)
        assert past % tk == 0 and past % CHUNK == 0 and l <= CHUNK and (past % GM_CHUNK == 0)
        nt = past // tk
        offs, cls = [FAR_OFFSET], []
        for j in range(nt):
            off = j * tk - past
            if _is_far(off, l, tk):
                cls.append(0)
            else:
                offs.append(off)
                cls.append(len(offs) - 1)
        bias_past = _bias_tiles(rel_bias, offs, l, tk)
        cls_tbl = jnp.asarray(np.asarray(cls, np.int32))
        bias_new = _bias_tiles(rel_bias, [0], l, l)
    else:
        past = 0
        tq = min(512, l)
        assert l % tq == 0 and tq % CHUNK == 0
        offs = [0]
        if l > tq:
            offs.append(-tq)
        if l > 2 * tq:
            assert _is_far(-2 * tq, tq, tq)
            offs.append(FAR_OFFSET)
        bias_self = _bias_tiles(rel_bias, offs, tq, tq, keys_major=True)

    h = x.reshape(m, d)
    new_k, new_v, new_ssm, new_conv, new_pool, new_gv = [], [], [], [], [], []
    for li, w in enumerate(layers):
        lam_init = 0.8 - 0.6 * math.exp(-0.3 * li)
        h = _ffn(h, w['g_ffn1'], w['ffn1_in'], w['ffn1_out'])
        proj = _mm(h, w['w_in'], g=w['g_mix'], out_dtype=F32, tm=512, tn=1664, name="in_proj")
        proj = proj.reshape(b, l, P_W)

        ys, s_ssm = _ssd(proj, conv0[li] if cached else None, ssm0[li] if cached else None,
                         w['conv_w'], w['conv_b'], w['dt_bias'], w['a_log'], w['d_skip'], w['g_ssd'], q=128)
        ys = _pool(proj, pool0[li] if cached else None, past, w['w_pool'], w['pool_scale'], t=256,
                   ybuf=ys, slab=1)
        if cached:
            qn, kn, knb, vb = _qknorm(proj, w['g_q'], w['g_k'], t=512)
            ys = _attn_cached(qn, knb, vb, cache_k, cache_v, li, bias_past, cls_tbl, bias_new,
                              w['lam'], w['g_subln'], lam_init, tk=tk, ybuf=ys, slab=2)
        else:
            qt, kn, knb, vt = _qknorm_t(proj, w['g_q'], w['g_k'], t=tq)
            ys = _attn_self(qt, knb, vt, bias_self, w['lam'], w['g_subln'], lam_init, tq=tq, ybuf=ys, slab=2)
        ys, v_gm = _gmlp(proj, w['g_gv'], w['w_sp'], w['b_sp'], keep_vn=cached, ybuf=ys, slab=3)

        merged = _merge(h, w['g_mix'], ys.reshape(N_BRANCH, m, BRANCH_W), w['w_gate'], w['b_gate'], w['w_branch'])
        h = _mm(merged, w['w_out'], res=h, out_dtype=F32, tm=512, tn=1024, name="out_proj", w_resident=True)

        qx = _mm(h, w['w_xq'], g=w['g_x'], gh=w['g_xq'], out_dtype=BF16, tm=512, tn=512, name="xattn_q")
        o = _xattn(qx.reshape(b, l, X_H * X_D), mem_k[li].reshape(b, -1, X_H * X_D),
                   mem_v[li].reshape(b, -1, X_H * X_D), tq=512)
        h = _mm(o.reshape(m, X_H * X_D), w['w_xo'], res=h, out_dtype=F32, tm=512, tn=1024, name="xattn_o")
        h = _ffn(h, w['g_ffn2'], w['ffn2_in'], w['ffn2_out'], post_g=w['g_post'])

        new_k.append(kn.reshape(b, l, ATT_KV, 2, ATT_D))
        new_v.append(proj[:, :, P_V:P_V + ATT_KV * ATT_VD].reshape(b, l, ATT_KV, ATT_VD))
        new_ssm.append(s_ssm)
        new_conv.append(proj[:, l - (CONV_W - 1):, P_XBC:P_XBC + XBC_W])
        new_pool.append(proj[:, l - POOL_BUF:, P_POOL:P_POOL + BRANCH_W])
        new_gv.append(v_gm)
    return (h.reshape(b, l, d), jnp.stack(new_k), jnp.stack(new_v), jnp.stack(new_ssm),
            jnp.stack(new_conv), jnp.stack(new_pool), jnp.stack(new_gv) if cached else None)


def kernel(x_prompt, x_sample, cache_attn_k, cache_attn_v, cache_mem_k, cache_mem_v, state_ssm, state_conv, state_pool, mem_prompt, g_ffn1, w_ffn1_in, w_ffn1_out, g_mix, w_in, conv_w, conv_b, dt_bias, a_log, d_skip, g_ssd, w_pool, pool_scale, g_q, g_k, lam, g_subln, rel_bias, g_gv, w_sp, b_sp, w_gate, b_gate, w_branch, w_out, g_x, w_xq, g_xq, g_mem, w_mk, w_mv, g_xk, w_xo, g_ffn2, w_ffn2_in, w_ffn2_out, g_post):
    p = dict(g_ffn1=g_ffn1, w_ffn1_in=w_ffn1_in, w_ffn1_out=w_ffn1_out, g_mix=g_mix, w_in=w_in,
             conv_w=conv_w, conv_b=conv_b, dt_bias=dt_bias, a_log=a_log, d_skip=d_skip, g_ssd=g_ssd,
             w_pool=w_pool, pool_scale=pool_scale, g_q=g_q, g_k=g_k, lam=lam, g_subln=g_subln,
             g_gv=g_gv, w_sp=w_sp, b_sp=b_sp, w_gate=w_gate, b_gate=b_gate, w_branch=w_branch,
             w_out=w_out, g_x=g_x, w_xq=w_xq, g_xq=g_xq, g_mem=g_mem, w_mk=w_mk, w_mv=w_mv, g_xk=g_xk,
             w_xo=w_xo, g_ffn2=g_ffn2, w_ffn2_in=w_ffn2_in, w_ffn2_out=w_ffn2_out, g_post=g_post)
    depth = w_in.shape[0]
    layers = _prep_weights(p, depth)

    bp, mlen, d = mem_prompt.shape
    mem2 = mem_prompt.reshape(bp * mlen, d)
    mks, mvs = [], []
    for w in layers:
        mk = _mm(mem2, w['w_mk'], g=w['g_mem'], gh=w['g_xk'], out_dtype=F32, tm=512, tn=512, name="mem_k")
        mv = _mm(mem2, w['w_mv'], g=w['g_mem'], out_dtype=F32, tm=512, tn=512, name="mem_v")
        mks.append(mk.reshape(bp, mlen, X_H, X_D))
        mvs.append(mv.reshape(bp, mlen, X_H, X_D))
    p_mem_k = jnp.stack(mks)
    p_mem_v = jnp.stack(mvs)

    y_prompt, p_attn_k, p_attn_v, p_ssm, p_conv, p_pool, _ = _run_trunk(
        x_prompt, None, None, p_mem_k, p_mem_v, None, None, None, layers, rel_bias)
    y_sample, s_attn_k, s_attn_v, s_ssm, s_conv, s_pool, s_gmlp_v = _run_trunk(
        x_sample, cache_attn_k, cache_attn_v, cache_mem_k, cache_mem_v, state_ssm, state_conv,
        state_pool, layers, rel_bias)

    return (y_prompt, y_sample, p_attn_k, p_attn_v, p_mem_k, p_mem_v, p_ssm, p_conv, p_pool,
            s_attn_k, s_attn_v, s_ssm, s_conv, s_pool, s_gmlp_v)
```

```python
import functools
import math

import jax
import jax.numpy as jnp
import numpy as np
from jax import lax
from jax.experimental import pallas as pl
from jax.experimental.pallas import tpu as pltpu

F32 = jnp.float32
BF16 = jnp.bfloat16
EPS = 1e-6
NEG = -1e30

VMEM_LIMIT_BYTES = 56 * 1024 * 1024
LANES = 128

BRANCH_W = 1024
SSM_P = 64
SSM_H = 16
SSM_G = 4
SSM_N = 128
SSM_D = SSM_H * SSM_P
CONV_W = 4
XBC_W = SSM_D + 2 * SSM_G * SSM_N
POOL_WINDOWS = (2, 4, 8, 16)
POOL_GW = BRANCH_W // len(POOL_WINDOWS)
POOL_BUF = max(POOL_WINDOWS) - 1
ATT_H = 8
ATT_KV = 4
ATT_D = 64
ATT_VD = 128
CHUNK = 64
N_BUCKETS = 32
MAX_DIST = 128
GM_CHUNK = 128
GM_G = 4
GM_GW = BRANCH_W // GM_G
X_H = 4
X_D = 128

P_XBC = 0
P_Z = 2048
P_POOL = 3072
P_Q = 4096
P_GU = 5120
P_GV = 6144
P_K = 7168
P_V = 7680
P_DT = 8192
P_W = 8320


def _cparams(sem):
    return pltpu.CompilerParams(dimension_semantics=sem, vmem_limit_bytes=VMEM_LIMIT_BYTES)


N_BRANCH = 4


def _skip_ref(fn, pos):
    def wrapped(*refs):
        return fn(*refs[:pos], *refs[pos + 1:])
    return wrapped


def _branch_out(ybuf, slab, b, l, args, specs, kernel_fn, n_prefetch=0):
    shape = jax.ShapeDtypeStruct((N_BRANCH, b, l, BRANCH_W), BF16)
    if ybuf is None:
        return shape, kernel_fn, {}
    assert slab > 0 and ybuf.shape == shape.shape and ybuf.dtype == shape.dtype
    pos = n_prefetch + len(args)
    args.append(ybuf)
    specs.append(pl.BlockSpec(memory_space=pl.ANY))
    return shape, _skip_ref(kernel_fn, pos), {pos: 0}


def _rms(xf, g):
    ms = jnp.mean(xf * xf, axis=-1, keepdims=True)
    return xf * lax.rsqrt(ms + EPS) * g


def _silu(x):
    return x * jax.nn.sigmoid(x)


def _mm_kernel(*refs, norm, head_norm, residual):
    it = iter(refs)
    x_ref = next(it)
    g_ref = next(it) if norm else None
    w_ref = next(it)
    gh_ref = next(it) if head_norm else None
    res_ref = next(it) if residual else None
    o_ref = next(it)
    xn_ref = next(it) if norm else None

    if norm:
        @pl.when(pl.program_id(1) == 0)
        def _():
            xn_ref[...] = _rms(x_ref[...].astype(F32), g_ref[...]).astype(BF16)
        xb = xn_ref[...]
    else:
        xb = x_ref[...].astype(BF16)
    acc = jnp.dot(xb, w_ref[...], preferred_element_type=F32)
    if head_norm:
        parts = []
        for c in range(acc.shape[1] // LANES):
            parts.append(_rms(acc[:, c * LANES:(c + 1) * LANES], gh_ref[...]))
        acc = jnp.concatenate(parts, axis=-1)
    if residual:
        acc = res_ref[...] + acc
    o_ref[...] = acc.astype(o_ref.dtype)


def _mm(x, w, *, g=None, gh=None, res=None, out_dtype, tm, tn, name, w_resident=False):
    m, k = x.shape
    n = w.shape[1]
    tm = min(tm, m)
    tn = min(tn, n)
    assert m % tm == 0 and n % tn == 0
    norm, head_norm, residual = g is not None, gh is not None, res is not None
    assert not (norm and w_resident)
    if w_resident:
        grid = (n // tn, m // tm)
        ij = lambda a, b: (b, a)
    else:
        grid = (m // tm, n // tn)
        ij = lambda a, b: (a, b)
    args, specs = [x], [pl.BlockSpec((tm, k), lambda a, b: (ij(a, b)[0], 0))]
    if norm:
        args.append(g.reshape(1, k))
        specs.append(pl.BlockSpec((1, k), lambda a, b: (0, 0)))
    args.append(w)
    specs.append(pl.BlockSpec((k, tn), lambda a, b: (0, ij(a, b)[1])))
    if head_norm:
        args.append(gh.reshape(1, LANES))
        specs.append(pl.BlockSpec((1, LANES), lambda a, b: (0, 0)))
    if residual:
        args.append(res)
        specs.append(pl.BlockSpec((tm, tn), lambda a, b: ij(a, b)))
    scratch = [pltpu.VMEM((tm, k), BF16)] if norm else []
    return pl.pallas_call(
        functools.partial(_mm_kernel, norm=norm, head_norm=head_norm, residual=residual),
        out_shape=jax.ShapeDtypeStruct((m, n), out_dtype),
        grid=grid,
        in_specs=specs,
        out_specs=pl.BlockSpec((tm, tn), lambda a, b: ij(a, b)),
        scratch_shapes=scratch,
        compiler_params=_cparams(("parallel", "arbitrary")),
        name=name,
    )(*args)


def _ffn_kernel(*refs, post, pre):
    it = iter(refs)
    x_ref, g_ref, wg_ref, wu_ref, wo_ref = (next(it) for _ in range(5))
    gp_ref = next(it) if post else None
    a_ref, wa_ref = (next(it), next(it)) if pre else (None, None)
    o_ref, xn_ref, acc_ref = next(it), next(it), next(it)
    f = pl.program_id(1)

    @pl.when(f == 0)
    def _():
        x = x_ref[...]
        if pre:
            x = x + jnp.dot(a_ref[...], wa_ref[...], preferred_element_type=F32)
            o_ref[...] = x
        xn_ref[...] = _rms(x, g_ref[...]).astype(BF16)
        acc_ref[...] = jnp.zeros_like(acc_ref)

    xb = xn_ref[...]
    gate = jnp.dot(xb, wg_ref[...], preferred_element_type=F32)
    up = jnp.dot(xb, wu_ref[...], preferred_element_type=F32)
    mid = (_silu(gate) * up).astype(BF16)
    acc_ref[...] += jnp.dot(mid, wo_ref[...], preferred_element_type=F32)

    @pl.when(f == pl.num_programs(1) - 1)
    def _():
        h = (o_ref[...] if pre else x_ref[...]) + 0.5 * acc_ref[...]
        if post:
            h = _rms(h, gp_ref[...])
        o_ref[...] = h


def _ffn(x, g, w_in, w_out, *, post_g=None, pre=None, tm=512, tf=512):
    m, d = x.shape
    ff = w_out.shape[0]
    tm = min(tm, m)
    assert m % tm == 0 and ff % tf == 0
    nf = ff // tf
    post = post_g is not None
    args = [x, g.reshape(1, d), w_in, w_in, w_out]
    specs = [
        pl.BlockSpec((tm, d), lambda i, f: (i, 0)),
        pl.BlockSpec((1, d), lambda i, f: (0, 0)),
        pl.BlockSpec((d, tf), lambda i, f: (0, f)),
        pl.BlockSpec((d, tf), lambda i, f: (0, nf + f)),
        pl.BlockSpec((tf, d), lambda i, f: (f, 0)),
    ]
    if post:
        args.append(post_g.reshape(1, d))
        specs.append(pl.BlockSpec((1, d), lambda i, f: (0, 0)))
    if pre is not None:
        a, wa = pre
        ka = a.shape[1]
        args += [a, wa]
        specs += [pl.BlockSpec((tm, ka), lambda i, f: (i, 0)), pl.BlockSpec((ka, d), lambda i, f: (0, 0))]
    return pl.pallas_call(
        functools.partial(_ffn_kernel, post=post, pre=pre is not None),
        out_shape=jax.ShapeDtypeStruct((m, d), F32),
        grid=(m // tm, nf),
        in_specs=specs,
        out_specs=pl.BlockSpec((tm, d), lambda i, f: (i, 0)),
        scratch_shapes=[pltpu.VMEM((tm, d), BF16), pltpu.VMEM((tm, d), F32)],
        compiler_params=_cparams(("parallel", "arbitrary")),
        name="swiglu_ffn",
    )(*args)


def _merge_kernel(h_ref, g_ref, y_ref, wg_ref, bg_ref, wb_ref, o_ref, hn_ref, acc_ref):
    j = pl.program_id(1)
    br = pl.program_id(2)

    @pl.when((j == 0) & (br == 0))
    def _():
        hn_ref[...] = _rms(h_ref[...], g_ref[...]).astype(BF16)

    @pl.when(br == 0)
    def _():
        acc_ref[...] = jnp.zeros_like(acc_ref)

    gate = jax.nn.sigmoid(jnp.dot(hn_ref[...], wg_ref[...], preferred_element_type=F32) + bg_ref[...])
    acc_ref[...] += gate * jnp.dot(y_ref[...], wb_ref[...], preferred_element_type=F32)

    @pl.when(br == pl.num_programs(2) - 1)
    def _():
        o_ref[...] = acc_ref[...].astype(o_ref.dtype)


def _merge(h, g, ys, w_gate, b_gate, w_branch, *, tm=512, tn=1024):
    m, d = h.shape
    nb, bw, _ = w_branch.shape
    tm = min(tm, m)
    assert m % tm == 0 and d % tn == 0 and ys.shape == (nb, m, bw)
    return pl.pallas_call(
        _merge_kernel,
        out_shape=jax.ShapeDtypeStruct((m, d), BF16),
        grid=(m // tm, d // tn, nb),
        in_specs=[
            pl.BlockSpec((tm, d), lambda i, j, b: (i, 0)),
            pl.BlockSpec((1, d), lambda i, j, b: (0, 0)),
            pl.BlockSpec((None, tm, bw), lambda i, j, b: (b, i, 0)),
            pl.BlockSpec((None, d, tn), lambda i, j, b: (b, 0, j)),
            pl.BlockSpec((None, 1, tn), lambda i, j, b: (b, 0, j)),
            pl.BlockSpec((None, bw, tn), lambda i, j, b: (b, 0, j)),
        ],
        out_specs=pl.BlockSpec((tm, tn), lambda i, j, b: (i, j)),
        scratch_shapes=[pltpu.VMEM((tm, d), BF16), pltpu.VMEM((tm, tn), F32)],
        compiler_params=_cparams(("parallel", "arbitrary", "arbitrary")),
        name="branch_merge",
    )(h, g.reshape(1, d), ys, w_gate, b_gate.reshape(nb, 1, d), w_branch)


def _ssd_kernel(*refs, q, has_state):
    it = iter(refs)
    xbc_ref, z_ref, dt_ref = next(it), next(it), next(it)
    conv0_ref = next(it) if has_state else None
    s0_ref = next(it) if has_state else None
    cw_ref, cb_ref, dtb_ref, alog_ref, dskip_ref, gssd_ref = (next(it) for _ in range(6))
    y_ref, s_ref, cbuf = next(it), next(it), next(it)
    hist = 8

    @pl.when(pl.program_id(1) == 0)
    def _():
        cbuf[0:hist, :] = jnp.zeros((hist, XBC_W), F32)
        if has_state:
            cbuf[hist - (CONV_W - 1):hist, :] = conv0_ref[0]
            s_ref[0] = s0_ref[0]
        else:
            s_ref[0] = jnp.zeros(s_ref.shape[1:], F32)

    cbuf[hist:hist + q, :] = xbc_ref[0]
    conv = cb_ref[...]
    for k in range(CONV_W):
        lo = hist - (CONV_W - 1) + k
        conv = conv + cbuf[lo:lo + q, :] * cw_ref[k:k + 1, :]
    cbuf[0:hist, :] = cbuf[q:q + hist, :]
    xc = _silu(conv)
    xs = xc[:, :SSM_D]
    bm = xc[:, SSM_D:SSM_D + SSM_G * SSM_N]
    cm = xc[:, SSM_D + SSM_G * SSM_N:]

    dtl = dt_ref[0] + dtb_ref[...]
    dt = jnp.maximum(dtl, 0.0) + jnp.log1p(jnp.exp(-jnp.abs(dtl)))
    a = dt * (-jnp.exp(alog_ref[...]))
    ri = lax.broadcasted_iota(jnp.int32, (q, q), 0)
    ci = lax.broadcasted_iota(jnp.int32, (q, q), 1)
    tril = ri >= ci
    cum = jnp.dot(tril.astype(F32), a, preferred_element_type=F32,
                  precision=lax.Precision.HIGHEST)
    cum_t = cum.T
    dt_t = dt.T
    cum_last = cum[q - 1:q, :]
    w_state = jnp.exp(cum_last - cum) * dt
    ecum = jnp.exp(cum)
    ecl = jnp.exp(cum_last)

    lane_lo = lax.broadcasted_iota(jnp.int32, (q, LANES), 1) < SSM_P
    row_lo_t = lax.broadcasted_iota(jnp.int32, (LANES, q), 0) < SSM_P
    row_lo_s = lax.broadcasted_iota(jnp.int32, (LANES, SSM_N), 0) < SSM_P
    dn_t = (((1,), (1,)), ((), ()))

    y_parts = []
    cb = None
    heads_per_group = SSM_H // SSM_G
    for pp in range(SSM_H // 2):
        grp = (2 * pp) // heads_per_group
        xp = xs[:, pp * LANES:(pp + 1) * LANES]
        bg = bm[:, grp * SSM_N:(grp + 1) * SSM_N]
        cg = cm[:, grp * SSM_N:(grp + 1) * SSM_N]
        cg_b = cg.astype(BF16)
        if (2 * pp) % heads_per_group == 0:
            cb = lax.dot_general(cg_b, bg.astype(BF16), dn_t, preferred_element_type=F32)
        xp_t = xp.T
        s_pair = s_ref[0, pp * LANES:(pp + 1) * LANES, :]
        y_pair = None
        s_new = None
        for hh in range(2):
            hd = 2 * pp + hh
            seg = cum[:, hd:hd + 1] - cum_t[hd:hd + 1, :]
            mat = jnp.exp(jnp.where(tril, seg, NEG)) * cb * dt_t[hd:hd + 1, :]
            sel = lane_lo if hh == 0 else jnp.logical_not(lane_lo)
            x_h = jnp.where(sel, xp, 0.0).astype(BF16)
            yd = jnp.dot(mat.astype(BF16), x_h, preferred_element_type=F32)
            y_pair = yd if y_pair is None else y_pair + yd
            sel_t = row_lo_t if hh == 0 else jnp.logical_not(row_lo_t)
            xt_h = jnp.where(sel_t, xp_t, 0.0).astype(BF16)
            bs = (bg * w_state[:, hd:hd + 1]).astype(BF16)
            sn = jnp.dot(xt_h, bs, preferred_element_type=F32)
            s_new = sn if s_new is None else s_new + sn
        y_off = lax.dot_general(cg_b, s_pair.astype(BF16), dn_t, preferred_element_type=F32)
        e_pair = jnp.where(lane_lo, ecum[:, 2 * pp:2 * pp + 1], ecum[:, 2 * pp + 1:2 * pp + 2])
        y_parts.append(y_pair + y_off * e_pair)
        dec = jnp.where(row_lo_s, ecl[:, 2 * pp:2 * pp + 1], ecl[:, 2 * pp + 1:2 * pp + 2])
        s_ref[0, pp * LANES:(pp + 1) * LANES, :] = dec * s_pair + s_new

    y = jnp.concatenate(y_parts, axis=-1)
    y = y + dskip_ref[...] * xs
    y = y * _silu(z_ref[0])
    gw = SSM_D // SSM_G
    outs = [_rms(y[:, gi * gw:(gi + 1) * gw], gssd_ref[:, gi * gw:(gi + 1) * gw]) for gi in range(SSM_G)]
    y_ref[0] = jnp.concatenate(outs, axis=-1).astype(y_ref.dtype)


def _ssd(proj, conv0, s0, cw, cb, dtb, alog, dskip, gssd, *, q, ybuf=None, slab=0):
    b, l, _ = proj.shape
    q = min(q, l)
    assert l % q == 0 and q >= 8
    has_state = s0 is not None
    args = [proj, proj, proj]
    specs = [
        pl.BlockSpec((1, q, XBC_W), lambda i, c: (i, c, P_XBC // XBC_W)),
        pl.BlockSpec((1, q, SSM_D), lambda i, c: (i, c, P_Z // SSM_D)),
        pl.BlockSpec((1, q, LANES), lambda i, c: (i, c, P_DT // LANES)),
    ]
    if has_state:
        args += [conv0, s0.reshape(b, SSM_H * SSM_P, SSM_N)]
        specs += [
            pl.BlockSpec((1, CONV_W - 1, XBC_W), lambda i, c: (i, 0, 0)),
            pl.BlockSpec((1, SSM_H * SSM_P, SSM_N), lambda i, c: (i, 0, 0)),
        ]
    args += [cw, cb, dtb, alog, dskip, gssd]
    specs += [
        pl.BlockSpec((CONV_W, XBC_W), lambda i, c: (0, 0)),
        pl.BlockSpec((1, XBC_W), lambda i, c: (0, 0)),
        pl.BlockSpec((1, LANES), lambda i, c: (0, 0)),
        pl.BlockSpec((1, LANES), lambda i, c: (0, 0)),
        pl.BlockSpec((1, SSM_D), lambda i, c: (0, 0)),
        pl.BlockSpec((1, SSM_D), lambda i, c: (0, 0)),
    ]
    yshape, kern, aliases = _branch_out(ybuf, slab, b, l, args, specs,
                                        functools.partial(_ssd_kernel, q=q, has_state=has_state))
    y, s = pl.pallas_call(
        kern,
        out_shape=(yshape, jax.ShapeDtypeStruct((b, SSM_H * SSM_P, SSM_N), F32)),
        grid=(b, l // q),
        in_specs=specs,
        out_specs=(pl.BlockSpec((None, 1, q, SSM_D), lambda i, c: (slab, i, c, 0)),
                   pl.BlockSpec((1, SSM_H * SSM_P, SSM_N), lambda i, c: (i, 0, 0))),
        scratch_shapes=[pltpu.VMEM((q + 8, XBC_W), F32)],
        input_output_aliases=aliases,
        compiler_params=_cparams(("parallel", "arbitrary")),
        name="ssd_mixer",
    )(*args)
    return y, s.reshape(b, SSM_H, SSM_P, SSM_N)


def _pool_kernel(*refs, t, pos0, has_state):
    it = iter(refs)
    xp_ref = next(it)
    p0_ref = next(it) if has_state else None
    w_ref, sc_ref, y_ref, buf = next(it), next(it), next(it), next(it)
    hist = POOL_BUF + 1
    ti = pl.program_id(1)

    @pl.when(ti == 0)
    def _():
        buf[0:hist, :] = jnp.zeros((hist, BRANCH_W), F32)
        if has_state:
            buf[1:hist, :] = p0_ref[0]

    xp = xp_ref[0]
    buf[hist:hist + t, :] = xp
    pos = pos0 + ti * t + lax.broadcasted_iota(jnp.int32, (t, POOL_GW), 0)
    outs = []
    for gi, w in enumerate(POOL_WINDOWS):
        c0 = gi * POOL_GW
        s = xp[:, c0:c0 + POOL_GW]
        for k in range(1, w):
            s = s + buf[hist - k:hist - k + t, c0:c0 + POOL_GW]
        cnt = jnp.minimum(w, pos + 1).astype(F32)
        pooled = s / cnt - xp[:, c0:c0 + POOL_GW]
        y = jnp.dot(pooled.astype(BF16), w_ref[gi], preferred_element_type=F32)
        outs.append(y * sc_ref[:, c0:c0 + POOL_GW])
    y_ref[0] = jnp.concatenate(outs, axis=-1).astype(y_ref.dtype)
    buf[0:hist, :] = buf[t:t + hist, :]


def _pool(proj, pool0, pos0, w_pool, scale, *, t, ybuf=None, slab=0):
    b, l, _ = proj.shape
    t = min(t, l)
    assert l % t == 0 and t >= POOL_BUF + 1
    has_state = pool0 is not None
    args = [proj]
    specs = [pl.BlockSpec((1, t, BRANCH_W), lambda i, c: (i, c, P_POOL // BRANCH_W))]
    if has_state:
        args.append(pool0)
        specs.append(pl.BlockSpec((1, POOL_BUF, BRANCH_W), lambda i, c: (i, 0, 0)))
    args += [w_pool, scale]
    specs += [
        pl.BlockSpec((len(POOL_WINDOWS), POOL_GW, POOL_GW), lambda i, c: (0, 0, 0)),
        pl.BlockSpec((1, BRANCH_W), lambda i, c: (0, 0)),
    ]
    yshape, kern, aliases = _branch_out(
        ybuf, slab, b, l, args, specs, functools.partial(_pool_kernel, t=t, pos0=pos0, has_state=has_state))
    return pl.pallas_call(
        kern,
        out_shape=yshape,
        grid=(b, l // t),
        in_specs=specs,
        out_specs=pl.BlockSpec((None, 1, t, BRANCH_W), lambda i, c: (slab, i, c, 0)),
        scratch_shapes=[pltpu.VMEM((t + POOL_BUF + 1, BRANCH_W), F32)],
        input_output_aliases=aliases,
        compiler_params=_cparams(("parallel", "arbitrary")),
        name="pool_mixer",
    )(*args)


def _gmlp_kernel(u_ref, v_ref, g_ref, w_ref, b_ref, y_ref, *rest, cl, nsub, keep_vn):
    tril = lax.broadcasted_iota(jnp.int32, (cl, cl), 0) >= lax.broadcasted_iota(jnp.int32, (cl, cl), 1)
    ws = [jnp.where(tril, w_ref[gi], 0.0).astype(BF16) for gi in range(GM_G)]
    for sub in range(nsub):
        rows = slice(sub * cl, (sub + 1) * cl)
        u = jax.nn.gelu(u_ref[0, rows, :], approximate=True)
        vn = _rms(jax.nn.gelu(v_ref[0, rows, :], approximate=True), g_ref[...])
        if keep_vn:
            rest[0][0, rows, :] = vn
        outs = []
        for gi in range(GM_G):
            s = jnp.dot(ws[gi], vn[:, gi * GM_GW:(gi + 1) * GM_GW].astype(BF16), preferred_element_type=F32)
            s = s + b_ref[:, gi:gi + 1]
            outs.append(u[:, gi * GM_GW:(gi + 1) * GM_GW] * s)
        y_ref[0, rows, :] = jnp.concatenate(outs, axis=-1).astype(y_ref.dtype)


def _gmlp(proj, g_gv, w_sp, b_sp, *, keep_vn, ybuf=None, slab=0):
    b, l, _ = proj.shape
    cl = min(GM_CHUNK, l)
    assert l % cl == 0
    nsub = math.gcd(l // cl, 4)
    t = cl * nsub
    w = w_sp[:, :cl, :cl]
    bt = b_sp[:, :cl].T
    args = [proj, proj, g_gv, w, bt]
    specs = [
        pl.BlockSpec((1, t, BRANCH_W), lambda i, c: (i, c, P_GU // BRANCH_W)),
        pl.BlockSpec((1, t, BRANCH_W), lambda i, c: (i, c, P_GV // BRANCH_W)),
        pl.BlockSpec((1, BRANCH_W), lambda i, c: (0, 0)),
        pl.BlockSpec((GM_G, cl, cl), lambda i, c: (0, 0, 0)),
        pl.BlockSpec((cl, GM_G), lambda i, c: (0, 0)),
    ]
    yshape, kern, aliases = _branch_out(
        ybuf, slab, b, l, args, specs, functools.partial(_gmlp_kernel, cl=cl, nsub=nsub, keep_vn=keep_vn))
    out_shape = [yshape]
    out_specs = [pl.BlockSpec((None, 1, t, BRANCH_W), lambda i, c: (slab, i, c, 0))]
    if keep_vn:
        out_shape.append(jax.ShapeDtypeStruct((b, l, BRANCH_W), F32))
        out_specs.append(pl.BlockSpec((1, t, BRANCH_W), lambda i, c: (i, c, 0)))
    outs = pl.pallas_call(
        kern,
        out_shape=tuple(out_shape),
        grid=(b, l // t),
        in_specs=specs,
        out_specs=tuple(out_specs),
        input_output_aliases=aliases,
        compiler_params=_cparams(("parallel", "parallel")),
        name="gmlp_mixer",
    )(*args)
    return outs[0], (outs[1] if keep_vn else None)


def _norm_halves(x, g):
    lo = lax.broadcasted_iota(jnp.int32, x.shape, 1) < ATT_D
    xx = x * x
    s_lo = jnp.sum(jnp.where(lo, xx, 0.0), axis=-1, keepdims=True)
    s_hi = jnp.sum(jnp.where(lo, 0.0, xx), axis=-1, keepdims=True)
    ms = jnp.where(lo, s_lo, s_hi) * (1.0 / ATT_D)
    return x * lax.rsqrt(ms + EPS) * g


def _qknorm_kernel(q_ref, k_ref, v_ref, gq_ref, gk_ref, qn_ref, kn_ref, knb_ref, vb_ref):
    scale = ATT_D ** -0.5
    q = q_ref[0]
    qs = [_norm_halves(q[:, c * LANES:(c + 1) * LANES], gq_ref[...]) * scale
          for c in range(q.shape[1] // LANES)]
    qn_ref[0] = jnp.concatenate(qs, axis=-1).astype(qn_ref.dtype)
    k = k_ref[0]
    ks = [_norm_halves(k[:, c * LANES:(c + 1) * LANES], gk_ref[...]) for c in range(k.shape[1] // LANES)]
    kn = jnp.concatenate(ks, axis=-1)
    kn_ref[0] = kn
    knb_ref[0] = kn.astype(BF16)
    vb_ref[0] = v_ref[0].astype(BF16)


def _qknorm(proj, g_q, g_k, *, t):
    b, l, _ = proj.shape
    t = min(t, l)
    assert l % t == 0
    qw, kw = ATT_H * 2 * ATT_D, ATT_KV * 2 * ATT_D
    gq = jnp.tile(g_q, 2).reshape(1, LANES)
    gk = jnp.tile(g_k, 2).reshape(1, LANES)
    return pl.pallas_call(
        _qknorm_kernel,
        out_shape=(jax.ShapeDtypeStruct((b, l, qw), BF16), jax.ShapeDtypeStruct((b, l, kw), F32),
                   jax.ShapeDtypeStruct((b, l, kw), BF16), jax.ShapeDtypeStruct((b, l, kw), BF16)),
        grid=(b, l // t),
        in_specs=[
            pl.BlockSpec((1, t, qw), lambda i, c: (i, c, P_Q // qw)),
            pl.BlockSpec((1, t, kw), lambda i, c: (i, c, P_K // kw)),
            pl.BlockSpec((1, t, kw), lambda i, c: (i, c, P_V // kw)),
            pl.BlockSpec((1, LANES), lambda i, c: (0, 0)),
            pl.BlockSpec((1, LANES), lambda i, c: (0, 0)),
        ],
        out_specs=(pl.BlockSpec((1, t, qw), lambda i, c: (i, c, 0)),
                   pl.BlockSpec((1, t, kw), lambda i, c: (i, c, 0)),
                   pl.BlockSpec((1, t, kw), lambda i, c: (i, c, 0)),
                   pl.BlockSpec((1, t, kw), lambda i, c: (i, c, 0))),
        compiler_params=_cparams(("parallel", "parallel")),
        name="qk_norm",
    )(proj, proj, proj, gq, gk)


def _bias_kernel(off_ref, tbl_ref, o_ref, *, tq, tk, keys_major):
    c = pl.program_id(0)
    h = pl.program_id(1)
    off = off_ref[c]
    shape = (tk, tq) if keys_major else (tq, tk)
    row = lax.broadcasted_iota(jnp.int32, shape, 1 if keys_major else 0)
    col = lax.broadcasted_iota(jnp.int32, shape, 0 if keys_major else 1)
    rel = off + col - row
    nb = N_BUCKETS // 2
    max_exact = nb // 2
    n = jnp.abs(rel)
    large = max_exact + (jnp.log(jnp.maximum(n, 1).astype(F32) / max_exact)
                         / math.log(MAX_DIST / max_exact) * (nb - max_exact)).astype(jnp.int32)
    large = jnp.minimum(large, nb - 1)
    bucket = jnp.where(rel > 0, nb, 0) + jnp.where(n < max_exact, n, large)
    val = jnp.zeros(shape, F32)
    for bk in range(N_BUCKETS):
        val = jnp.where(bucket == bk, tbl_ref[bk * ATT_H + h], val)
    visible = jnp.right_shift(off + col, 6) <= jnp.right_shift(row, 6)
    o_ref[0, 0] = jnp.where(visible, val, NEG)


def _bias_tiles(rel_bias, offsets, tq, tk, keys_major=False):
    assert CHUNK == 64
    offs = jnp.asarray(np.asarray(offsets, np.int32))
    n_cls = len(offsets)
    shape = (tk, tq) if keys_major else (tq, tk)
    return pl.pallas_call(
        functools.partial(_bias_kernel, tq=tq, tk=tk, keys_major=keys_major),
        out_shape=jax.ShapeDtypeStruct((n_cls, ATT_H) + shape, F32),
        grid=(n_cls, ATT_H),
        in_specs=[pl.BlockSpec(memory_space=pltpu.SMEM), pl.BlockSpec(memory_space=pltpu.SMEM)],
        out_specs=pl.BlockSpec((1, 1) + shape, lambda c, h: (c, h, 0, 0)),
        compiler_params=_cparams(("parallel", "parallel")),
        name="rel_bias_tiles",
    )(offs, rel_bias.reshape(-1))


FAR_OFFSET = -(1 << 20)


def _is_far(off, tq, tk):
    nb = N_BUCKETS // 2
    max_rel = off + tk - 1
    if max_rel >= 0:
        return False
    n_min = -max_rel
    sat = (nb // 2) * (MAX_DIST / (nb // 2)) ** ((nb - 1 - nb // 2) / (nb - nb // 2))
    return n_min >= math.ceil(sat) + 1


def _q4(q2):
    lo = lax.broadcasted_iota(jnp.int32, (q2.shape[0], LANES), 1) < ATT_D
    parts = []
    for r in range(2):
        qr = q2[:, r * LANES:(r + 1) * LANES]
        parts.append(jnp.where(lo, qr, jnp.zeros_like(qr)))
        parts.append(jnp.where(lo, jnp.zeros_like(qr), qr))
    return jnp.concatenate(parts, axis=0)


def _osm_softmax(s, b0, b1, m_ref, l_ref):
    s = s + jnp.concatenate([b0, b0, b1, b1], axis=0)
    m_prev = m_ref[...]
    m_new = jnp.maximum(m_prev, jnp.max(s, axis=-1, keepdims=True))
    alpha = jnp.exp(m_prev - m_new)
    p = jnp.exp(s - m_new[:, 0:1])
    l_ref[...] = alpha * l_ref[...] + jnp.sum(p, axis=-1, keepdims=True)
    m_ref[...] = m_new
    return p.astype(BF16), alpha


def _osm_finish(l_ref, acc_ref, lam_ref, gs_ref, lam_init, tq):
    lp = lam_ref[...]
    lam = (jnp.exp(jnp.sum(lp[0:1] * lp[1:2], axis=-1, keepdims=True))
           - jnp.exp(jnp.sum(lp[2:3] * lp[3:4], axis=-1, keepdims=True)) + lam_init)
    o = acc_ref[...] / l_ref[...]
    outs = []
    for r in range(2):
        d = o[(2 * r) * tq:(2 * r + 1) * tq] - lam * o[(2 * r + 1) * tq:(2 * r + 2) * tq]
        outs.append(_rms(d, gs_ref[...]) * (1.0 - lam_init))
    return jnp.concatenate(outs, axis=-1)


def _qknorm_t_kernel(q_ref, k_ref, v_ref, gq_ref, gk_ref, qt_ref, kn_ref, knb_ref, vt_ref):
    scale = ATT_D ** -0.5
    q = q_ref[0]
    t = q.shape[0]
    lo = lax.broadcasted_iota(jnp.int32, (t, LANES), 1) < ATT_D
    for kv in range(ATT_KV):
        cols = []
        for r in range(2):
            c = kv * 2 + r
            qr = _norm_halves(q[:, c * LANES:(c + 1) * LANES], gq_ref[...]) * scale
            cols.append(jnp.where(lo, qr, 0.0).T)
            cols.append(jnp.where(lo, 0.0, qr).T)
        qt_ref[0, kv] = jnp.concatenate(cols, axis=1).astype(qt_ref.dtype)
    k = k_ref[0]
    ks = [_norm_halves(k[:, c * LANES:(c + 1) * LANES], gk_ref[...]) for c in range(k.shape[1] // LANES)]
    kn = jnp.concatenate(ks, axis=-1)
    kn_ref[0] = kn
    knb_ref[0] = kn.astype(BF16)
    v = v_ref[0]
    for kv in range(ATT_KV):
        vt_ref[0, kv, 0] = v[:, kv * LANES:(kv + 1) * LANES].T.astype(vt_ref.dtype)


def _qknorm_t(proj, g_q, g_k, *, t):
    b, l, _ = proj.shape
    assert l % t == 0
    qw, kw = ATT_H * 2 * ATT_D, ATT_KV * 2 * ATT_D
    gq = jnp.tile(g_q, 2).reshape(1, LANES)
    gk = jnp.tile(g_k, 2).reshape(1, LANES)
    return pl.pallas_call(
        _qknorm_t_kernel,
        out_shape=(jax.ShapeDtypeStruct((b, ATT_KV, LANES, 4 * l), BF16),
                   jax.ShapeDtypeStruct((b, l, kw), F32),
                   jax.ShapeDtypeStruct((b, l, kw), BF16),
                   jax.ShapeDtypeStruct((b, ATT_KV, l // t, ATT_VD, t), BF16)),
        grid=(b, l // t),
        in_specs=[
            pl.BlockSpec((1, t, qw), lambda i, c: (i, c, P_Q // qw)),
            pl.BlockSpec((1, t, kw), lambda i, c: (i, c, P_K // kw)),
            pl.BlockSpec((1, t, kw), lambda i, c: (i, c, P_V // kw)),
            pl.BlockSpec((1, LANES), lambda i, c: (0, 0)),
            pl.BlockSpec((1, LANES), lambda i, c: (0, 0)),
        ],
        out_specs=(pl.BlockSpec((1, ATT_KV, LANES, 4 * t), lambda i, c: (i, 0, 0, c)),
                   pl.BlockSpec((1, t, kw), lambda i, c: (i, c, 0)),
                   pl.BlockSpec((1, t, kw), lambda i, c: (i, c, 0)),
                   pl.BlockSpec((1, ATT_KV, 1, ATT_VD, t), lambda i, c: (i, 0, c, 0, 0))),
        compiler_params=_cparams(("parallel", "parallel")),
        name="qk_norm_t",
    )(proj, proj, proj, gq, gk)


def _attn_self_kernel(qt_ref, k_ref, vt_ref, bias_ref, lam_ref, gs_ref, o_ref, m_ref, l_ref, acc_ref,
                      *, tq, tk, lam_init, n_cls, group):
    i = pl.program_id(2)
    m_ref[...] = jnp.full(m_ref.shape, NEG, F32)
    l_ref[...] = jnp.zeros(l_ref.shape, F32)
    acc_ref[...] = jnp.zeros(acc_ref.shape, F32)

    def step(j, far):
        kt = k_ref[0, pl.ds(pl.multiple_of(j * tk, tk), tk), :]
        vt = vt_ref[0, 0, j]
        cls = jnp.minimum(i - j, n_cls - 1)
        for c0 in range(0, 4, group):
            update(kt, vt, cls, far, range(c0, c0 + group))

    def update(kt, vt, cls, far, combos):
        cols = {c: slice(c * tq, (c + 1) * tq) for c in combos}
        ss = {c: jnp.dot(kt, qt_ref[0, 0, :, cs], preferred_element_type=F32) for c, cs in cols.items()}
        ps, alphas = {}, {}
        for c, cs in cols.items():
            s = ss[c]
            m_prev = m_ref[:, cs]
            if far:
                crow = bias_ref[cls, c // 2, 0:1, :]
                m_new = jnp.maximum(m_prev, jnp.max(s, axis=0, keepdims=True) + crow)
                p = jnp.exp(s - (m_new - crow))
            else:
                s = s + bias_ref[cls, c // 2]
                m_new = jnp.maximum(m_prev, jnp.max(s, axis=0, keepdims=True))
                p = jnp.exp(s - m_new)
            alpha = jnp.exp(m_prev - m_new)
            l_ref[:, cs] = alpha * l_ref[:, cs] + jnp.sum(p, axis=0, keepdims=True)
            m_ref[:, cs] = m_new
            ps[c] = p.astype(BF16)
            alphas[c] = alpha
        for c, cs in cols.items():
            acc_ref[:, cs] = alphas[c] * acc_ref[:, cs] + jnp.dot(vt, ps[c], preferred_element_type=F32)

    n_far = jnp.maximum(i - (n_cls - 2), 0) if n_cls == 3 else 0

    def far_body(j, carry):
        step(j, True)
        return carry

    def near_body(j, carry):
        step(j, False)
        return carry

    if n_cls == 3:
        lax.fori_loop(0, n_far, far_body, 0)
    lax.fori_loop(n_far, i + 1, near_body, 0)

    lp = lam_ref[...]
    lam = (jnp.exp(jnp.sum(lp[0:1] * lp[1:2], axis=-1, keepdims=True))
           - jnp.exp(jnp.sum(lp[2:3] * lp[3:4], axis=-1, keepdims=True)) + lam_init)
    o = acc_ref[...] / l_ref[...]
    outs = []
    for r in range(2):
        d = o[:, (2 * r) * tq:(2 * r + 1) * tq] - lam * o[:, (2 * r + 1) * tq:(2 * r + 2) * tq]
        ms = jnp.mean(d * d, axis=0, keepdims=True)
        y = d * lax.rsqrt(ms + EPS) * gs_ref[...] * (1.0 - lam_init)
        outs.append(y.T)
    o_ref[0] = jnp.concatenate(outs, axis=-1).astype(o_ref.dtype)


def _attn_self(qt, knb, vt, bias, lam_p, g_subln, lam_init, *, tq, ybuf=None, slab=0):
    b, l, _ = knb.shape
    n_cls = bias.shape[0]
    nk = l // tq
    hw = 2 * ATT_VD
    assert ATT_H * ATT_VD == BRANCH_W
    args = [qt, knb, vt, bias, lam_p, g_subln.reshape(ATT_VD, 1)]
    specs = [
        pl.BlockSpec((1, 1, LANES, 4 * tq), lambda bi, kv, i: (bi, kv, 0, i)),
        pl.BlockSpec((1, l, LANES), lambda bi, kv, i: (bi, 0, kv)),
        pl.BlockSpec((1, 1, nk, ATT_VD, tq), lambda bi, kv, i: (bi, kv, 0, 0, 0)),
        pl.BlockSpec((n_cls, 2, tq, tq), lambda bi, kv, i: (0, kv, 0, 0)),
        pl.BlockSpec((4, ATT_D), lambda bi, kv, i: (0, 0)),
        pl.BlockSpec((ATT_VD, 1), lambda bi, kv, i: (0, 0)),
    ]
    yshape, kern, aliases = _branch_out(
        ybuf, slab, b, l, args, specs,
        functools.partial(_attn_self_kernel, tq=tq, tk=tq, lam_init=lam_init, n_cls=n_cls, group=4))
    return pl.pallas_call(
        kern,
        out_shape=yshape,
        grid=(b, ATT_KV, l // tq),
        in_specs=specs,
        out_specs=pl.BlockSpec((None, 1, tq, hw), lambda bi, kv, i: (slab, bi, i, kv)),
        scratch_shapes=[pltpu.VMEM((1, 4 * tq), F32), pltpu.VMEM((1, 4 * tq), F32),
                        pltpu.VMEM((ATT_VD, 4 * tq), F32)],
        input_output_aliases=aliases,
        compiler_params=_cparams(("parallel", "parallel", "arbitrary")),
        name="diff_attn_self",
    )(*args)


def _attn_cached_kernel(cls_ref, q_ref, ck_ref, cv_ref, kn_ref, vn_ref, bp_ref, bn_ref, lam_ref, gs_ref,
                        o_ref, m_ref, l_ref, acc_ref, *, tq, nt, lam_init):
    del cls_ref
    j = pl.program_id(1)

    @pl.when(j == 0)
    def _():
        m_ref[...] = jnp.full(m_ref.shape, NEG, F32)
        l_ref[...] = jnp.zeros(l_ref.shape, F32)
        acc_ref[...] = jnp.zeros(acc_ref.shape, F32)

    dn_t = (((1,), (1,)), ((), ()))

    def update(k_all, v_all, bias_at):
        kvs = range(ATT_KV)
        ss = [lax.dot_general(_q4(q_ref[0, :, kv * 2 * LANES:(kv + 1) * 2 * LANES]),
                              k_all[:, kv * LANES:(kv + 1) * LANES].astype(BF16), dn_t,
                              preferred_element_type=F32) for kv in kvs]
        pa = [_osm_softmax(ss[kv], bias_at(2 * kv), bias_at(2 * kv + 1), m_ref.at[kv], l_ref.at[kv])
              for kv in kvs]
        for kv in kvs:
            p, alpha = pa[kv]
            acc_ref[kv] = alpha * acc_ref[kv] + jnp.dot(
                p, v_all[:, kv * LANES:(kv + 1) * LANES].astype(BF16), preferred_element_type=F32)

    @pl.when(j < nt)
    def _():
        update(ck_ref[...], cv_ref[...], lambda h: bp_ref[0, h])

    @pl.when(j == nt)
    def _():
        update(kn_ref[0], vn_ref[0], lambda h: bn_ref[0, h])
        outs = [_osm_finish(l_ref.at[kv], acc_ref.at[kv], lam_ref, gs_ref, lam_init, tq)
                for kv in range(ATT_KV)]
        o_ref[0] = jnp.concatenate(outs, axis=-1).astype(o_ref.dtype)


def _attn_cached(qn, knb, vb, cache_k, cache_v, layer, bias_past, cls_tbl, bias_new, lam_p, g_subln,
                 lam_init, *, tk, ybuf=None, slab=0):
    b, l, _ = qn.shape
    past = cache_k.shape[2]
    nt = past // tk
    kw = ATT_KV * 2 * ATT_D
    ck = cache_k.reshape(cache_k.shape[0], b, past, kw)
    cv = cache_v.reshape(cache_v.shape[0], b, past, kw)
    args = [qn, ck, cv, knb, vb, bias_past, bias_new, lam_p, g_subln.reshape(1, ATT_VD)]
    specs = [
        pl.BlockSpec((1, l, ATT_H * 2 * ATT_D), lambda bi, j, cls: (bi, 0, 0)),
        pl.BlockSpec((None, None, tk, kw), lambda bi, j, cls: (layer, bi, jnp.minimum(j, nt - 1), 0)),
        pl.BlockSpec((None, None, tk, kw), lambda bi, j, cls: (layer, bi, jnp.minimum(j, nt - 1), 0)),
        pl.BlockSpec((1, l, kw), lambda bi, j, cls: (bi, 0, 0)),
        pl.BlockSpec((1, l, kw), lambda bi, j, cls: (bi, 0, 0)),
        pl.BlockSpec((1, ATT_H, l, tk), lambda bi, j, cls: (cls[jnp.minimum(j, nt - 1)], 0, 0, 0)),
        pl.BlockSpec((1, ATT_H, l, l), lambda bi, j, cls: (0, 0, 0, 0)),
        pl.BlockSpec((4, ATT_D), lambda bi, j, cls: (0, 0)),
        pl.BlockSpec((1, ATT_VD), lambda bi, j, cls: (0, 0)),
    ]
    yshape, kern, aliases = _branch_out(
        ybuf, slab, b, l, args, specs,
        functools.partial(_attn_cached_kernel, tq=l, nt=nt, lam_init=lam_init), n_prefetch=1)
    grid_spec = pltpu.PrefetchScalarGridSpec(
        num_scalar_prefetch=1,
        grid=(b, nt + 1),
        in_specs=specs,
        out_specs=pl.BlockSpec((None, 1, l, ATT_H * ATT_VD), lambda bi, j, cls: (slab, bi, 0, 0)),
        scratch_shapes=[pltpu.VMEM((ATT_KV, 4 * l, LANES), F32), pltpu.VMEM((ATT_KV, 4 * l, LANES), F32),
                        pltpu.VMEM((ATT_KV, 4 * l, ATT_VD), F32)],
    )
    return pl.pallas_call(
        kern,
        out_shape=yshape,
        grid_spec=grid_spec,
        input_output_aliases=aliases,
        compiler_params=_cparams(("parallel", "arbitrary")),
        name="diff_attn_cached",
    )(cls_tbl, *args)


def _xattn_kernel(q_ref, mk_ref, mv_ref, o_ref):
    dn_t = (((1,), (1,)), ((), ()))
    outs = []
    for h in range(X_H):
        sl = slice(h * X_D, (h + 1) * X_D)
        s = lax.dot_general(q_ref[0, :, sl], mk_ref[0, :, sl].astype(BF16), dn_t,
                            preferred_element_type=F32) * (X_D ** -0.5)
        s = s - jnp.max(s, axis=-1, keepdims=True)
        p = jnp.exp(s)
        p = p / jnp.sum(p, axis=-1, keepdims=True)
        outs.append(jnp.dot(p.astype(BF16), mv_ref[0, :, sl].astype(BF16), preferred_element_type=F32))
    o_ref[0] = jnp.concatenate(outs, axis=-1).astype(o_ref.dtype)


def _xattn(q, mk, mv, *, tq):
    b, l, w = q.shape
    mlen = mk.shape[1]
    tq = min(tq, l)
    assert l % tq == 0
    return pl.pallas_call(
        _xattn_kernel,
        out_shape=jax.ShapeDtypeStruct((b, l, w), BF16),
        grid=(b, l // tq),
        in_specs=[
            pl.BlockSpec((1, tq, w), lambda i, c: (i, c, 0)),
            pl.BlockSpec((1, mlen, w), lambda i, c: (i, 0, 0)),
            pl.BlockSpec((1, mlen, w), lambda i, c: (i, 0, 0)),
        ],
        out_specs=pl.BlockSpec((1, tq, w), lambda i, c: (i, c, 0)),
        compiler_params=_cparams(("parallel", "parallel")),
        name="mem_cross_attn",
    )(q, mk, mv)


def _prep_weights(p, depth):
    layers = []
    off_z, off_xbc = 0, SSM_D
    off_dt = off_xbc + XBC_W
    off_pool = off_dt + SSM_H
    off_q = off_pool + BRANCH_W
    off_k = off_q + ATT_H * 2 * ATT_D
    off_v = off_k + ATT_KV * 2 * ATT_D
    off_gu = off_v + ATT_KV * ATT_VD
    off_gv = off_gu + BRANCH_W
    in_w = off_gv + BRANCH_W
    for l in range(depth):
        w = p['w_in'][l]
        d = w.shape[0]
        w_in = jnp.concatenate([
            w[:, off_xbc:off_dt], w[:, off_z:off_xbc], w[:, off_pool:off_q], w[:, off_q:off_k],
            w[:, off_gu:off_gv], w[:, off_gv:in_w], w[:, off_k:off_v], w[:, off_v:off_gu],
            w[:, off_dt:off_pool], jnp.zeros((d, LANES - SSM_H), w.dtype)], axis=1).astype(BF16)
        pad = lambda v: jnp.pad(v, (0, LANES - SSM_H)).reshape(1, LANES)
        layers.append(dict(
            g_ffn1=p['g_ffn1'][l], ffn1_in=p['w_ffn1_in'][l].astype(BF16), ffn1_out=p['w_ffn1_out'][l].astype(BF16),
            g_mix=p['g_mix'][l], w_in=w_in,
            conv_w=p['conv_w'][l], conv_b=p['conv_b'][l].reshape(1, XBC_W),
            dt_bias=pad(p['dt_bias'][l]), a_log=pad(p['a_log'][l]),
            d_skip=jnp.repeat(p['d_skip'][l], SSM_P).reshape(1, SSM_D), g_ssd=p['g_ssd'][l].reshape(1, SSM_D),
            w_pool=p['w_pool'][l].astype(BF16), pool_scale=p['pool_scale'][l].reshape(1, BRANCH_W),
            g_q=p['g_q'][l], g_k=p['g_k'][l], lam=p['lam'][l], g_subln=p['g_subln'][l],
            g_gv=p['g_gv'][l].reshape(1, BRANCH_W), w_sp=p['w_sp'][l], b_sp=p['b_sp'][l],
            w_gate=p['w_gate'][l].astype(BF16), b_gate=p['b_gate'][l], w_branch=p['w_branch'][l].astype(BF16),
            w_out=p['w_out'][l].astype(BF16),
            g_x=p['g_x'][l], w_xq=p['w_xq'][l].astype(BF16), g_xq=p['g_xq'][l], w_xo=p['w_xo'][l].astype(BF16),
            g_ffn2=p['g_ffn2'][l], ffn2_in=p['w_ffn2_in'][l].astype(BF16), ffn2_out=p['w_ffn2_out'][l].astype(BF16),
            g_post=p['g_post'][l],
            g_mem=p['g_mem'][l], w_mk=p['w_mk'][l].astype(BF16), w_mv=p['w_mv'][l].astype(BF16), g_xk=p['g_xk'][l],
        ))
    return layers


def _run_trunk(x, cache_k, cache_v, mem_k, mem_v, ssm0, conv0, pool0, layers, rel_bias):
    b, l, d = x.shape
    m = b * l
    depth = len(layers)
    cached = cache_k is not None
    assert l >= POOL_BUF and l >= CONV_W - 1
    if cached:
        past = cache_k.shape[2]
        tk = min(512, past)
        assert past % tk == 0 and past % CHUNK == 0 and l <= CHUNK and (past % GM_CHUNK == 0)
        nt = past // tk
        offs, cls = [FAR_OFFSET], []
        for j in range(nt):
            off = j * tk - past
            if _is_far(off, l, tk):
                cls.append(0)
            else:
                offs.append(off)
                cls.append(len(offs) - 1)
        bias_past = _bias_tiles(rel_bias, offs, l, tk)
        cls_tbl = jnp.asarray(np.asarray(cls, np.int32))
        bias_new = _bias_tiles(rel_bias, [0], l, l)
    else:
        past = 0
        tq = min(512, l)
        assert l % tq == 0 and tq % CHUNK == 0
        offs = [0]
        if l > tq:
            offs.append(-tq)
        if l > 2 * tq:
            assert _is_far(-2 * tq, tq, tq)
            offs.append(FAR_OFFSET)
        bias_self = _bias_tiles(rel_bias, offs, tq, tq, keys_major=True)

    h = x.reshape(m, d)
    new_k, new_v, new_ssm, new_conv, new_pool, new_gv = [], [], [], [], [], []
    for li, w in enumerate(layers):
        lam_init = 0.8 - 0.6 * math.exp(-0.3 * li)
        h = _ffn(h, w['g_ffn1'], w['ffn1_in'], w['ffn1_out'])
        proj = _mm(h, w['w_in'], g=w['g_mix'], out_dtype=F32, tm=1024, tn=1664, name="in_proj")
        proj = proj.reshape(b, l, P_W)

        ys, s_ssm = _ssd(proj, conv0[li] if cached else None, ssm0[li] if cached else None,
                         w['conv_w'], w['conv_b'], w['dt_bias'], w['a_log'], w['d_skip'], w['g_ssd'], q=128)
        ys = _pool(proj, pool0[li] if cached else None, past, w['w_pool'], w['pool_scale'], t=256,
                   ybuf=ys, slab=1)
        if cached:
            qn, kn, knb, vb = _qknorm(proj, w['g_q'], w['g_k'], t=512)
            ys = _attn_cached(qn, knb, vb, cache_k, cache_v, li, bias_past, cls_tbl, bias_new,
                              w['lam'], w['g_subln'], lam_init, tk=tk, ybuf=ys, slab=2)
        else:
            qt, kn, knb, vt = _qknorm_t(proj, w['g_q'], w['g_k'], t=tq)
            ys = _attn_self(qt, knb, vt, bias_self, w['lam'], w['g_subln'], lam_init, tq=tq, ybuf=ys, slab=2)
        ys, v_gm = _gmlp(proj, w['g_gv'], w['w_sp'], w['b_sp'], keep_vn=cached, ybuf=ys, slab=3)

        merged = _merge(h, w['g_mix'], ys.reshape(N_BRANCH, m, BRANCH_W), w['w_gate'], w['b_gate'], w['w_branch'])
        h = _mm(merged, w['w_out'], res=h, out_dtype=F32, tm=512, tn=1024, name="out_proj", w_resident=True)

        qx = _mm(h, w['w_xq'], g=w['g_x'], gh=w['g_xq'], out_dtype=BF16, tm=512, tn=512, name="xattn_q")
        o = _xattn(qx.reshape(b, l, X_H * X_D), mem_k[li].reshape(b, -1, X_H * X_D),
                   mem_v[li].reshape(b, -1, X_H * X_D), tq=512)
        h = _ffn(h, w['g_ffn2'], w['ffn2_in'], w['ffn2_out'], post_g=w['g_post'],
                 pre=(o.reshape(m, X_H * X_D), w['w_xo']))

        new_k.append(kn.reshape(b, l, ATT_KV, 2, ATT_D))
        new_v.append(proj[:, :, P_V:P_V + ATT_KV * ATT_VD].reshape(b, l, ATT_KV, ATT_VD))
        new_ssm.append(s_ssm)
        new_conv.append(proj[:, l - (CONV_W - 1):, P_XBC:P_XBC + XBC_W])
        new_pool.append(proj[:, l - POOL_BUF:, P_POOL:P_POOL + BRANCH_W])
        new_gv.append(v_gm)
    return (h.reshape(b, l, d), jnp.stack(new_k), jnp.stack(new_v), jnp.stack(new_ssm),
            jnp.stack(new_conv), jnp.stack(new_pool), jnp.stack(new_gv) if cached else None)


def kernel(x_prompt, x_sample, cache_attn_k, cache_attn_v, cache_mem_k, cache_mem_v, state_ssm, state_conv, state_pool, mem_prompt, g_ffn1, w_ffn1_in, w_ffn1_out, g_mix, w_in, conv_w, conv_b, dt_bias, a_log, d_skip, g_ssd, w_pool, pool_scale, g_q, g_k, lam, g_subln, rel_bias, g_gv, w_sp, b_sp, w_gate, b_gate, w_branch, w_out, g_x, w_xq, g_xq, g_mem, w_mk, w_mv, g_xk, w_xo, g_ffn2, w_ffn2_in, w_ffn2_out, g_post):
    p = dict(g_ffn1=g_ffn1, w_ffn1_in=w_ffn1_in, w_ffn1_out=w_ffn1_out, g_mix=g_mix, w_in=w_in,
             conv_w=conv_w, conv_b=conv_b, dt_bias=dt_bias, a_log=a_log, d_skip=d_skip, g_ssd=g_ssd,
             w_pool=w_pool, pool_scale=pool_scale, g_q=g_q, g_k=g_k, lam=lam, g_subln=g_subln,
             g_gv=g_gv, w_sp=w_sp, b_sp=b_sp, w_gate=w_gate, b_gate=b_gate, w_branch=w_branch,
             w_out=w_out, g_x=g_x, w_xq=w_xq, g_xq=g_xq, g_mem=g_mem, w_mk=w_mk, w_mv=w_mv, g_xk=g_xk,
             w_xo=w_xo, g_ffn2=g_ffn2, w_ffn2_in=w_ffn2_in, w_ffn2_out=w_ffn2_out, g_post=g_post)
    depth = w_in.shape[0]
    layers = _prep_weights(p, depth)

    bp, mlen, d = mem_prompt.shape
    mem2 = mem_prompt.reshape(bp * mlen, d)
    mks, mvs = [], []
    for w in layers:
        mk = _mm(mem2, w['w_mk'], g=w['g_mem'], gh=w['g_xk'], out_dtype=F32, tm=512, tn=512, name="mem_k")
        mv = _mm(mem2, w['w_mv'], g=w['g_mem'], out_dtype=F32, tm=512, tn=512, name="mem_v")
        mks.append(mk.reshape(bp, mlen, X_H, X_D))
        mvs.append(mv.reshape(bp, mlen, X_H, X_D))
    p_mem_k = jnp.stack(mks)
    p_mem_v = jnp.stack(mvs)

    y_prompt, p_attn_k, p_attn_v, p_ssm, p_conv, p_pool, _ = _run_trunk(
        x_prompt, None, None, p_mem_k, p_mem_v, None, None, None, layers, rel_bias)
    y_sample, s_attn_k, s_attn_v, s_ssm, s_conv, s_pool, s_gmlp_v = _run_trunk(
        x_sample, cache_attn_k, cache_attn_v, cache_mem_k, cache_mem_v, state_ssm, state_conv,
        state_pool, layers, rel_bias)

    return (y_prompt, y_sample, p_attn_k, p_attn_v, p_mem_k, p_mem_v, p_ssm, p_conv, p_pool,
            s_attn_k, s_attn_v, s_ssm, s_conv, s_pool, s_gmlp_v)
```

```python
import functools
import math

import jax
import jax.numpy as jnp
import numpy as np
from jax import lax
from jax.experimental import pallas as pl
from jax.experimental.pallas import tpu as pltpu

F32 = jnp.float32
BF16 = jnp.bfloat16
EPS = 1e-6
NEG = -1e30

VMEM_LIMIT_BYTES = 56 * 1024 * 1024
LANES = 128

BRANCH_W = 1024
SSM_P = 64
SSM_H = 16
SSM_G = 4
SSM_N = 128
SSM_D = SSM_H * SSM_P
CONV_W = 4
XBC_W = SSM_D + 2 * SSM_G * SSM_N
POOL_WINDOWS = (2, 4, 8, 16)
POOL_GW = BRANCH_W // len(POOL_WINDOWS)
POOL_BUF = max(POOL_WINDOWS) - 1
ATT_H = 8
ATT_KV = 4
ATT_D = 64
ATT_VD = 128
CHUNK = 64
N_BUCKETS = 32
MAX_DIST = 128
GM_CHUNK = 128
GM_G = 4
GM_GW = BRANCH_W // GM_G
X_H = 4
X_D = 128

P_XBC = 0
P_Z = 2048
P_POOL = 3072
P_Q = 4096
P_GU = 5120
P_GV = 6144
P_K = 7168
P_V = 7680
P_DT = 8192
P_W = 8320


def _cparams(sem):
    return pltpu.CompilerParams(dimension_semantics=sem, vmem_limit_bytes=VMEM_LIMIT_BYTES)


N_BRANCH = 4


def _skip_ref(fn, pos):
    def wrapped(*refs):
        return fn(*refs[:pos], *refs[pos + 1:])
    return wrapped


def _branch_out(ybuf, slab, b, l, args, specs, kernel_fn, n_prefetch=0):
    shape = jax.ShapeDtypeStruct((N_BRANCH, b, l, BRANCH_W), BF16)
    if ybuf is None:
        return shape, kernel_fn, {}
    assert slab > 0 and ybuf.shape == shape.shape and ybuf.dtype == shape.dtype
    pos = n_prefetch + len(args)
    args.append(ybuf)
    specs.append(pl.BlockSpec(memory_space=pl.ANY))
    return shape, _skip_ref(kernel_fn, pos), {pos: 0}


def _rms(xf, g):
    ms = jnp.mean(xf * xf, axis=-1, keepdims=True)
    return xf * lax.rsqrt(ms + EPS) * g


def _silu(x):
    return x * jax.nn.sigmoid(x)


def _mm_kernel(*refs, norm, head_norm, residual):
    it = iter(refs)
    x_ref = next(it)
    g_ref = next(it) if norm else None
    w_ref = next(it)
    gh_ref = next(it) if head_norm else None
    res_ref = next(it) if residual else None
    o_ref = next(it)
    xn_ref = next(it) if norm else None

    if norm:
        @pl.when(pl.program_id(1) == 0)
        def _():
            xn_ref[...] = _rms(x_ref[...].astype(F32), g_ref[...]).astype(BF16)
        xb = xn_ref[...]
    else:
        xb = x_ref[...].astype(BF16)
    acc = jnp.dot(xb, w_ref[...], preferred_element_type=F32)
    if head_norm:
        parts = []
        for c in range(acc.shape[1] // LANES):
            parts.append(_rms(acc[:, c * LANES:(c + 1) * LANES], gh_ref[...]))
        acc = jnp.concatenate(parts, axis=-1)
    if residual:
        acc = res_ref[...] + acc
    o_ref[...] = acc.astype(o_ref.dtype)


def _mm(x, w, *, g=None, gh=None, res=None, out_dtype, tm, tn, name, w_resident=False):
    m, k = x.shape
    n = w.shape[1]
    tm = min(tm, m)
    tn = min(tn, n)
    assert m % tm == 0 and n % tn == 0
    norm, head_norm, residual = g is not None, gh is not None, res is not None
    assert not (norm and w_resident)
    if w_resident:
        grid = (n // tn, m // tm)
        ij = lambda a, b: (b, a)
    else:
        grid = (m // tm, n // tn)
        ij = lambda a, b: (a, b)
    args, specs = [x], [pl.BlockSpec((tm, k), lambda a, b: (ij(a, b)[0], 0))]
    if norm:
        args.append(g.reshape(1, k))
        specs.append(pl.BlockSpec((1, k), lambda a, b: (0, 0)))
    args.append(w)
    specs.append(pl.BlockSpec((k, tn), lambda a, b: (0, ij(a, b)[1])))
    if head_norm:
        args.append(gh.reshape(1, LANES))
        specs.append(pl.BlockSpec((1, LANES), lambda a, b: (0, 0)))
    if residual:
        args.append(res)
        specs.append(pl.BlockSpec((tm, tn), lambda a, b: ij(a, b)))
    scratch = [pltpu.VMEM((tm, k), BF16)] if norm else []
    return pl.pallas_call(
        functools.partial(_mm_kernel, norm=norm, head_norm=head_norm, residual=residual),
        out_shape=jax.ShapeDtypeStruct((m, n), out_dtype),
        grid=grid,
        in_specs=specs,
        out_specs=pl.BlockSpec((tm, tn), lambda a, b: ij(a, b)),
        scratch_shapes=scratch,
        compiler_params=_cparams(("parallel", "arbitrary")),
        name=name,
    )(*args)


def _ffn_kernel(*refs, post, pre):
    it = iter(refs)
    x_ref, g_ref, wg_ref, wu_ref, wo_ref = (next(it) for _ in range(5))
    gp_ref = next(it) if post else None
    a_ref, wa_ref = (next(it), next(it)) if pre else (None, None)
    o_ref, xn_ref, acc_ref = next(it), next(it), next(it)
    f = pl.program_id(1)

    @pl.when(f == 0)
    def _():
        x = x_ref[...]
        if pre:
            x = x + jnp.dot(a_ref[...], wa_ref[...], preferred_element_type=F32)
            o_ref[...] = x
        xn_ref[...] = _rms(x, g_ref[...]).astype(BF16)
        acc_ref[...] = jnp.zeros_like(acc_ref)

    xb = xn_ref[...]
    gate = jnp.dot(xb, wg_ref[...], preferred_element_type=F32)
    up = jnp.dot(xb, wu_ref[...], preferred_element_type=F32)
    mid = (_silu(gate) * up).astype(BF16)
    acc_ref[...] += jnp.dot(mid, wo_ref[...], preferred_element_type=F32)

    @pl.when(f == pl.num_programs(1) - 1)
    def _():
        h = (o_ref[...] if pre else x_ref[...]) + 0.5 * acc_ref[...]
        if post:
            h = _rms(h, gp_ref[...])
        o_ref[...] = h


def _ffn(x, g, w_in, w_out, *, post_g=None, pre=None, tm=512, tf=512):
    m, d = x.shape
    ff = w_out.shape[0]
    tm = min(tm, m)
    assert m % tm == 0 and ff % tf == 0
    nf = ff // tf
    post = post_g is not None
    args = [x, g.reshape(1, d), w_in, w_in, w_out]
    specs = [
        pl.BlockSpec((tm, d), lambda i, f: (i, 0)),
        pl.BlockSpec((1, d), lambda i, f: (0, 0)),
        pl.BlockSpec((d, tf), lambda i, f: (0, f)),
        pl.BlockSpec((d, tf), lambda i, f: (0, nf + f)),
        pl.BlockSpec((tf, d), lambda i, f: (f, 0)),
    ]
    if post:
        args.append(post_g.reshape(1, d))
        specs.append(pl.BlockSpec((1, d), lambda i, f: (0, 0)))
    if pre is not None:
        a, wa = pre
        ka = a.shape[1]
        args += [a, wa]
        specs += [pl.BlockSpec((tm, ka), lambda i, f: (i, 0)), pl.BlockSpec((ka, d), lambda i, f: (0, 0))]
    return pl.pallas_call(
        functools.partial(_ffn_kernel, post=post, pre=pre is not None),
        out_shape=jax.ShapeDtypeStruct((m, d), F32),
        grid=(m // tm, nf),
        in_specs=specs,
        out_specs=pl.BlockSpec((tm, d), lambda i, f: (i, 0)),
        scratch_shapes=[pltpu.VMEM((tm, d), BF16), pltpu.VMEM((tm, d), F32)],
        compiler_params=_cparams(("parallel", "arbitrary")),
        name="swiglu_ffn",
    )(*args)


def _merge_kernel(h_ref, g_ref, y_ref, wg_ref, bg_ref, wb_ref, o_ref, hn_ref, acc_ref):
    j = pl.program_id(1)
    br = pl.program_id(2)

    @pl.when((j == 0) & (br == 0))
    def _():
        hn_ref[...] = _rms(h_ref[...], g_ref[...]).astype(BF16)

    @pl.when(br == 0)
    def _():
        acc_ref[...] = jnp.zeros_like(acc_ref)

    gate = jax.nn.sigmoid(jnp.dot(hn_ref[...], wg_ref[...], preferred_element_type=F32) + bg_ref[...])
    acc_ref[...] += gate * jnp.dot(y_ref[...], wb_ref[...], preferred_element_type=F32)

    @pl.when(br == pl.num_programs(2) - 1)
    def _():
        o_ref[...] = acc_ref[...].astype(o_ref.dtype)


def _merge(h, g, ys, w_gate, b_gate, w_branch, *, tm=1024, tn=1024):
    m, d = h.shape
    nb, bw, _ = w_branch.shape
    tm = min(tm, m)
    assert m % tm == 0 and d % tn == 0 and ys.shape == (nb, m, bw)
    return pl.pallas_call(
        _merge_kernel,
        out_shape=jax.ShapeDtypeStruct((m, d), BF16),
        grid=(m // tm, d // tn, nb),
        in_specs=[
            pl.BlockSpec((tm, d), lambda i, j, b: (i, 0)),
            pl.BlockSpec((1, d), lambda i, j, b: (0, 0)),
            pl.BlockSpec((None, tm, bw), lambda i, j, b: (b, i, 0)),
            pl.BlockSpec((None, d, tn), lambda i, j, b: (b, 0, j)),
            pl.BlockSpec((None, 1, tn), lambda i, j, b: (b, 0, j)),
            pl.BlockSpec((None, bw, tn), lambda i, j, b: (b, 0, j)),
        ],
        out_specs=pl.BlockSpec((tm, tn), lambda i, j, b: (i, j)),
        scratch_shapes=[pltpu.VMEM((tm, d), BF16), pltpu.VMEM((tm, tn), F32)],
        compiler_params=_cparams(("parallel", "arbitrary", "arbitrary")),
        name="branch_merge",
    )(h, g.reshape(1, d), ys, w_gate, b_gate.reshape(nb, 1, d), w_branch)


def _ssd_kernel(*refs, q, has_state):
    it = iter(refs)
    xbc_ref, z_ref, dt_ref = next(it), next(it), next(it)
    conv0_ref = next(it) if has_state else None
    s0_ref = next(it) if has_state else None
    cw_ref, cb_ref, dtb_ref, alog_ref, dskip_ref, gssd_ref = (next(it) for _ in range(6))
    y_ref, s_ref, cbuf = next(it), next(it), next(it)
    hist = 8

    @pl.when(pl.program_id(1) == 0)
    def _():
        cbuf[0:hist, :] = jnp.zeros((hist, XBC_W), F32)
        if has_state:
            cbuf[hist - (CONV_W - 1):hist, :] = conv0_ref[0]
            s_ref[0] = s0_ref[0]
        else:
            s_ref[0] = jnp.zeros(s_ref.shape[1:], F32)

    cbuf[hist:hist + q, :] = xbc_ref[0]
    conv = cb_ref[...]
    for k in range(CONV_W):
        lo = hist - (CONV_W - 1) + k
        conv = conv + cbuf[lo:lo + q, :] * cw_ref[k:k + 1, :]
    cbuf[0:hist, :] = cbuf[q:q + hist, :]
    xc = _silu(conv)
    xs = xc[:, :SSM_D]
    bm = xc[:, SSM_D:SSM_D + SSM_G * SSM_N]
    cm = xc[:, SSM_D + SSM_G * SSM_N:]

    dtl = dt_ref[0] + dtb_ref[...]
    dt = jnp.maximum(dtl, 0.0) + jnp.log1p(jnp.exp(-jnp.abs(dtl)))
    a = dt * (-jnp.exp(alog_ref[...]))
    ri = lax.broadcasted_iota(jnp.int32, (q, q), 0)
    ci = lax.broadcasted_iota(jnp.int32, (q, q), 1)
    tril = ri >= ci
    cum = jnp.dot(tril.astype(F32), a, preferred_element_type=F32,
                  precision=lax.Precision.HIGHEST)
    cum_t = cum.T
    dt_t = dt.T
    cum_last = cum[q - 1:q, :]
    w_state = jnp.exp(cum_last - cum) * dt
    ecum = jnp.exp(cum)
    ecl = jnp.exp(cum_last)

    lane_lo = lax.broadcasted_iota(jnp.int32, (q, LANES), 1) < SSM_P
    row_lo_t = lax.broadcasted_iota(jnp.int32, (LANES, q), 0) < SSM_P
    row_lo_s = lax.broadcasted_iota(jnp.int32, (LANES, SSM_N), 0) < SSM_P
    dn_t = (((1,), (1,)), ((), ()))

    y_parts = []
    cb = None
    heads_per_group = SSM_H // SSM_G
    for pp in range(SSM_H // 2):
        grp = (2 * pp) // heads_per_group
        xp = xs[:, pp * LANES:(pp + 1) * LANES]
        bg = bm[:, grp * SSM_N:(grp + 1) * SSM_N]
        cg = cm[:, grp * SSM_N:(grp + 1) * SSM_N]
        cg_b = cg.astype(BF16)
        if (2 * pp) % heads_per_group == 0:
            cb = lax.dot_general(cg_b, bg.astype(BF16), dn_t, preferred_element_type=F32)
        xp_t = xp.T
        s_pair = s_ref[0, pp * LANES:(pp + 1) * LANES, :]
        y_pair = None
        s_new = None
        for hh in range(2):
            hd = 2 * pp + hh
            seg = cum[:, hd:hd + 1] - cum_t[hd:hd + 1, :]
            mat = jnp.exp(jnp.where(tril, seg, NEG)) * cb * dt_t[hd:hd + 1, :]
            sel = lane_lo if hh == 0 else jnp.logical_not(lane_lo)
            x_h = jnp.where(sel, xp, 0.0).astype(BF16)
            yd = jnp.dot(mat.astype(BF16), x_h, preferred_element_type=F32)
            y_pair = yd if y_pair is None else y_pair + yd
            sel_t = row_lo_t if hh == 0 else jnp.logical_not(row_lo_t)
            xt_h = jnp.where(sel_t, xp_t, 0.0).astype(BF16)
            bs = (bg * w_state[:, hd:hd + 1]).astype(BF16)
            sn = jnp.dot(xt_h, bs, preferred_element_type=F32)
            s_new = sn if s_new is None else s_new + sn
        y_off = lax.dot_general(cg_b, s_pair.astype(BF16), dn_t, preferred_element_type=F32)
        e_pair = jnp.where(lane_lo, ecum[:, 2 * pp:2 * pp + 1], ecum[:, 2 * pp + 1:2 * pp + 2])
        y_parts.append(y_pair + y_off * e_pair)
        dec = jnp.where(row_lo_s, ecl[:, 2 * pp:2 * pp + 1], ecl[:, 2 * pp + 1:2 * pp + 2])
        s_ref[0, pp * LANES:(pp + 1) * LANES, :] = dec * s_pair + s_new

    y = jnp.concatenate(y_parts, axis=-1)
    y = y + dskip_ref[...] * xs
    y = y * _silu(z_ref[0])
    gw = SSM_D // SSM_G
    outs = [_rms(y[:, gi * gw:(gi + 1) * gw], gssd_ref[:, gi * gw:(gi + 1) * gw]) for gi in range(SSM_G)]
    y_ref[0] = jnp.concatenate(outs, axis=-1).astype(y_ref.dtype)


def _ssd(proj, conv0, s0, cw, cb, dtb, alog, dskip, gssd, *, q, ybuf=None, slab=0):
    b, l, _ = proj.shape
    q = min(q, l)
    assert l % q == 0 and q >= 8
    has_state = s0 is not None
    args = [proj, proj, proj]
    specs = [
        pl.BlockSpec((1, q, XBC_W), lambda i, c: (i, c, P_XBC // XBC_W)),
        pl.BlockSpec((1, q, SSM_D), lambda i, c: (i, c, P_Z // SSM_D)),
        pl.BlockSpec((1, q, LANES), lambda i, c: (i, c, P_DT // LANES)),
    ]
    if has_state:
        args += [conv0, s0.reshape(b, SSM_H * SSM_P, SSM_N)]
        specs += [
            pl.BlockSpec((1, CONV_W - 1, XBC_W), lambda i, c: (i, 0, 0)),
            pl.BlockSpec((1, SSM_H * SSM_P, SSM_N), lambda i, c: (i, 0, 0)),
        ]
    args += [cw, cb, dtb, alog, dskip, gssd]
    specs += [
        pl.BlockSpec((CONV_W, XBC_W), lambda i, c: (0, 0)),
        pl.BlockSpec((1, XBC_W), lambda i, c: (0, 0)),
        pl.BlockSpec((1, LANES), lambda i, c: (0, 0)),
        pl.BlockSpec((1, LANES), lambda i, c: (0, 0)),
        pl.BlockSpec((1, SSM_D), lambda i, c: (0, 0)),
        pl.BlockSpec((1, SSM_D), lambda i, c: (0, 0)),
    ]
    yshape, kern, aliases = _branch_out(ybuf, slab, b, l, args, specs,
                                        functools.partial(_ssd_kernel, q=q, has_state=has_state))
    y, s = pl.pallas_call(
        kern,
        out_shape=(yshape, jax.ShapeDtypeStruct((b, SSM_H * SSM_P, SSM_N), F32)),
        grid=(b, l // q),
        in_specs=specs,
        out_specs=(pl.BlockSpec((None, 1, q, SSM_D), lambda i, c: (slab, i, c, 0)),
                   pl.BlockSpec((1, SSM_H * SSM_P, SSM_N), lambda i, c: (i, 0, 0))),
        scratch_shapes=[pltpu.VMEM((q + 8, XBC_W), F32)],
        input_output_aliases=aliases,
        compiler_params=_cparams(("parallel", "arbitrary")),
        name="ssd_mixer",
    )(*args)
    return y, s.reshape(b, SSM_H, SSM_P, SSM_N)


def _pool_kernel(*refs, t, pos0, has_state):
    it = iter(refs)
    xp_ref = next(it)
    p0_ref = next(it) if has_state else None
    w_ref, sc_ref, y_ref, buf = next(it), next(it), next(it), next(it)
    hist = POOL_BUF + 1
    ti = pl.program_id(1)

    @pl.when(ti == 0)
    def _():
        buf[0:hist, :] = jnp.zeros((hist, BRANCH_W), F32)
        if has_state:
            buf[1:hist, :] = p0_ref[0]

    xp = xp_ref[0]
    buf[hist:hist + t, :] = xp
    pos = pos0 + ti * t + lax.broadcasted_iota(jnp.int32, (t, POOL_GW), 0)
    outs = []
    for gi, w in enumerate(POOL_WINDOWS):
        c0 = gi * POOL_GW
        s = xp[:, c0:c0 + POOL_GW]
        for k in range(1, w):
            s = s + buf[hist - k:hist - k + t, c0:c0 + POOL_GW]
        cnt = jnp.minimum(w, pos + 1).astype(F32)
        pooled = s / cnt - xp[:, c0:c0 + POOL_GW]
        y = jnp.dot(pooled.astype(BF16), w_ref[gi], preferred_element_type=F32)
        outs.append(y * sc_ref[:, c0:c0 + POOL_GW])
    y_ref[0] = jnp.concatenate(outs, axis=-1).astype(y_ref.dtype)
    buf[0:hist, :] = buf[t:t + hist, :]


def _pool(proj, pool0, pos0, w_pool, scale, *, t, ybuf=None, slab=0):
    b, l, _ = proj.shape
    t = min(t, l)
    assert l % t == 0 and t >= POOL_BUF + 1
    has_state = pool0 is not None
    args = [proj]
    specs = [pl.BlockSpec((1, t, BRANCH_W), lambda i, c: (i, c, P_POOL // BRANCH_W))]
    if has_state:
        args.append(pool0)
        specs.append(pl.BlockSpec((1, POOL_BUF, BRANCH_W), lambda i, c: (i, 0, 0)))
    args += [w_pool, scale]
    specs += [
        pl.BlockSpec((len(POOL_WINDOWS), POOL_GW, POOL_GW), lambda i, c: (0, 0, 0)),
        pl.BlockSpec((1, BRANCH_W), lambda i, c: (0, 0)),
    ]
    yshape, kern, aliases = _branch_out(
        ybuf, slab, b, l, args, specs, functools.partial(_pool_kernel, t=t, pos0=pos0, has_state=has_state))
    return pl.pallas_call(
        kern,
        out_shape=yshape,
        grid=(b, l // t),
        in_specs=specs,
        out_specs=pl.BlockSpec((None, 1, t, BRANCH_W), lambda i, c: (slab, i, c, 0)),
        scratch_shapes=[pltpu.VMEM((t + POOL_BUF + 1, BRANCH_W), F32)],
        input_output_aliases=aliases,
        compiler_params=_cparams(("parallel", "arbitrary")),
        name="pool_mixer",
    )(*args)


def _gmlp_kernel(u_ref, v_ref, g_ref, w_ref, b_ref, y_ref, *rest, cl, nsub, keep_vn):
    tril = lax.broadcasted_iota(jnp.int32, (cl, cl), 0) >= lax.broadcasted_iota(jnp.int32, (cl, cl), 1)
    ws = [jnp.where(tril, w_ref[gi], 0.0).astype(BF16) for gi in range(GM_G)]
    for sub in range(nsub):
        rows = slice(sub * cl, (sub + 1) * cl)
        u = jax.nn.gelu(u_ref[0, rows, :], approximate=True)
        vn = _rms(jax.nn.gelu(v_ref[0, rows, :], approximate=True), g_ref[...])
        if keep_vn:
            rest[0][0, rows, :] = vn
        outs = []
        for gi in range(GM_G):
            s = jnp.dot(ws[gi], vn[:, gi * GM_GW:(gi + 1) * GM_GW].astype(BF16), preferred_element_type=F32)
            s = s + b_ref[:, gi:gi + 1]
            outs.append(u[:, gi * GM_GW:(gi + 1) * GM_GW] * s)
        y_ref[0, rows, :] = jnp.concatenate(outs, axis=-1).astype(y_ref.dtype)


def _gmlp(proj, g_gv, w_sp, b_sp, *, keep_vn, ybuf=None, slab=0):
    b, l, _ = proj.shape
    cl = min(GM_CHUNK, l)
    assert l % cl == 0
    nsub = math.gcd(l // cl, 4)
    t = cl * nsub
    w = w_sp[:, :cl, :cl]
    bt = b_sp[:, :cl].T
    args = [proj, proj, g_gv, w, bt]
    specs = [
        pl.BlockSpec((1, t, BRANCH_W), lambda i, c: (i, c, P_GU // BRANCH_W)),
        pl.BlockSpec((1, t, BRANCH_W), lambda i, c: (i, c, P_GV // BRANCH_W)),
        pl.BlockSpec((1, BRANCH_W), lambda i, c: (0, 0)),
        pl.BlockSpec((GM_G, cl, cl), lambda i, c: (0, 0, 0)),
        pl.BlockSpec((cl, GM_G), lambda i, c: (0, 0)),
    ]
    yshape, kern, aliases = _branch_out(
        ybuf, slab, b, l, args, specs, functools.partial(_gmlp_kernel, cl=cl, nsub=nsub, keep_vn=keep_vn))
    out_shape = [yshape]
    out_specs = [pl.BlockSpec((None, 1, t, BRANCH_W), lambda i, c: (slab, i, c, 0))]
    if keep_vn:
        out_shape.append(jax.ShapeDtypeStruct((b, l, BRANCH_W), F32))
        out_specs.append(pl.BlockSpec((1, t, BRANCH_W), lambda i, c: (i, c, 0)))
    outs = pl.pallas_call(
        kern,
        out_shape=tuple(out_shape),
        grid=(b, l // t),
        in_specs=specs,
        out_specs=tuple(out_specs),
        input_output_aliases=aliases,
        compiler_params=_cparams(("parallel", "parallel")),
        name="gmlp_mixer",
    )(*args)
    return outs[0], (outs[1] if keep_vn else None)


def _norm_halves(x, g):
    lo = lax.broadcasted_iota(jnp.int32, x.shape, 1) < ATT_D
    xx = x * x
    s_lo = jnp.sum(jnp.where(lo, xx, 0.0), axis=-1, keepdims=True)
    s_hi = jnp.sum(jnp.where(lo, 0.0, xx), axis=-1, keepdims=True)
    ms = jnp.where(lo, s_lo, s_hi) * (1.0 / ATT_D)
    return x * lax.rsqrt(ms + EPS) * g


def _qknorm_kernel(q_ref, k_ref, v_ref, gq_ref, gk_ref, qn_ref, kn_ref, knb_ref, vb_ref):
    scale = ATT_D ** -0.5
    q = q_ref[0]
    qs = [_norm_halves(q[:, c * LANES:(c + 1) * LANES], gq_ref[...]) * scale
          for c in range(q.shape[1] // LANES)]
    qn_ref[0] = jnp.concatenate(qs, axis=-1).astype(qn_ref.dtype)
    k = k_ref[0]
    ks = [_norm_halves(k[:, c * LANES:(c + 1) * LANES], gk_ref[...]) for c in range(k.shape[1] // LANES)]
    kn = jnp.concatenate(ks, axis=-1)
    kn_ref[0] = kn
    knb_ref[0] = kn.astype(BF16)
    vb_ref[0] = v_ref[0].astype(BF16)


def _qknorm(proj, g_q, g_k, *, t):
    b, l, _ = proj.shape
    t = min(t, l)
    assert l % t == 0
    qw, kw = ATT_H * 2 * ATT_D, ATT_KV * 2 * ATT_D
    gq = jnp.tile(g_q, 2).reshape(1, LANES)
    gk = jnp.tile(g_k, 2).reshape(1, LANES)
    return pl.pallas_call(
        _qknorm_kernel,
        out_shape=(jax.ShapeDtypeStruct((b, l, qw), BF16), jax.ShapeDtypeStruct((b, l, kw), F32),
                   jax.ShapeDtypeStruct((b, l, kw), BF16), jax.ShapeDtypeStruct((b, l, kw), BF16)),
        grid=(b, l // t),
        in_specs=[
            pl.BlockSpec((1, t, qw), lambda i, c: (i, c, P_Q // qw)),
            pl.BlockSpec((1, t, kw), lambda i, c: (i, c, P_K // kw)),
            pl.BlockSpec((1, t, kw), lambda i, c: (i, c, P_V // kw)),
            pl.BlockSpec((1, LANES), lambda i, c: (0, 0)),
            pl.BlockSpec((1, LANES), lambda i, c: (0, 0)),
        ],
        out_specs=(pl.BlockSpec((1, t, qw), lambda i, c: (i, c, 0)),
                   pl.BlockSpec((1, t, kw), lambda i, c: (i, c, 0)),
                   pl.BlockSpec((1, t, kw), lambda i, c: (i, c, 0)),
                   pl.BlockSpec((1, t, kw), lambda i, c: (i, c, 0))),
        compiler_params=_cparams(("parallel", "parallel")),
        name="qk_norm",
    )(proj, proj, proj, gq, gk)


def _bias_kernel(off_ref, tbl_ref, o_ref, *, tq, tk, keys_major):
    c = pl.program_id(0)
    h = pl.program_id(1)
    off = off_ref[c]
    shape = (tk, tq) if keys_major else (tq, tk)
    row = lax.broadcasted_iota(jnp.int32, shape, 1 if keys_major else 0)
    col = lax.broadcasted_iota(jnp.int32, shape, 0 if keys_major else 1)
    rel = off + col - row
    nb = N_BUCKETS // 2
    max_exact = nb // 2
    n = jnp.abs(rel)
    large = max_exact + (jnp.log(jnp.maximum(n, 1).astype(F32) / max_exact)
                         / math.log(MAX_DIST / max_exact) * (nb - max_exact)).astype(jnp.int32)
    large = jnp.minimum(large, nb - 1)
    bucket = jnp.where(rel > 0, nb, 0) + jnp.where(n < max_exact, n, large)
    val = jnp.zeros(shape, F32)
    for bk in range(N_BUCKETS):
        val = jnp.where(bucket == bk, tbl_ref[bk * ATT_H + h], val)
    visible = jnp.right_shift(off + col, 6) <= jnp.right_shift(row, 6)
    o_ref[0, 0] = jnp.where(visible, val, NEG)


def _bias_tiles(rel_bias, offsets, tq, tk, keys_major=False):
    assert CHUNK == 64
    offs = jnp.asarray(np.asarray(offsets, np.int32))
    n_cls = len(offsets)
    shape = (tk, tq) if keys_major else (tq, tk)
    return pl.pallas_call(
        functools.partial(_bias_kernel, tq=tq, tk=tk, keys_major=keys_major),
        out_shape=jax.ShapeDtypeStruct((n_cls, ATT_H) + shape, F32),
        grid=(n_cls, ATT_H),
        in_specs=[pl.BlockSpec(memory_space=pltpu.SMEM), pl.BlockSpec(memory_space=pltpu.SMEM)],
        out_specs=pl.BlockSpec((1, 1) + shape, lambda c, h: (c, h, 0, 0)),
        compiler_params=_cparams(("parallel", "parallel")),
        name="rel_bias_tiles",
    )(offs, rel_bias.reshape(-1))


FAR_OFFSET = -(1 << 20)


def _is_far(off, tq, tk):
    nb = N_BUCKETS // 2
    max_rel = off + tk - 1
    if max_rel >= 0:
        return False
    n_min = -max_rel
    sat = (nb // 2) * (MAX_DIST / (nb // 2)) ** ((nb - 1 - nb // 2) / (nb - nb // 2))
    return n_min >= math.ceil(sat) + 1


def _q4(q2):
    lo = lax.broadcasted_iota(jnp.int32, (q2.shape[0], LANES), 1) < ATT_D
    parts = []
    for r in range(2):
        qr = q2[:, r * LANES:(r + 1) * LANES]
        parts.append(jnp.where(lo, qr, jnp.zeros_like(qr)))
        parts.append(jnp.where(lo, jnp.zeros_like(qr), qr))
    return jnp.concatenate(parts, axis=0)


def _osm_softmax(s, b0, b1, m_ref, l_ref):
    s = s + jnp.concatenate([b0, b0, b1, b1], axis=0)
    m_prev = m_ref[...]
    m_new = jnp.maximum(m_prev, jnp.max(s, axis=-1, keepdims=True))
    alpha = jnp.exp(m_prev - m_new)
    p = jnp.exp(s - m_new[:, 0:1])
    l_ref[...] = alpha * l_ref[...] + jnp.sum(p, axis=-1, keepdims=True)
    m_ref[...] = m_new
    return p.astype(BF16), alpha


def _osm_finish(l_ref, acc_ref, lam_ref, gs_ref, lam_init, tq):
    lp = lam_ref[...]
    lam = (jnp.exp(jnp.sum(lp[0:1] * lp[1:2], axis=-1, keepdims=True))
           - jnp.exp(jnp.sum(lp[2:3] * lp[3:4], axis=-1, keepdims=True)) + lam_init)
    o = acc_ref[...] / l_ref[...]
    outs = []
    for r in range(2):
        d = o[(2 * r) * tq:(2 * r + 1) * tq] - lam * o[(2 * r + 1) * tq:(2 * r + 2) * tq]
        outs.append(_rms(d, gs_ref[...]) * (1.0 - lam_init))
    return jnp.concatenate(outs, axis=-1)


def _qknorm_t_kernel(q_ref, k_ref, v_ref, gq_ref, gk_ref, qt_ref, kn_ref, knb_ref, vt_ref):
    scale = ATT_D ** -0.5
    q = q_ref[0]
    t = q.shape[0]
    lo = lax.broadcasted_iota(jnp.int32, (t, LANES), 1) < ATT_D
    for kv in range(ATT_KV):
        cols = []
        for r in range(2):
            c = kv * 2 + r
            qr = _norm_halves(q[:, c * LANES:(c + 1) * LANES], gq_ref[...]) * scale
            cols.append(jnp.where(lo, qr, 0.0).T)
            cols.append(jnp.where(lo, 0.0, qr).T)
        qt_ref[0, kv] = jnp.concatenate(cols, axis=1).astype(qt_ref.dtype)
    k = k_ref[0]
    ks = [_norm_halves(k[:, c * LANES:(c + 1) * LANES], gk_ref[...]) for c in range(k.shape[1] // LANES)]
    kn = jnp.concatenate(ks, axis=-1)
    kn_ref[0] = kn
    knb_ref[0] = kn.astype(BF16)
    v = v_ref[0]
    for kv in range(ATT_KV):
        vt_ref[0, kv, 0] = v[:, kv * LANES:(kv + 1) * LANES].T.astype(vt_ref.dtype)


def _qknorm_t(proj, g_q, g_k, *, t):
    b, l, _ = proj.shape
    assert l % t == 0
    qw, kw = ATT_H * 2 * ATT_D, ATT_KV * 2 * ATT_D
    gq = jnp.tile(g_q, 2).reshape(1, LANES)
    gk = jnp.tile(g_k, 2).reshape(1, LANES)
    return pl.pallas_call(
        _qknorm_t_kernel,
        out_shape=(jax.ShapeDtypeStruct((b, ATT_KV, LANES, 4 * l), BF16),
                   jax.ShapeDtypeStruct((b, l, kw), F32),
                   jax.ShapeDtypeStruct((b, l, kw), BF16),
                   jax.ShapeDtypeStruct((b, ATT_KV, l // t, ATT_VD, t), BF16)),
        grid=(b, l // t),
        in_specs=[
            pl.BlockSpec((1, t, qw), lambda i, c: (i, c, P_Q // qw)),
            pl.BlockSpec((1, t, kw), lambda i, c: (i, c, P_K // kw)),
            pl.BlockSpec((1, t, kw), lambda i, c: (i, c, P_V // kw)),
            pl.BlockSpec((1, LANES), lambda i, c: (0, 0)),
            pl.BlockSpec((1, LANES), lambda i, c: (0, 0)),
        ],
        out_specs=(pl.BlockSpec((1, ATT_KV, LANES, 4 * t), lambda i, c: (i, 0, 0, c)),
                   pl.BlockSpec((1, t, kw), lambda i, c: (i, c, 0)),
                   pl.BlockSpec((1, t, kw), lambda i, c: (i, c, 0)),
                   pl.BlockSpec((1, ATT_KV, 1, ATT_VD, t), lambda i, c: (i, 0, c, 0, 0))),
        compiler_params=_cparams(("parallel", "parallel")),
        name="qk_norm_t",
    )(proj, proj, proj, gq, gk)


def _attn_self_kernel(qt_ref, k_ref, vt_ref, bias_ref, lam_ref, gs_ref, o_ref, m_ref, l_ref, acc_ref,
                      *, tq, tk, lam_init, n_cls, group):
    i = pl.program_id(2)
    m_ref[...] = jnp.full(m_ref.shape, NEG, F32)
    l_ref[...] = jnp.zeros(l_ref.shape, F32)
    acc_ref[...] = jnp.zeros(acc_ref.shape, F32)

    def step(j, far):
        kt = k_ref[0, pl.ds(pl.multiple_of(j * tk, tk), tk), :]
        vt = vt_ref[0, 0, j]
        cls = jnp.minimum(i - j, n_cls - 1)
        for c0 in range(0, 4, group):
            update(kt, vt, cls, far, range(c0, c0 + group))

    def update(kt, vt, cls, far, combos):
        cols = {c: slice(c * tq, (c + 1) * tq) for c in combos}
        ss = {c: jnp.dot(kt, qt_ref[0, 0, :, cs], preferred_element_type=F32) for c, cs in cols.items()}
        ps, alphas = {}, {}
        for c, cs in cols.items():
            s = ss[c]
            m_prev = m_ref[:, cs]
            if far:
                crow = bias_ref[cls, c // 2, 0:1, :]
                m_new = jnp.maximum(m_prev, jnp.max(s, axis=0, keepdims=True) + crow)
                p = jnp.exp(s - (m_new - crow))
            else:
                s = s + bias_ref[cls, c // 2]
                m_new = jnp.maximum(m_prev, jnp.max(s, axis=0, keepdims=True))
                p = jnp.exp(s - m_new)
            alpha = jnp.exp(m_prev - m_new)
            l_ref[:, cs] = alpha * l_ref[:, cs] + jnp.sum(p, axis=0, keepdims=True)
            m_ref[:, cs] = m_new
            ps[c] = p.astype(BF16)
            alphas[c] = alpha
        for c, cs in cols.items():
            acc_ref[:, cs] = alphas[c] * acc_ref[:, cs] + jnp.dot(vt, ps[c], preferred_element_type=F32)

    n_far = jnp.maximum(i - (n_cls - 2), 0) if n_cls == 3 else 0

    def far_body(j, carry):
        step(j, True)
        return carry

    def near_body(j, carry):
        step(j, False)
        return carry

    if n_cls == 3:
        lax.fori_loop(0, n_far, far_body, 0)
    lax.fori_loop(n_far, i + 1, near_body, 0)

    lp = lam_ref[...]
    lam = (jnp.exp(jnp.sum(lp[0:1] * lp[1:2], axis=-1, keepdims=True))
           - jnp.exp(jnp.sum(lp[2:3] * lp[3:4], axis=-1, keepdims=True)) + lam_init)
    o = acc_ref[...] / l_ref[...]
    outs = []
    for r in range(2):
        d = o[:, (2 * r) * tq:(2 * r + 1) * tq] - lam * o[:, (2 * r + 1) * tq:(2 * r + 2) * tq]
        ms = jnp.mean(d * d, axis=0, keepdims=True)
        y = d * lax.rsqrt(ms + EPS) * gs_ref[...] * (1.0 - lam_init)
        outs.append(y.T)
    o_ref[0] = jnp.concatenate(outs, axis=-1).astype(o_ref.dtype)


def _attn_self(qt, knb, vt, bias, lam_p, g_subln, lam_init, *, tq, ybuf=None, slab=0):
    b, l, _ = knb.shape
    n_cls = bias.shape[0]
    nk = l // tq
    hw = 2 * ATT_VD
    assert ATT_H * ATT_VD == BRANCH_W
    args = [qt, knb, vt, bias, lam_p, g_subln.reshape(ATT_VD, 1)]
    specs = [
        pl.BlockSpec((1, 1, LANES, 4 * tq), lambda bi, kv, i: (bi, kv, 0, i)),
        pl.BlockSpec((1, l, LANES), lambda bi, kv, i: (bi, 0, kv)),
        pl.BlockSpec((1, 1, nk, ATT_VD, tq), lambda bi, kv, i: (bi, kv, 0, 0, 0)),
        pl.BlockSpec((n_cls, 2, tq, tq), lambda bi, kv, i: (0, kv, 0, 0)),
        pl.BlockSpec((4, ATT_D), lambda bi, kv, i: (0, 0)),
        pl.BlockSpec((ATT_VD, 1), lambda bi, kv, i: (0, 0)),
    ]
    yshape, kern, aliases = _branch_out(
        ybuf, slab, b, l, args, specs,
        functools.partial(_attn_self_kernel, tq=tq, tk=tq, lam_init=lam_init, n_cls=n_cls, group=4))
    return pl.pallas_call(
        kern,
        out_shape=yshape,
        grid=(b, ATT_KV, l // tq),
        in_specs=specs,
        out_specs=pl.BlockSpec((None, 1, tq, hw), lambda bi, kv, i: (slab, bi, i, kv)),
        scratch_shapes=[pltpu.VMEM((1, 4 * tq), F32), pltpu.VMEM((1, 4 * tq), F32),
                        pltpu.VMEM((ATT_VD, 4 * tq), F32)],
        input_output_aliases=aliases,
        compiler_params=_cparams(("parallel", "parallel", "arbitrary")),
        name="diff_attn_self",
    )(*args)


def _attn_cached_kernel(cls_ref, q_ref, ck_ref, cv_ref, kn_ref, vn_ref, bp_ref, bn_ref, lam_ref, gs_ref,
                        o_ref, m_ref, l_ref, acc_ref, *, tq, tk, nt, lam_init):
    del cls_ref
    j = pl.program_id(1)

    @pl.when(j == 0)
    def _():
        m_ref[...] = jnp.full(m_ref.shape, NEG, F32)
        l_ref[...] = jnp.zeros(l_ref.shape, F32)
        acc_ref[...] = jnp.zeros(acc_ref.shape, F32)

    dn_t = (((1,), (1,)), ((), ()))

    def update(k_of, v_of, bias_at):
        kvs = range(ATT_KV)
        ss = []
        for kv in kvs:
            sm = []
            for mp in range(2):
                c0 = kv * 2 * LANES + mp * ATT_D
                qm = jnp.concatenate([q_ref[0, :, c0 + r * LANES:c0 + r * LANES + ATT_D] for r in range(2)],
                                     axis=0)
                sm.append(lax.dot_general(qm, k_of(kv, mp).astype(BF16), dn_t, preferred_element_type=F32))
            ss.append(jnp.concatenate([sm[0][:tq], sm[1][:tq], sm[0][tq:], sm[1][tq:]], axis=0))
        pa = [_osm_softmax(ss[kv], bias_at(2 * kv), bias_at(2 * kv + 1), m_ref.at[kv], l_ref.at[kv])
              for kv in kvs]
        for kv in kvs:
            p, alpha = pa[kv]
            acc_ref[kv] = alpha * acc_ref[kv] + jnp.dot(p, v_of(kv).astype(BF16), preferred_element_type=F32)

    @pl.when(j < nt)
    def _():
        update(lambda kv, mp: ck_ref[pl.ds(kv * 2 + mp, tk, stride=2 * ATT_KV), :],
               lambda kv: cv_ref[pl.ds(kv, tk, stride=ATT_KV), :], lambda h: bp_ref[0, h])

    @pl.when(j == nt)
    def _():
        update(lambda kv, mp: kn_ref[0, :, kv * LANES + mp * ATT_D:kv * LANES + (mp + 1) * ATT_D],
               lambda kv: vn_ref[0, :, kv * LANES:(kv + 1) * LANES], lambda h: bn_ref[0, h])
        outs = [_osm_finish(l_ref.at[kv], acc_ref.at[kv], lam_ref, gs_ref, lam_init, tq)
                for kv in range(ATT_KV)]
        o_ref[0] = jnp.concatenate(outs, axis=-1).astype(o_ref.dtype)


def _attn_cached(qn, knb, vb, cache_k, cache_v, layer, bias_past, cls_tbl, bias_new, lam_p, g_subln,
                 lam_init, *, tk, ybuf=None, slab=0):
    b, l, _ = qn.shape
    past = cache_k.shape[2]
    nt = past // tk
    kw = ATT_KV * 2 * ATT_D
    rk, rv = 2 * ATT_KV, ATT_KV
    ck = cache_k.reshape(cache_k.shape[0], b, past * rk, ATT_D)
    cv = cache_v.reshape(cache_v.shape[0], b, past * rv, ATT_VD)
    args = [qn, ck, cv, knb, vb, bias_past, bias_new, lam_p, g_subln.reshape(1, ATT_VD)]
    specs = [
        pl.BlockSpec((1, l, ATT_H * 2 * ATT_D), lambda bi, j, cls: (bi, 0, 0)),
        pl.BlockSpec((None, None, tk * rk, ATT_D), lambda bi, j, cls: (layer, bi, jnp.minimum(j, nt - 1), 0)),
        pl.BlockSpec((None, None, tk * rv, ATT_VD), lambda bi, j, cls: (layer, bi, jnp.minimum(j, nt - 1), 0)),
        pl.BlockSpec((1, l, kw), lambda bi, j, cls: (bi, 0, 0)),
        pl.BlockSpec((1, l, kw), lambda bi, j, cls: (bi, 0, 0)),
        pl.BlockSpec((1, ATT_H, l, tk), lambda bi, j, cls: (cls[jnp.minimum(j, nt - 1)], 0, 0, 0)),
        pl.BlockSpec((1, ATT_H, l, l), lambda bi, j, cls: (0, 0, 0, 0)),
        pl.BlockSpec((4, ATT_D), lambda bi, j, cls: (0, 0)),
        pl.BlockSpec((1, ATT_VD), lambda bi, j, cls: (0, 0)),
    ]
    yshape, kern, aliases = _branch_out(
        ybuf, slab, b, l, args, specs,
        functools.partial(_attn_cached_kernel, tq=l, tk=tk, nt=nt, lam_init=lam_init), n_prefetch=1)
    grid_spec = pltpu.PrefetchScalarGridSpec(
        num_scalar_prefetch=1,
        grid=(b, nt + 1),
        in_specs=specs,
        out_specs=pl.BlockSpec((None, 1, l, ATT_H * ATT_VD), lambda bi, j, cls: (slab, bi, 0, 0)),
        scratch_shapes=[pltpu.VMEM((ATT_KV, 4 * l, LANES), F32), pltpu.VMEM((ATT_KV, 4 * l, LANES), F32),
                        pltpu.VMEM((ATT_KV, 4 * l, ATT_VD), F32)],
    )
    return pl.pallas_call(
        kern,
        out_shape=yshape,
        grid_spec=grid_spec,
        input_output_aliases=aliases,
        compiler_params=_cparams(("parallel", "arbitrary")),
        name="diff_attn_cached",
    )(cls_tbl, *args)


def _xattn_kernel(q_ref, mk_ref, mv_ref, o_ref):
    dn_t = (((1,), (1,)), ((), ()))
    outs = []
    for h in range(X_H):
        sl = slice(h * X_D, (h + 1) * X_D)
        s = lax.dot_general(q_ref[0, :, sl], mk_ref[0, :, sl].astype(BF16), dn_t,
                            preferred_element_type=F32) * (X_D ** -0.5)
        s = s - jnp.max(s, axis=-1, keepdims=True)
        p = jnp.exp(s)
        p = p / jnp.sum(p, axis=-1, keepdims=True)
        outs.append(jnp.dot(p.astype(BF16), mv_ref[0, :, sl].astype(BF16), preferred_element_type=F32))
    o_ref[0] = jnp.concatenate(outs, axis=-1).astype(o_ref.dtype)


def _xattn(q, mk, mv, *, tq):
    b, l, w = q.shape
    mlen = mk.shape[1]
    tq = min(tq, l)
    assert l % tq == 0
    return pl.pallas_call(
        _xattn_kernel,
        out_shape=jax.ShapeDtypeStruct((b, l, w), BF16),
        grid=(b, l // tq),
        in_specs=[
            pl.BlockSpec((1, tq, w), lambda i, c: (i, c, 0)),
            pl.BlockSpec((1, mlen, w), lambda i, c: (i, 0, 0)),
            pl.BlockSpec((1, mlen, w), lambda i, c: (i, 0, 0)),
        ],
        out_specs=pl.BlockSpec((1, tq, w), lambda i, c: (i, c, 0)),
        compiler_params=_cparams(("parallel", "parallel")),
        name="mem_cross_attn",
    )(q, mk, mv)


def _prep_weights(p, depth):
    layers = []
    off_z, off_xbc = 0, SSM_D
    off_dt = off_xbc + XBC_W
    off_pool = off_dt + SSM_H
    off_q = off_pool + BRANCH_W
    off_k = off_q + ATT_H * 2 * ATT_D
    off_v = off_k + ATT_KV * 2 * ATT_D
    off_gu = off_v + ATT_KV * ATT_VD
    off_gv = off_gu + BRANCH_W
    in_w = off_gv + BRANCH_W
    for l in range(depth):
        w = p['w_in'][l]
        d = w.shape[0]
        w_in = jnp.concatenate([
            w[:, off_xbc:off_dt], w[:, off_z:off_xbc], w[:, off_pool:off_q], w[:, off_q:off_k],
            w[:, off_gu:off_gv], w[:, off_gv:in_w], w[:, off_k:off_v], w[:, off_v:off_gu],
            w[:, off_dt:off_pool], jnp.zeros((d, LANES - SSM_H), w.dtype)], axis=1).astype(BF16)
        pad = lambda v: jnp.pad(v, (0, LANES - SSM_H)).reshape(1, LANES)
        layers.append(dict(
            g_ffn1=p['g_ffn1'][l], ffn1_in=p['w_ffn1_in'][l].astype(BF16), ffn1_out=p['w_ffn1_out'][l].astype(BF16),
            g_mix=p['g_mix'][l], w_in=w_in,
            conv_w=p['conv_w'][l], conv_b=p['conv_b'][l].reshape(1, XBC_W),
            dt_bias=pad(p['dt_bias'][l]), a_log=pad(p['a_log'][l]),
            d_skip=jnp.repeat(p['d_skip'][l], SSM_P).reshape(1, SSM_D), g_ssd=p['g_ssd'][l].reshape(1, SSM_D),
            w_pool=p['w_pool'][l].astype(BF16), pool_scale=p['pool_scale'][l].reshape(1, BRANCH_W),
            g_q=p['g_q'][l], g_k=p['g_k'][l], lam=p['lam'][l], g_subln=p['g_subln'][l],
            g_gv=p['g_gv'][l].reshape(1, BRANCH_W), w_sp=p['w_sp'][l], b_sp=p['b_sp'][l],
            w_gate=p['w_gate'][l].astype(BF16), b_gate=p['b_gate'][l], w_branch=p['w_branch'][l].astype(BF16),
            w_out=p['w_out'][l].astype(BF16),
            g_x=p['g_x'][l], w_xq=p['w_xq'][l].astype(BF16), g_xq=p['g_xq'][l], w_xo=p['w_xo'][l].astype(BF16),
            g_ffn2=p['g_ffn2'][l], ffn2_in=p['w_ffn2_in'][l].astype(BF16), ffn2_out=p['w_ffn2_out'][l].astype(BF16),
            g_post=p['g_post'][l],
            g_mem=p['g_mem'][l], w_mk=p['w_mk'][l].astype(BF16), w_mv=p['w_mv'][l].astype(BF16), g_xk=p['g_xk'][l],
        ))
    return layers


def _run_trunk(x, cache_k, cache_v, mem_k, mem_v, ssm0, conv0, pool0, layers, rel_bias):
    b, l, d = x.shape
    m = b * l
    depth = len(layers)
    cached = cache_k is not None
    assert l >= POOL_BUF and l >= CONV_W - 1
    if cached:
        past = cache_k.shape[2]
        tk = min(512, past)
        assert past % tk == 0 and past % CHUNK == 0 and l <= CHUNK and (past % GM_CHUNK == 0)
        nt = past // tk
        offs, cls = [FAR_OFFSET], []
        for j in range(nt):
            off = j * tk - past
            if _is_far(off, l, tk):
                cls.append(0)
            else:
                offs.append(off)
                cls.append(len(offs) - 1)
        bias_past = _bias_tiles(rel_bias, offs, l, tk)
        cls_tbl = jnp.asarray(np.asarray(cls, np.int32))
        bias_new = _bias_tiles(rel_bias, [0], l, l)
    else:
        past = 0
        tq = min(512, l)
        assert l % tq == 0 and tq % CHUNK == 0
        offs = [0]
        if l > tq:
            offs.append(-tq)
        if l > 2 * tq:
            assert _is_far(-2 * tq, tq, tq)
            offs.append(FAR_OFFSET)
        bias_self = _bias_tiles(rel_bias, offs, tq, tq, keys_major=True)

    h = x.reshape(m, d)
    new_k, new_v, new_ssm, new_conv, new_pool, new_gv = [], [], [], [], [], []
    for li, w in enumerate(layers):
        lam_init = 0.8 - 0.6 * math.exp(-0.3 * li)
        h = _ffn(h, w['g_ffn1'], w['ffn1_in'], w['ffn1_out'])
        proj = _mm(h, w['w_in'], g=w['g_mix'], out_dtype=F32, tm=1024, tn=1664, name="in_proj")
        proj = proj.reshape(b, l, P_W)

        ys, s_ssm = _ssd(proj, conv0[li] if cached else None, ssm0[li] if cached else None,
                         w['conv_w'], w['conv_b'], w['dt_bias'], w['a_log'], w['d_skip'], w['g_ssd'], q=128)
        ys = _pool(proj, pool0[li] if cached else None, past, w['w_pool'], w['pool_scale'], t=256,
                   ybuf=ys, slab=1)
        if cached:
            qn, kn, knb, vb = _qknorm(proj, w['g_q'], w['g_k'], t=512)
            ys = _attn_cached(qn, knb, vb, cache_k, cache_v, li, bias_past, cls_tbl, bias_new,
                              w['lam'], w['g_subln'], lam_init, tk=tk, ybuf=ys, slab=2)
        else:
            qt, kn, knb, vt = _qknorm_t(proj, w['g_q'], w['g_k'], t=tq)
            ys = _attn_self(qt, knb, vt, bias_self, w['lam'], w['g_subln'], lam_init, tq=tq, ybuf=ys, slab=2)
        ys, v_gm = _gmlp(proj, w['g_gv'], w['w_sp'], w['b_sp'], keep_vn=cached, ybuf=ys, slab=3)

        merged = _merge(h, w['g_mix'], ys.reshape(N_BRANCH, m, BRANCH_W), w['w_gate'], w['b_gate'], w['w_branch'])
        h = _mm(merged, w['w_out'], res=h, out_dtype=F32, tm=512, tn=1024, name="out_proj", w_resident=True)

        qx = _mm(h, w['w_xq'], g=w['g_x'], gh=w['g_xq'], out_dtype=BF16, tm=512, tn=512, name="xattn_q")
        o = _xattn(qx.reshape(b, l, X_H * X_D), mem_k[li].reshape(b, -1, X_H * X_D),
                   mem_v[li].reshape(b, -1, X_H * X_D), tq=512)
        h = _ffn(h, w['g_ffn2'], w['ffn2_in'], w['ffn2_out'], post_g=w['g_post'],
                 pre=(o.reshape(m, X_H * X_D), w['w_xo']))

        new_k.append(kn.reshape(b, l, ATT_KV, 2, ATT_D))
        new_v.append(proj[:, :, P_V:P_V + ATT_KV * ATT_VD].reshape(b, l, ATT_KV, ATT_VD))
        new_ssm.append(s_ssm)
        new_conv.append(proj[:, l - (CONV_W - 1):, P_XBC:P_XBC + XBC_W])
        new_pool.append(proj[:, l - POOL_BUF:, P_POOL:P_POOL + BRANCH_W])
        new_gv.append(v_gm)
    return (h.reshape(b, l, d), jnp.stack(new_k), jnp.stack(new_v), jnp.stack(new_ssm),
            jnp.stack(new_conv), jnp.stack(new_pool), jnp.stack(new_gv) if cached else None)


def kernel(x_prompt, x_sample, cache_attn_k, cache_attn_v, cache_mem_k, cache_mem_v, state_ssm, state_conv, state_pool, mem_prompt, g_ffn1, w_ffn1_in, w_ffn1_out, g_mix, w_in, conv_w, conv_b, dt_bias, a_log, d_skip, g_ssd, w_pool, pool_scale, g_q, g_k, lam, g_subln, rel_bias, g_gv, w_sp, b_sp, w_gate, b_gate, w_branch, w_out, g_x, w_xq, g_xq, g_mem, w_mk, w_mv, g_xk, w_xo, g_ffn2, w_ffn2_in, w_ffn2_out, g_post):
    p = dict(g_ffn1=g_ffn1, w_ffn1_in=w_ffn1_in, w_ffn1_out=w_ffn1_out, g_mix=g_mix, w_in=w_in,
             conv_w=conv_w, conv_b=conv_b, dt_bias=dt_bias, a_log=a_log, d_skip=d_skip, g_ssd=g_ssd,
             w_pool=w_pool, pool_scale=pool_scale, g_q=g_q, g_k=g_k, lam=lam, g_subln=g_subln,
             g_gv=g_gv, w_sp=w_sp, b_sp=b_sp, w_gate=w_gate, b_gate=b_gate, w_branch=w_branch,
             w_out=w_out, g_x=g_x, w_xq=w_xq, g_xq=g_xq, g_mem=g_mem, w_mk=w_mk, w_mv=w_mv, g_xk=g_xk,
             w_xo=w_xo, g_ffn2=g_ffn2, w_ffn2_in=w_ffn2_in, w_ffn2_out=w_ffn2_out, g_post=g_post)
    depth = w_in.shape[0]
    layers = _prep_weights(p, depth)

    bp, mlen, d = mem_prompt.shape
    mem2 = mem_prompt.reshape(bp * mlen, d)
    mks, mvs = [], []
    for w in layers:
        mk = _mm(mem2, w['w_mk'], g=w['g_mem'], gh=w['g_xk'], out_dtype=F32, tm=512, tn=512, name="mem_k")
        mv = _mm(mem2, w['w_mv'], g=w['g_mem'], out_dtype=F32, tm=512, tn=512, name="mem_v")
        mks.append(mk.reshape(bp, mlen, X_H, X_D))
        mvs.append(mv.reshape(bp, mlen, X_H, X_D))
    p_mem_k = jnp.stack(mks)
    p_mem_v = jnp.stack(mvs)

    y_prompt, p_attn_k, p_attn_v, p_ssm, p_conv, p_pool, _ = _run_trunk(
        x_prompt, None, None, p_mem_k, p_mem_v, None, None, None, layers, rel_bias)
    y_sample, s_attn_k, s_attn_v, s_ssm, s_conv, s_pool, s_gmlp_v = _run_trunk(
        x_sample, cache_attn_k, cache_attn_v, cache_mem_k, cache_mem_v, state_ssm, state_conv,
        state_pool, layers, rel_bias)

    return (y_prompt, y_sample, p_attn_k, p_attn_v, p_mem_k, p_mem_v, p_ssm, p_conv, p_pool,
            s_attn_k, s_attn_v, s_ssm, s_conv, s_pool, s_gmlp_v)
```

```python
import functools
import math

import jax
import jax.numpy as jnp
import numpy as np
from jax import lax
from jax.experimental import pallas as pl
from jax.experimental.pallas import tpu as pltpu

F32 = jnp.float32
BF16 = jnp.bfloat16
EPS = 1e-6
NEG = -1e30

VMEM_LIMIT_BYTES = 56 * 1024 * 1024
LANES = 128

BRANCH_W = 1024
SSM_P = 64
SSM_H = 16
SSM_G = 4
SSM_N = 128
SSM_D = SSM_H * SSM_P
CONV_W = 4
XBC_W = SSM_D + 2 * SSM_G * SSM_N
POOL_WINDOWS = (2, 4, 8, 16)
POOL_GW = BRANCH_W // len(POOL_WINDOWS)
POOL_BUF = max(POOL_WINDOWS) - 1
ATT_H = 8
ATT_KV = 4
ATT_D = 64
ATT_VD = 128
CHUNK = 64
N_BUCKETS = 32
MAX_DIST = 128
GM_CHUNK = 128
GM_G = 4
GM_GW = BRANCH_W // GM_G
X_H = 4
X_D = 128
VT_ONES = 16

P_XBC = 0
P_Z = 2048
P_POOL = 3072
P_Q = 4096
P_GU = 5120
P_GV = 6144
P_K = 7168
P_V = 7680
P_DT = 8192
P_W = 8320


def _cparams(sem):
    return pltpu.CompilerParams(dimension_semantics=sem, vmem_limit_bytes=VMEM_LIMIT_BYTES)


N_BRANCH = 4


def _skip_ref(fn, pos):
    def wrapped(*refs):
        return fn(*refs[:pos], *refs[pos + 1:])
    return wrapped


def _branch_out(ybuf, slab, b, l, args, specs, kernel_fn, n_prefetch=0):
    shape = jax.ShapeDtypeStruct((N_BRANCH, b, l, BRANCH_W), BF16)
    if ybuf is None:
        return shape, kernel_fn, {}
    assert slab > 0 and ybuf.shape == shape.shape and ybuf.dtype == shape.dtype
    pos = n_prefetch + len(args)
    args.append(ybuf)
    specs.append(pl.BlockSpec(memory_space=pl.ANY))
    return shape, _skip_ref(kernel_fn, pos), {pos: 0}


def _rms(xf, g):
    ms = jnp.mean(xf * xf, axis=-1, keepdims=True)
    return xf * lax.rsqrt(ms + EPS) * g


def _silu(x):
    return x * jax.nn.sigmoid(x)


def _mm_kernel(*refs, norm, head_norm, residual):
    it = iter(refs)
    x_ref = next(it)
    g_ref = next(it) if norm else None
    w_ref = next(it)
    gh_ref = next(it) if head_norm else None
    res_ref = next(it) if residual else None
    o_ref = next(it)
    xn_ref = next(it) if norm else None

    if norm:
        @pl.when(pl.program_id(1) == 0)
        def _():
            xn_ref[...] = _rms(x_ref[...].astype(F32), g_ref[...]).astype(BF16)
        xb = xn_ref[...]
    else:
        xb = x_ref[...].astype(BF16)
    acc = jnp.dot(xb, w_ref[...], preferred_element_type=F32)
    if head_norm:
        parts = []
        for c in range(acc.shape[1] // LANES):
            parts.append(_rms(acc[:, c * LANES:(c + 1) * LANES], gh_ref[...]))
        acc = jnp.concatenate(parts, axis=-1)
    if residual:
        acc = res_ref[...] + acc
    o_ref[...] = acc.astype(o_ref.dtype)


def _mm(x, w, *, g=None, gh=None, res=None, out_dtype, tm, tn, name, w_resident=False):
    m, k = x.shape
    n = w.shape[1]
    tm = min(tm, m)
    tn = min(tn, n)
    assert m % tm == 0 and n % tn == 0
    norm, head_norm, residual = g is not None, gh is not None, res is not None
    assert not (norm and w_resident)
    if w_resident:
        grid = (n // tn, m // tm)
        ij = lambda a, b: (b, a)
    else:
        grid = (m // tm, n // tn)
        ij = lambda a, b: (a, b)
    args, specs = [x], [pl.BlockSpec((tm, k), lambda a, b: (ij(a, b)[0], 0))]
    if norm:
        args.append(g.reshape(1, k))
        specs.append(pl.BlockSpec((1, k), lambda a, b: (0, 0)))
    args.append(w)
    specs.append(pl.BlockSpec((k, tn), lambda a, b: (0, ij(a, b)[1])))
    if head_norm:
        args.append(gh.reshape(1, LANES))
        specs.append(pl.BlockSpec((1, LANES), lambda a, b: (0, 0)))
    if residual:
        args.append(res)
        specs.append(pl.BlockSpec((tm, tn), lambda a, b: ij(a, b)))
    scratch = [pltpu.VMEM((tm, k), BF16)] if norm else []
    return pl.pallas_call(
        functools.partial(_mm_kernel, norm=norm, head_norm=head_norm, residual=residual),
        out_shape=jax.ShapeDtypeStruct((m, n), out_dtype),
        grid=grid,
        in_specs=specs,
        out_specs=pl.BlockSpec((tm, tn), lambda a, b: ij(a, b)),
        scratch_shapes=scratch,
        compiler_params=_cparams(("parallel", "arbitrary")),
        name=name,
    )(*args)


def _ffn_kernel(*refs, post, pre):
    it = iter(refs)
    x_ref, g_ref, wg_ref, wu_ref, wo_ref = (next(it) for _ in range(5))
    gp_ref = next(it) if post else None
    a_ref, wa_ref = (next(it), next(it)) if pre else (None, None)
    o_ref, xn_ref, acc_ref = next(it), next(it), next(it)
    f = pl.program_id(1)

    @pl.when(f == 0)
    def _():
        x = x_ref[...]
        if pre:
            x = x + jnp.dot(a_ref[...], wa_ref[...], preferred_element_type=F32)
            o_ref[...] = x
        xn_ref[...] = _rms(x, g_ref[...]).astype(BF16)
        acc_ref[...] = jnp.zeros_like(acc_ref)

    xb = xn_ref[...]
    gate = jnp.dot(xb, wg_ref[...], preferred_element_type=F32)
    up = jnp.dot(xb, wu_ref[...], preferred_element_type=F32)
    mid = (_silu(gate) * up).astype(BF16)
    acc_ref[...] += jnp.dot(mid, wo_ref[...], preferred_element_type=F32)

    @pl.when(f == pl.num_programs(1) - 1)
    def _():
        h = (o_ref[...] if pre else x_ref[...]) + 0.5 * acc_ref[...]
        if post:
            h = _rms(h, gp_ref[...])
        o_ref[...] = h


def _ffn(x, g, w_in, w_out, *, post_g=None, pre=None, tm=512, tf=512):
    m, d = x.shape
    ff = w_out.shape[0]
    tm = min(tm, m)
    assert m % tm == 0 and ff % tf == 0
    nf = ff // tf
    post = post_g is not None
    args = [x, g.reshape(1, d), w_in, w_in, w_out]
    specs = [
        pl.BlockSpec((tm, d), lambda i, f: (i, 0)),
        pl.BlockSpec((1, d), lambda i, f: (0, 0)),
        pl.BlockSpec((d, tf), lambda i, f: (0, f)),
        pl.BlockSpec((d, tf), lambda i, f: (0, nf + f)),
        pl.BlockSpec((tf, d), lambda i, f: (f, 0)),
    ]
    if post:
        args.append(post_g.reshape(1, d))
        specs.append(pl.BlockSpec((1, d), lambda i, f: (0, 0)))
    if pre is not None:
        a, wa = pre
        ka = a.shape[1]
        args += [a, wa]
        specs += [pl.BlockSpec((tm, ka), lambda i, f: (i, 0)), pl.BlockSpec((ka, d), lambda i, f: (0, 0))]
    return pl.pallas_call(
        functools.partial(_ffn_kernel, post=post, pre=pre is not None),
        out_shape=jax.ShapeDtypeStruct((m, d), F32),
        grid=(m // tm, nf),
        in_specs=specs,
        out_specs=pl.BlockSpec((tm, d), lambda i, f: (i, 0)),
        scratch_shapes=[pltpu.VMEM((tm, d), BF16), pltpu.VMEM((tm, d), F32)],
        compiler_params=_cparams(("parallel", "arbitrary")),
        name="swiglu_ffn",
    )(*args)


def _merge_kernel(h_ref, g_ref, y_ref, wg_ref, bg_ref, wb_ref, o_ref, hn_ref, acc_ref):
    j = pl.program_id(1)
    br = pl.program_id(2)

    @pl.when((j == 0) & (br == 0))
    def _():
        hn_ref[...] = _rms(h_ref[...], g_ref[...]).astype(BF16)

    @pl.when(br == 0)
    def _():
        acc_ref[...] = jnp.zeros_like(acc_ref)

    gate = jax.nn.sigmoid(jnp.dot(hn_ref[...], wg_ref[...], preferred_element_type=F32) + bg_ref[...])
    acc_ref[...] += gate * jnp.dot(y_ref[...], wb_ref[...], preferred_element_type=F32)

    @pl.when(br == pl.num_programs(2) - 1)
    def _():
        o_ref[...] = acc_ref[...].astype(o_ref.dtype)


def _merge(h, g, ys, w_gate, b_gate, w_branch, *, tm=1024, tn=1024):
    m, d = h.shape
    nb, bw, _ = w_branch.shape
    tm = min(tm, m)
    assert m % tm == 0 and d % tn == 0 and ys.shape == (nb, m, bw)
    return pl.pallas_call(
        _merge_kernel,
        out_shape=jax.ShapeDtypeStruct((m, d), BF16),
        grid=(m // tm, d // tn, nb),
        in_specs=[
            pl.BlockSpec((tm, d), lambda i, j, b: (i, 0)),
            pl.BlockSpec((1, d), lambda i, j, b: (0, 0)),
            pl.BlockSpec((None, tm, bw), lambda i, j, b: (b, i, 0)),
            pl.BlockSpec((None, d, tn), lambda i, j, b: (b, 0, j)),
            pl.BlockSpec((None, 1, tn), lambda i, j, b: (b, 0, j)),
            pl.BlockSpec((None, bw, tn), lambda i, j, b: (b, 0, j)),
        ],
        out_specs=pl.BlockSpec((tm, tn), lambda i, j, b: (i, j)),
        scratch_shapes=[pltpu.VMEM((tm, d), BF16), pltpu.VMEM((tm, tn), F32)],
        compiler_params=_cparams(("parallel", "arbitrary", "arbitrary")),
        name="branch_merge",
    )(h, g.reshape(1, d), ys, w_gate, b_gate.reshape(nb, 1, d), w_branch)


def _ssd_kernel(*refs, q, has_state):
    it = iter(refs)
    xbc_ref, z_ref, dt_ref = next(it), next(it), next(it)
    conv0_ref = next(it) if has_state else None
    s0_ref = next(it) if has_state else None
    cw_ref, cb_ref, dtb_ref, alog_ref, dskip_ref, gssd_ref = (next(it) for _ in range(6))
    y_ref, s_ref, cbuf = next(it), next(it), next(it)
    hist = 8

    @pl.when(pl.program_id(1) == 0)
    def _():
        cbuf[0:hist, :] = jnp.zeros((hist, XBC_W), F32)
        if has_state:
            cbuf[hist - (CONV_W - 1):hist, :] = conv0_ref[0]
            s_ref[0] = s0_ref[0]
        else:
            s_ref[0] = jnp.zeros(s_ref.shape[1:], F32)

    cbuf[hist:hist + q, :] = xbc_ref[0]
    conv = cb_ref[...]
    for k in range(CONV_W):
        lo = hist - (CONV_W - 1) + k
        conv = conv + cbuf[lo:lo + q, :] * cw_ref[k:k + 1, :]
    cbuf[0:hist, :] = cbuf[q:q + hist, :]
    xc = _silu(conv)
    xs = xc[:, :SSM_D]
    bm = xc[:, SSM_D:SSM_D + SSM_G * SSM_N]
    cm = xc[:, SSM_D + SSM_G * SSM_N:]

    dtl = dt_ref[0] + dtb_ref[...]
    dt = jnp.maximum(dtl, 0.0) + jnp.log1p(jnp.exp(-jnp.abs(dtl)))
    a = dt * (-jnp.exp(alog_ref[...]))
    ri = lax.broadcasted_iota(jnp.int32, (q, q), 0)
    ci = lax.broadcasted_iota(jnp.int32, (q, q), 1)
    tril = ri >= ci
    cum = jnp.dot(tril.astype(F32), a, preferred_element_type=F32,
                  precision=lax.Precision.HIGHEST)
    cum_t = cum.T
    dt_t = dt.T
    cum_last = cum[q - 1:q, :]
    w_state = jnp.exp(cum_last - cum) * dt
    ecum = jnp.exp(cum)
    ecl = jnp.exp(cum_last)

    lane_lo = lax.broadcasted_iota(jnp.int32, (q, LANES), 1) < SSM_P
    row_lo_t = lax.broadcasted_iota(jnp.int32, (LANES, q), 0) < SSM_P
    row_lo_s = lax.broadcasted_iota(jnp.int32, (LANES, SSM_N), 0) < SSM_P
    dn_t = (((1,), (1,)), ((), ()))

    y_parts = []
    cb = None
    heads_per_group = SSM_H // SSM_G
    for pp in range(SSM_H // 2):
        grp = (2 * pp) // heads_per_group
        xp = xs[:, pp * LANES:(pp + 1) * LANES]
        bg = bm[:, grp * SSM_N:(grp + 1) * SSM_N]
        cg = cm[:, grp * SSM_N:(grp + 1) * SSM_N]
        cg_b = cg.astype(BF16)
        if (2 * pp) % heads_per_group == 0:
            cb = lax.dot_general(cg_b, bg.astype(BF16), dn_t, preferred_element_type=F32)
        xp_t = xp.T
        s_pair = s_ref[0, pp * LANES:(pp + 1) * LANES, :]
        y_pair = None
        s_new = None
        for hh in range(2):
            hd = 2 * pp + hh
            seg = cum[:, hd:hd + 1] - cum_t[hd:hd + 1, :]
            mat = jnp.exp(jnp.where(tril, seg, NEG)) * cb * dt_t[hd:hd + 1, :]
            sel = lane_lo if hh == 0 else jnp.logical_not(lane_lo)
            x_h = jnp.where(sel, xp, 0.0).astype(BF16)
            yd = jnp.dot(mat.astype(BF16), x_h, preferred_element_type=F32)
            y_pair = yd if y_pair is None else y_pair + yd
            sel_t = row_lo_t if hh == 0 else jnp.logical_not(row_lo_t)
            xt_h = jnp.where(sel_t, xp_t, 0.0).astype(BF16)
            bs = (bg * w_state[:, hd:hd + 1]).astype(BF16)
            sn = jnp.dot(xt_h, bs, preferred_element_type=F32)
            s_new = sn if s_new is None else s_new + sn
        y_off = lax.dot_general(cg_b, s_pair.astype(BF16), dn_t, preferred_element_type=F32)
        e_pair = jnp.where(lane_lo, ecum[:, 2 * pp:2 * pp + 1], ecum[:, 2 * pp + 1:2 * pp + 2])
        y_parts.append(y_pair + y_off * e_pair)
        dec = jnp.where(row_lo_s, ecl[:, 2 * pp:2 * pp + 1], ecl[:, 2 * pp + 1:2 * pp + 2])
        s_ref[0, pp * LANES:(pp + 1) * LANES, :] = dec * s_pair + s_new

    y = jnp.concatenate(y_parts, axis=-1)
    y = y + dskip_ref[...] * xs
    y = y * _silu(z_ref[0])
    gw = SSM_D // SSM_G
    outs = [_rms(y[:, gi * gw:(gi + 1) * gw], gssd_ref[:, gi * gw:(gi + 1) * gw]) for gi in range(SSM_G)]
    y_ref[0] = jnp.concatenate(outs, axis=-1).astype(y_ref.dtype)


def _ssd(proj, conv0, s0, cw, cb, dtb, alog, dskip, gssd, *, q, ybuf=None, slab=0):
    b, l, _ = proj.shape
    q = min(q, l)
    assert l % q == 0 and q >= 8
    has_state = s0 is not None
    args = [proj, proj, proj]
    specs = [
        pl.BlockSpec((1, q, XBC_W), lambda i, c: (i, c, P_XBC // XBC_W)),
        pl.BlockSpec((1, q, SSM_D), lambda i, c: (i, c, P_Z // SSM_D)),
        pl.BlockSpec((1, q, LANES), lambda i, c: (i, c, P_DT // LANES)),
    ]
    if has_state:
        args += [conv0, s0.reshape(b, SSM_H * SSM_P, SSM_N)]
        specs += [
            pl.BlockSpec((1, CONV_W - 1, XBC_W), lambda i, c: (i, 0, 0)),
            pl.BlockSpec((1, SSM_H * SSM_P, SSM_N), lambda i, c: (i, 0, 0)),
        ]
    args += [cw, cb, dtb, alog, dskip, gssd]
    specs += [
        pl.BlockSpec((CONV_W, XBC_W), lambda i, c: (0, 0)),
        pl.BlockSpec((1, XBC_W), lambda i, c: (0, 0)),
        pl.BlockSpec((1, LANES), lambda i, c: (0, 0)),
        pl.BlockSpec((1, LANES), lambda i, c: (0, 0)),
        pl.BlockSpec((1, SSM_D), lambda i, c: (0, 0)),
        pl.BlockSpec((1, SSM_D), lambda i, c: (0, 0)),
    ]
    yshape, kern, aliases = _branch_out(ybuf, slab, b, l, args, specs,
                                        functools.partial(_ssd_kernel, q=q, has_state=has_state))
    y, s = pl.pallas_call(
        kern,
        out_shape=(yshape, jax.ShapeDtypeStruct((b, SSM_H * SSM_P, SSM_N), F32)),
        grid=(b, l // q),
        in_specs=specs,
        out_specs=(pl.BlockSpec((None, 1, q, SSM_D), lambda i, c: (slab, i, c, 0)),
                   pl.BlockSpec((1, SSM_H * SSM_P, SSM_N), lambda i, c: (i, 0, 0))),
        scratch_shapes=[pltpu.VMEM((q + 8, XBC_W), F32)],
        input_output_aliases=aliases,
        compiler_params=_cparams(("parallel", "arbitrary")),
        name="ssd_mixer",
    )(*args)
    return y, s.reshape(b, SSM_H, SSM_P, SSM_N)


def _pool_kernel(*refs, t, pos0, has_state):
    it = iter(refs)
    xp_ref = next(it)
    p0_ref = next(it) if has_state else None
    w_ref, sc_ref, y_ref, buf = next(it), next(it), next(it), next(it)
    hist = POOL_BUF + 1
    ti = pl.program_id(1)

    @pl.when(ti == 0)
    def _():
        buf[0:hist, :] = jnp.zeros((hist, BRANCH_W), F32)
        if has_state:
            buf[1:hist, :] = p0_ref[0]

    xp = xp_ref[0]
    buf[hist:hist + t, :] = xp
    pos = pos0 + ti * t + lax.broadcasted_iota(jnp.int32, (t, POOL_GW), 0)
    outs = []
    for gi, w in enumerate(POOL_WINDOWS):
        c0 = gi * POOL_GW
        s = xp[:, c0:c0 + POOL_GW]
        for k in range(1, w):
            s = s + buf[hist - k:hist - k + t, c0:c0 + POOL_GW]
        cnt = jnp.minimum(w, pos + 1).astype(F32)
        pooled = s / cnt - xp[:, c0:c0 + POOL_GW]
        y = jnp.dot(pooled.astype(BF16), w_ref[gi], preferred_element_type=F32)
        outs.append(y * sc_ref[:, c0:c0 + POOL_GW])
    y_ref[0] = jnp.concatenate(outs, axis=-1).astype(y_ref.dtype)
    buf[0:hist, :] = buf[t:t + hist, :]


def _pool(proj, pool0, pos0, w_pool, scale, *, t, ybuf=None, slab=0):
    b, l, _ = proj.shape
    t = min(t, l)
    assert l % t == 0 and t >= POOL_BUF + 1
    has_state = pool0 is not None
    args = [proj]
    specs = [pl.BlockSpec((1, t, BRANCH_W), lambda i, c: (i, c, P_POOL // BRANCH_W))]
    if has_state:
        args.append(pool0)
        specs.append(pl.BlockSpec((1, POOL_BUF, BRANCH_W), lambda i, c: (i, 0, 0)))
    args += [w_pool, scale]
    specs += [
        pl.BlockSpec((len(POOL_WINDOWS), POOL_GW, POOL_GW), lambda i, c: (0, 0, 0)),
        pl.BlockSpec((1, BRANCH_W), lambda i, c: (0, 0)),
    ]
    yshape, kern, aliases = _branch_out(
        ybuf, slab, b, l, args, specs, functools.partial(_pool_kernel, t=t, pos0=pos0, has_state=has_state))
    return pl.pallas_call(
        kern,
        out_shape=yshape,
        grid=(b, l // t),
        in_specs=specs,
        out_specs=pl.BlockSpec((None, 1, t, BRANCH_W), lambda i, c: (slab, i, c, 0)),
        scratch_shapes=[pltpu.VMEM((t + POOL_BUF + 1, BRANCH_W), F32)],
        input_output_aliases=aliases,
        compiler_params=_cparams(("parallel", "arbitrary")),
        name="pool_mixer",
    )(*args)


def _gmlp_kernel(u_ref, v_ref, g_ref, w_ref, b_ref, y_ref, *rest, cl, nsub, keep_vn):
    tril = lax.broadcasted_iota(jnp.int32, (cl, cl), 0) >= lax.broadcasted_iota(jnp.int32, (cl, cl), 1)
    ws = [jnp.where(tril, w_ref[gi], 0.0).astype(BF16) for gi in range(GM_G)]
    for sub in range(nsub):
        rows = slice(sub * cl, (sub + 1) * cl)
        u = jax.nn.gelu(u_ref[0, rows, :], approximate=True)
        vn = _rms(jax.nn.gelu(v_ref[0, rows, :], approximate=True), g_ref[...])
        if keep_vn:
            rest[0][0, rows, :] = vn
        outs = []
        for gi in range(GM_G):
            s = jnp.dot(ws[gi], vn[:, gi * GM_GW:(gi + 1) * GM_GW].astype(BF16), preferred_element_type=F32)
            s = s + b_ref[:, gi:gi + 1]
            outs.append(u[:, gi * GM_GW:(gi + 1) * GM_GW] * s)
        y_ref[0, rows, :] = jnp.concatenate(outs, axis=-1).astype(y_ref.dtype)


def _gmlp(proj, g_gv, w_sp, b_sp, *, keep_vn, ybuf=None, slab=0):
    b, l, _ = proj.shape
    cl = min(GM_CHUNK, l)
    assert l % cl == 0
    nsub = math.gcd(l // cl, 4)
    t = cl * nsub
    w = w_sp[:, :cl, :cl]
    bt = b_sp[:, :cl].T
    args = [proj, proj, g_gv, w, bt]
    specs = [
        pl.BlockSpec((1, t, BRANCH_W), lambda i, c: (i, c, P_GU // BRANCH_W)),
        pl.BlockSpec((1, t, BRANCH_W), lambda i, c: (i, c, P_GV // BRANCH_W)),
        pl.BlockSpec((1, BRANCH_W), lambda i, c: (0, 0)),
        pl.BlockSpec((GM_G, cl, cl), lambda i, c: (0, 0, 0)),
        pl.BlockSpec((cl, GM_G), lambda i, c: (0, 0)),
    ]
    yshape, kern, aliases = _branch_out(
        ybuf, slab, b, l, args, specs, functools.partial(_gmlp_kernel, cl=cl, nsub=nsub, keep_vn=keep_vn))
    out_shape = [yshape]
    out_specs = [pl.BlockSpec((None, 1, t, BRANCH_W), lambda i, c: (slab, i, c, 0))]
    if keep_vn:
        out_shape.append(jax.ShapeDtypeStruct((b, l, BRANCH_W), F32))
        out_specs.append(pl.BlockSpec((1, t, BRANCH_W), lambda i, c: (i, c, 0)))
    outs = pl.pallas_call(
        kern,
        out_shape=tuple(out_shape),
        grid=(b, l // t),
        in_specs=specs,
        out_specs=tuple(out_specs),
        input_output_aliases=aliases,
        compiler_params=_cparams(("parallel", "parallel")),
        name="gmlp_mixer",
    )(*args)
    return outs[0], (outs[1] if keep_vn else None)


def _norm_halves(x, g):
    lo = lax.broadcasted_iota(jnp.int32, x.shape, 1) < ATT_D
    xx = x * x
    s_lo = jnp.sum(jnp.where(lo, xx, 0.0), axis=-1, keepdims=True)
    s_hi = jnp.sum(jnp.where(lo, 0.0, xx), axis=-1, keepdims=True)
    ms = jnp.where(lo, s_lo, s_hi) * (1.0 / ATT_D)
    return x * lax.rsqrt(ms + EPS) * g


def _qknorm_kernel(q_ref, k_ref, v_ref, gq_ref, gk_ref, qn_ref, kn_ref, knb_ref, vb_ref):
    scale = ATT_D ** -0.5
    q = q_ref[0]
    qs = [_norm_halves(q[:, c * LANES:(c + 1) * LANES], gq_ref[...]) * scale
          for c in range(q.shape[1] // LANES)]
    qn_ref[0] = jnp.concatenate(qs, axis=-1).astype(qn_ref.dtype)
    k = k_ref[0]
    ks = [_norm_halves(k[:, c * LANES:(c + 1) * LANES], gk_ref[...]) for c in range(k.shape[1] // LANES)]
    kn = jnp.concatenate(ks, axis=-1)
    kn_ref[0] = kn
    knb_ref[0] = kn.astype(BF16)
    vb_ref[0] = v_ref[0].astype(BF16)


def _qknorm(proj, g_q, g_k, *, t):
    b, l, _ = proj.shape
    t = min(t, l)
    assert l % t == 0
    qw, kw = ATT_H * 2 * ATT_D, ATT_KV * 2 * ATT_D
    gq = jnp.tile(g_q, 2).reshape(1, LANES)
    gk = jnp.tile(g_k, 2).reshape(1, LANES)
    return pl.pallas_call(
        _qknorm_kernel,
        out_shape=(jax.ShapeDtypeStruct((b, l, qw), BF16), jax.ShapeDtypeStruct((b, l, kw), F32),
                   jax.ShapeDtypeStruct((b, l, kw), BF16), jax.ShapeDtypeStruct((b, l, kw), BF16)),
        grid=(b, l // t),
        in_specs=[
            pl.BlockSpec((1, t, qw), lambda i, c: (i, c, P_Q // qw)),
            pl.BlockSpec((1, t, kw), lambda i, c: (i, c, P_K // kw)),
            pl.BlockSpec((1, t, kw), lambda i, c: (i, c, P_V // kw)),
            pl.BlockSpec((1, LANES), lambda i, c: (0, 0)),
            pl.BlockSpec((1, LANES), lambda i, c: (0, 0)),
        ],
        out_specs=(pl.BlockSpec((1, t, qw), lambda i, c: (i, c, 0)),
                   pl.BlockSpec((1, t, kw), lambda i, c: (i, c, 0)),
                   pl.BlockSpec((1, t, kw), lambda i, c: (i, c, 0)),
                   pl.BlockSpec((1, t, kw), lambda i, c: (i, c, 0))),
        compiler_params=_cparams(("parallel", "parallel")),
        name="qk_norm",
    )(proj, proj, proj, gq, gk)


def _bias_kernel(off_ref, tbl_ref, o_ref, *, tq, tk, keys_major):
    c = pl.program_id(0)
    h = pl.program_id(1)
    off = off_ref[c]
    shape = (tk, tq) if keys_major else (tq, tk)
    row = lax.broadcasted_iota(jnp.int32, shape, 1 if keys_major else 0)
    col = lax.broadcasted_iota(jnp.int32, shape, 0 if keys_major else 1)
    rel = off + col - row
    nb = N_BUCKETS // 2
    max_exact = nb // 2
    n = jnp.abs(rel)
    large = max_exact + (jnp.log(jnp.maximum(n, 1).astype(F32) / max_exact)
                         / math.log(MAX_DIST / max_exact) * (nb - max_exact)).astype(jnp.int32)
    large = jnp.minimum(large, nb - 1)
    bucket = jnp.where(rel > 0, nb, 0) + jnp.where(n < max_exact, n, large)
    val = jnp.zeros(shape, F32)
    for bk in range(N_BUCKETS):
        val = jnp.where(bucket == bk, tbl_ref[bk * ATT_H + h], val)
    visible = jnp.right_shift(off + col, 6) <= jnp.right_shift(row, 6)
    o_ref[0, 0] = jnp.where(visible, val, NEG)


def _bias_tiles(rel_bias, offsets, tq, tk, keys_major=False):
    assert CHUNK == 64
    offs = jnp.asarray(np.asarray(offsets, np.int32))
    n_cls = len(offsets)
    shape = (tk, tq) if keys_major else (tq, tk)
    return pl.pallas_call(
        functools.partial(_bias_kernel, tq=tq, tk=tk, keys_major=keys_major),
        out_shape=jax.ShapeDtypeStruct((n_cls, ATT_H) + shape, F32),
        grid=(n_cls, ATT_H),
        in_specs=[pl.BlockSpec(memory_space=pltpu.SMEM), pl.BlockSpec(memory_space=pltpu.SMEM)],
        out_specs=pl.BlockSpec((1, 1) + shape, lambda c, h: (c, h, 0, 0)),
        compiler_params=_cparams(("parallel", "parallel")),
        name="rel_bias_tiles",
    )(offs, rel_bias.reshape(-1))


FAR_OFFSET = -(1 << 20)


def _is_far(off, tq, tk):
    nb = N_BUCKETS // 2
    max_rel = off + tk - 1
    if max_rel >= 0:
        return False
    n_min = -max_rel
    sat = (nb // 2) * (MAX_DIST / (nb // 2)) ** ((nb - 1 - nb // 2) / (nb - nb // 2))
    return n_min >= math.ceil(sat) + 1


def _q4(q2):
    lo = lax.broadcasted_iota(jnp.int32, (q2.shape[0], LANES), 1) < ATT_D
    parts = []
    for r in range(2):
        qr = q2[:, r * LANES:(r + 1) * LANES]
        parts.append(jnp.where(lo, qr, jnp.zeros_like(qr)))
        parts.append(jnp.where(lo, jnp.zeros_like(qr), qr))
    return jnp.concatenate(parts, axis=0)


def _osm_softmax(s, b0, b1, m_ref, l_ref):
    s = s + jnp.concatenate([b0, b0, b1, b1], axis=0)
    m_prev = m_ref[...]
    m_new = jnp.maximum(m_prev, jnp.max(s, axis=-1, keepdims=True))
    alpha = jnp.exp(m_prev - m_new)
    p = jnp.exp(s - m_new[:, 0:1])
    l_ref[...] = alpha * l_ref[...] + jnp.sum(p, axis=-1, keepdims=True)
    m_ref[...] = m_new
    return p.astype(BF16), alpha


def _osm_finish(l_ref, acc_ref, lam_ref, gs_ref, lam_init, tq):
    lp = lam_ref[...]
    lam = (jnp.exp(jnp.sum(lp[0:1] * lp[1:2], axis=-1, keepdims=True))
           - jnp.exp(jnp.sum(lp[2:3] * lp[3:4], axis=-1, keepdims=True)) + lam_init)
    o = acc_ref[...] / l_ref[...]
    outs = []
    for r in range(2):
        d = o[(2 * r) * tq:(2 * r + 1) * tq] - lam * o[(2 * r + 1) * tq:(2 * r + 2) * tq]
        outs.append(_rms(d, gs_ref[...]) * (1.0 - lam_init))
    return jnp.concatenate(outs, axis=-1)


def _qknorm_t_kernel(q_ref, k_ref, v_ref, gq_ref, gk_ref, qt_ref, kn_ref, knb_ref, vt_ref):
    scale = ATT_D ** -0.5
    q = q_ref[0]
    t = q.shape[0]
    lo = lax.broadcasted_iota(jnp.int32, (t, LANES), 1) < ATT_D
    for kv in range(ATT_KV):
        cols = []
        for r in range(2):
            c = kv * 2 + r
            qr = _norm_halves(q[:, c * LANES:(c + 1) * LANES], gq_ref[...]) * scale
            cols.append(jnp.where(lo, qr, 0.0).T)
            cols.append(jnp.where(lo, 0.0, qr).T)
        qt_ref[0, kv] = jnp.concatenate(cols, axis=1).astype(qt_ref.dtype)
    k = k_ref[0]
    ks = [_norm_halves(k[:, c * LANES:(c + 1) * LANES], gk_ref[...]) for c in range(k.shape[1] // LANES)]
    kn = jnp.concatenate(ks, axis=-1)
    kn_ref[0] = kn
    knb_ref[0] = kn.astype(BF16)
    v = v_ref[0]
    for kv in range(ATT_KV):
        vt_ref[0, kv, 0] = jnp.concatenate(
            [v[:, kv * LANES:(kv + 1) * LANES].T, jnp.ones((VT_ONES, t), F32)], axis=0).astype(vt_ref.dtype)


def _qknorm_t(proj, g_q, g_k, *, t):
    b, l, _ = proj.shape
    assert l % t == 0
    qw, kw = ATT_H * 2 * ATT_D, ATT_KV * 2 * ATT_D
    gq = jnp.tile(g_q, 2).reshape(1, LANES)
    gk = jnp.tile(g_k, 2).reshape(1, LANES)
    return pl.pallas_call(
        _qknorm_t_kernel,
        out_shape=(jax.ShapeDtypeStruct((b, ATT_KV, LANES, 4 * l), BF16),
                   jax.ShapeDtypeStruct((b, l, kw), F32),
                   jax.ShapeDtypeStruct((b, l, kw), BF16),
                   jax.ShapeDtypeStruct((b, ATT_KV, l // t, ATT_VD + VT_ONES, t), BF16)),
        grid=(b, l // t),
        in_specs=[
            pl.BlockSpec((1, t, qw), lambda i, c: (i, c, P_Q // qw)),
            pl.BlockSpec((1, t, kw), lambda i, c: (i, c, P_K // kw)),
            pl.BlockSpec((1, t, kw), lambda i, c: (i, c, P_V // kw)),
            pl.BlockSpec((1, LANES), lambda i, c: (0, 0)),
            pl.BlockSpec((1, LANES), lambda i, c: (0, 0)),
        ],
        out_specs=(pl.BlockSpec((1, ATT_KV, LANES, 4 * t), lambda i, c: (i, 0, 0, c)),
                   pl.BlockSpec((1, t, kw), lambda i, c: (i, c, 0)),
                   pl.BlockSpec((1, t, kw), lambda i, c: (i, c, 0)),
                   pl.BlockSpec((1, ATT_KV, 1, ATT_VD + VT_ONES, t), lambda i, c: (i, 0, c, 0, 0))),
        compiler_params=_cparams(("parallel", "parallel")),
        name="qk_norm_t",
    )(proj, proj, proj, gq, gk)


def _attn_self_kernel(qt_ref, k_ref, vt_ref, bias_ref, lam_ref, gs_ref, o_ref, m_ref, acc_ref,
                      *, tq, tk, lam_init, n_cls, group):
    i = pl.program_id(2)
    m_ref[...] = jnp.full(m_ref.shape, NEG, F32)
    acc_ref[...] = jnp.zeros(acc_ref.shape, F32)
    sub = LANES
    assert tq == tk and tq % sub == 0 and sub % CHUNK == 0

    def step(j, mode):
        kt = k_ref[0, pl.ds(pl.multiple_of(j * tk, tk), tk), :]
        vt = vt_ref[0, 0, j]
        cls = jnp.minimum(i - j, n_cls - 1)
        for c0 in range(0, 4, group):
            update(kt, vt, cls, mode, range(c0, c0 + group))

    def softmax_block(s, m_prev, crow):
        m_cur = jnp.max(s, axis=0, keepdims=True)
        if crow is None:
            m_new = jnp.maximum(m_prev, m_cur)
            return jnp.exp(s - m_new).astype(BF16), m_new
        m_new = jnp.maximum(m_prev, m_cur + crow)
        return jnp.exp(s - (m_new - crow)).astype(BF16), m_new

    def update(kt, vt, cls, mode, combos):
        cols = {c: slice(c * tq, (c + 1) * tq) for c in combos}
        ss = {c: jnp.dot(kt, qt_ref[0, 0, :, cs], preferred_element_type=F32) for c, cs in cols.items()}
        ps, alphas = {}, {}
        for c, cs in cols.items():
            s = ss[c]
            m_prev = m_ref[:, cs]
            if mode == "far":
                p, m_new = softmax_block(s, m_prev, bias_ref[cls, c // 2, 0:1, :])
            elif mode == "near":
                p, m_new = softmax_block(s + bias_ref[cls, c // 2], m_prev, None)
            else:
                pb, mb = [], []
                for qb in range(tq // sub):
                    rows, lanes = (qb + 1) * sub, slice(qb * sub, (qb + 1) * sub)
                    p_b, m_b = softmax_block(s[:rows, lanes] + bias_ref[cls, c // 2, :rows, lanes],
                                             m_prev[:, lanes], None)
                    if rows < tk:
                        p_b = jnp.concatenate([p_b, jnp.zeros((tk - rows, sub), BF16)], axis=0)
                    pb.append(p_b)
                    mb.append(m_b)
                p, m_new = jnp.concatenate(pb, axis=1), jnp.concatenate(mb, axis=1)
            alphas[c] = jnp.exp(m_prev - m_new)
            m_ref[:, cs] = m_new
            ps[c] = p
        for c, cs in cols.items():
            acc_ref[:, cs] = alphas[c] * acc_ref[:, cs] + jnp.dot(vt, ps[c], preferred_element_type=F32)

    n_far = jnp.maximum(i - (n_cls - 2), 0) if n_cls == 3 else 0

    def body(mode):
        def run(j, carry):
            step(j, mode)
            return carry
        return run

    if n_cls == 3:
        lax.fori_loop(0, n_far, body("far"), 0)
    lax.fori_loop(n_far, i, body("near"), 0)
    step(i, "diag")

    lp = lam_ref[...]
    lam = (jnp.exp(jnp.sum(lp[0:1] * lp[1:2], axis=-1, keepdims=True))
           - jnp.exp(jnp.sum(lp[2:3] * lp[3:4], axis=-1, keepdims=True)) + lam_init)
    o = acc_ref[0:ATT_VD, :] / acc_ref[ATT_VD:ATT_VD + 1, :]
    outs = []
    for r in range(2):
        d = o[:, (2 * r) * tq:(2 * r + 1) * tq] - lam * o[:, (2 * r + 1) * tq:(2 * r + 2) * tq]
        ms = jnp.mean(d * d, axis=0, keepdims=True)
        y = d * lax.rsqrt(ms + EPS) * gs_ref[...] * (1.0 - lam_init)
        outs.append(y.T)
    o_ref[0] = jnp.concatenate(outs, axis=-1).astype(o_ref.dtype)


def _attn_self(qt, knb, vt, bias, lam_p, g_subln, lam_init, *, tq, ybuf=None, slab=0):
    b, l, _ = knb.shape
    n_cls = bias.shape[0]
    nk = l // tq
    hw = 2 * ATT_VD
    assert ATT_H * ATT_VD == BRANCH_W
    args = [qt, knb, vt, bias, lam_p, g_subln.reshape(ATT_VD, 1)]
    specs = [
        pl.BlockSpec((1, 1, LANES, 4 * tq), lambda bi, kv, i: (bi, kv, 0, i)),
        pl.BlockSpec((1, l, LANES), lambda bi, kv, i: (bi, 0, kv)),
        pl.BlockSpec((1, 1, nk, ATT_VD + VT_ONES, tq), lambda bi, kv, i: (bi, kv, 0, 0, 0)),
        pl.BlockSpec((n_cls, 2, tq, tq), lambda bi, kv, i: (0, kv, 0, 0)),
        pl.BlockSpec((4, ATT_D), lambda bi, kv, i: (0, 0)),
        pl.BlockSpec((ATT_VD, 1), lambda bi, kv, i: (0, 0)),
    ]
    yshape, kern, aliases = _branch_out(
        ybuf, slab, b, l, args, specs,
        functools.partial(_attn_self_kernel, tq=tq, tk=tq, lam_init=lam_init, n_cls=n_cls, group=4))
    return pl.pallas_call(
        kern,
        out_shape=yshape,
        grid=(b, ATT_KV, l // tq),
        in_specs=specs,
        out_specs=pl.BlockSpec((None, 1, tq, hw), lambda bi, kv, i: (slab, bi, i, kv)),
        scratch_shapes=[pltpu.VMEM((1, 4 * tq), F32), pltpu.VMEM((ATT_VD + VT_ONES, 4 * tq), F32)],
        input_output_aliases=aliases,
        compiler_params=_cparams(("parallel", "parallel", "arbitrary")),
        name="diff_attn_self",
    )(*args)


def _attn_cached_kernel(cls_ref, q_ref, ck_ref, cv_ref, kn_ref, vn_ref, bp_ref, bn_ref, lam_ref, gs_ref,
                        o_ref, m_ref, l_ref, acc_ref, *, tq, tk, nt, lam_init):
    del cls_ref
    j = pl.program_id(1)

    @pl.when(j == 0)
    def _():
        m_ref[...] = jnp.full(m_ref.shape, NEG, F32)
        l_ref[...] = jnp.zeros(l_ref.shape, F32)
        acc_ref[...] = jnp.zeros(acc_ref.shape, F32)

    dn_t = (((1,), (1,)), ((), ()))

    def update(k_all, v_of, bias_at):
        kvs = range(ATT_KV)
        ss = [lax.dot_general(_q4(q_ref[0, :, kv * 2 * LANES:(kv + 1) * 2 * LANES]),
                              k_all[:, kv * LANES:(kv + 1) * LANES].astype(BF16), dn_t,
                              preferred_element_type=F32) for kv in kvs]
        pa = [_osm_softmax(ss[kv], bias_at(2 * kv), bias_at(2 * kv + 1), m_ref.at[kv], l_ref.at[kv])
              for kv in kvs]
        for kv in kvs:
            p, alpha = pa[kv]
            acc_ref[kv] = alpha * acc_ref[kv] + jnp.dot(p, v_of(kv).astype(BF16), preferred_element_type=F32)

    @pl.when(j < nt)
    def _():
        update(ck_ref[...], lambda kv: cv_ref[pl.ds(kv, tk, stride=ATT_KV), :], lambda h: bp_ref[0, h])

    @pl.when(j == nt)
    def _():
        update(kn_ref[0], lambda kv: vn_ref[0, :, kv * LANES:(kv + 1) * LANES], lambda h: bn_ref[0, h])
        outs = [_osm_finish(l_ref.at[kv], acc_ref.at[kv], lam_ref, gs_ref, lam_init, tq)
                for kv in range(ATT_KV)]
        o_ref[0] = jnp.concatenate(outs, axis=-1).astype(o_ref.dtype)


def _attn_cached(qn, knb, vb, cache_k, cache_v, layer, bias_past, cls_tbl, bias_new, lam_p, g_subln,
                 lam_init, *, tk, ybuf=None, slab=0):
    b, l, _ = qn.shape
    past = cache_k.shape[2]
    nt = past // tk
    kw = ATT_KV * 2 * ATT_D
    ck = cache_k.reshape(cache_k.shape[0], b, past, kw)
    cv = cache_v.reshape(cache_v.shape[0], b, past * ATT_KV, ATT_VD)
    args = [qn, ck, cv, knb, vb, bias_past, bias_new, lam_p, g_subln.reshape(1, ATT_VD)]
    specs = [
        pl.BlockSpec((1, l, ATT_H * 2 * ATT_D), lambda bi, j, cls: (bi, 0, 0)),
        pl.BlockSpec((None, None, tk, kw), lambda bi, j, cls: (layer, bi, jnp.minimum(j, nt - 1), 0)),
        pl.BlockSpec((None, None, tk * ATT_KV, ATT_VD),
                     lambda bi, j, cls: (layer, bi, jnp.minimum(j, nt - 1), 0)),
        pl.BlockSpec((1, l, kw), lambda bi, j, cls: (bi, 0, 0)),
        pl.BlockSpec((1, l, kw), lambda bi, j, cls: (bi, 0, 0)),
        pl.BlockSpec((1, ATT_H, l, tk), lambda bi, j, cls: (cls[jnp.minimum(j, nt - 1)], 0, 0, 0)),
        pl.BlockSpec((1, ATT_H, l, l), lambda bi, j, cls: (0, 0, 0, 0)),
        pl.BlockSpec((4, ATT_D), lambda bi, j, cls: (0, 0)),
        pl.BlockSpec((1, ATT_VD), lambda bi, j, cls: (0, 0)),
    ]
    yshape, kern, aliases = _branch_out(
        ybuf, slab, b, l, args, specs,
        functools.partial(_attn_cached_kernel, tq=l, tk=tk, nt=nt, lam_init=lam_init), n_prefetch=1)
    grid_spec = pltpu.PrefetchScalarGridSpec(
        num_scalar_prefetch=1,
        grid=(b, nt + 1),
        in_specs=specs,
        out_specs=pl.BlockSpec((None, 1, l, ATT_H * ATT_VD), lambda bi, j, cls: (slab, bi, 0, 0)),
        scratch_shapes=[pltpu.VMEM((ATT_KV, 4 * l, LANES), F32), pltpu.VMEM((ATT_KV, 4 * l, LANES), F32),
                        pltpu.VMEM((ATT_KV, 4 * l, ATT_VD), F32)],
    )
    return pl.pallas_call(
        kern,
        out_shape=yshape,
        grid_spec=grid_spec,
        input_output_aliases=aliases,
        compiler_params=_cparams(("parallel", "arbitrary")),
        name="diff_attn_cached",
    )(cls_tbl, *args)


def _xattn_kernel(q_ref, mk_ref, mv_ref, o_ref):
    dn_t = (((1,), (1,)), ((), ()))
    outs = []
    for h in range(X_H):
        sl = slice(h * X_D, (h + 1) * X_D)
        s = lax.dot_general(q_ref[0, :, sl], mk_ref[0, :, sl].astype(BF16), dn_t,
                            preferred_element_type=F32) * (X_D ** -0.5)
        s = s - jnp.max(s, axis=-1, keepdims=True)
        p = jnp.exp(s)
        p = p / jnp.sum(p, axis=-1, keepdims=True)
        outs.append(jnp.dot(p.astype(BF16), mv_ref[0, :, sl].astype(BF16), preferred_element_type=F32))
    o_ref[0] = jnp.concatenate(outs, axis=-1).astype(o_ref.dtype)


def _xattn(q, mk, mv, *, tq):
    b, l, w = q.shape
    mlen = mk.shape[1]
    tq = min(tq, l)
    assert l % tq == 0
    return pl.pallas_call(
        _xattn_kernel,
        out_shape=jax.ShapeDtypeStruct((b, l, w), BF16),
        grid=(b, l // tq),
        in_specs=[
            pl.BlockSpec((1, tq, w), lambda i, c: (i, c, 0)),
            pl.BlockSpec((1, mlen, w), lambda i, c: (i, 0, 0)),
            pl.BlockSpec((1, mlen, w), lambda i, c: (i, 0, 0)),
        ],
        out_specs=pl.BlockSpec((1, tq, w), lambda i, c: (i, c, 0)),
        compiler_params=_cparams(("parallel", "parallel")),
        name="mem_cross_attn",
    )(q, mk, mv)


def _prep_weights(p, depth):
    layers = []
    off_z, off_xbc = 0, SSM_D
    off_dt = off_xbc + XBC_W
    off_pool = off_dt + SSM_H
    off_q = off_pool + BRANCH_W
    off_k = off_q + ATT_H * 2 * ATT_D
    off_v = off_k + ATT_KV * 2 * ATT_D
    off_gu = off_v + ATT_KV * ATT_VD
    off_gv = off_gu + BRANCH_W
    in_w = off_gv + BRANCH_W
    for l in range(depth):
        w = p['w_in'][l]
        d = w.shape[0]
        w_in = jnp.concatenate([
            w[:, off_xbc:off_dt], w[:, off_z:off_xbc], w[:, off_pool:off_q], w[:, off_q:off_k],
            w[:, off_gu:off_gv], w[:, off_gv:in_w], w[:, off_k:off_v], w[:, off_v:off_gu],
            w[:, off_dt:off_pool], jnp.zeros((d, LANES - SSM_H), w.dtype)], axis=1).astype(BF16)
        pad = lambda v: jnp.pad(v, (0, LANES - SSM_H)).reshape(1, LANES)
        layers.append(dict(
            g_ffn1=p['g_ffn1'][l], ffn1_in=p['w_ffn1_in'][l].astype(BF16), ffn1_out=p['w_ffn1_out'][l].astype(BF16),
            g_mix=p['g_mix'][l], w_in=w_in,
            conv_w=p['conv_w'][l], conv_b=p['conv_b'][l].reshape(1, XBC_W),
            dt_bias=pad(p['dt_bias'][l]), a_log=pad(p['a_log'][l]),
            d_skip=jnp.repeat(p['d_skip'][l], SSM_P).reshape(1, SSM_D), g_ssd=p['g_ssd'][l].reshape(1, SSM_D),
            w_pool=p['w_pool'][l].astype(BF16), pool_scale=p['pool_scale'][l].reshape(1, BRANCH_W),
            g_q=p['g_q'][l], g_k=p['g_k'][l], lam=p['lam'][l], g_subln=p['g_subln'][l],
            g_gv=p['g_gv'][l].reshape(1, BRANCH_W), w_sp=p['w_sp'][l], b_sp=p['b_sp'][l],
            w_gate=p['w_gate'][l].astype(BF16), b_gate=p['b_gate'][l], w_branch=p['w_branch'][l].astype(BF16),
            w_out=p['w_out'][l].astype(BF16),
            g_x=p['g_x'][l], w_xq=p['w_xq'][l].astype(BF16), g_xq=p['g_xq'][l], w_xo=p['w_xo'][l].astype(BF16),
            g_ffn2=p['g_ffn2'][l], ffn2_in=p['w_ffn2_in'][l].astype(BF16), ffn2_out=p['w_ffn2_out'][l].astype(BF16),
            g_post=p['g_post'][l],
            g_mem=p['g_mem'][l], w_mk=p['w_mk'][l].astype(BF16), w_mv=p['w_mv'][l].astype(BF16), g_xk=p['g_xk'][l],
        ))
    return layers


def _run_trunk(x, cache_k, cache_v, mem_k, mem_v, ssm0, conv0, pool0, layers, rel_bias):
    b, l, d = x.shape
    m = b * l
    depth = len(layers)
    cached = cache_k is not None
    assert l >= POOL_BUF and l >= CONV_W - 1
    if cached:
        past = cache_k.shape[2]
        tk = min(512, past)
        assert past % tk == 0 and past % CHUNK == 0 and l <= CHUNK and (past % GM_CHUNK == 0)
        nt = past // tk
        offs, cls = [FAR_OFFSET], []
        for j in range(nt):
            off = j * tk - past
            if _is_far(off, l, tk):
                cls.append(0)
            else:
                offs.append(off)
                cls.append(len(offs) - 1)
        bias_past = _bias_tiles(rel_bias, offs, l, tk)
        cls_tbl = jnp.asarray(np.asarray(cls, np.int32))
        bias_new = _bias_tiles(rel_bias, [0], l, l)
    else:
        past = 0
        tq = min(512, l)
        assert l % tq == 0 and tq % CHUNK == 0
        offs = [0]
        if l > tq:
            offs.append(-tq)
        if l > 2 * tq:
            assert _is_far(-2 * tq, tq, tq)
            offs.append(FAR_OFFSET)
        bias_self = _bias_tiles(rel_bias, offs, tq, tq, keys_major=True)

    h = x.reshape(m, d)
    new_k, new_v, new_ssm, new_conv, new_pool, new_gv = [], [], [], [], [], []
    for li, w in enumerate(layers):
        lam_init = 0.8 - 0.6 * math.exp(-0.3 * li)
        h = _ffn(h, w['g_ffn1'], w['ffn1_in'], w['ffn1_out'])
        proj = _mm(h, w['w_in'], g=w['g_mix'], out_dtype=F32, tm=1024, tn=1664, name="in_proj")
        proj = proj.reshape(b, l, P_W)

        ys, s_ssm = _ssd(proj, conv0[li] if cached else None, ssm0[li] if cached else None,
                         w['conv_w'], w['conv_b'], w['dt_bias'], w['a_log'], w['d_skip'], w['g_ssd'], q=128)
        ys = _pool(proj, pool0[li] if cached else None, past, w['w_pool'], w['pool_scale'], t=256,
                   ybuf=ys, slab=1)
        if cached:
            qn, kn, knb, vb = _qknorm(proj, w['g_q'], w['g_k'], t=512)
            ys = _attn_cached(qn, knb, vb, cache_k, cache_v, li, bias_past, cls_tbl, bias_new,
                              w['lam'], w['g_subln'], lam_init, tk=tk, ybuf=ys, slab=2)
        else:
            qt, kn, knb, vt = _qknorm_t(proj, w['g_q'], w['g_k'], t=tq)
            ys = _attn_self(qt, knb, vt, bias_self, w['lam'], w['g_subln'], lam_init, tq=tq, ybuf=ys, slab=2)
        ys, v_gm = _gmlp(proj, w['g_gv'], w['w_sp'], w['b_sp'], keep_vn=cached, ybuf=ys, slab=3)

        merged = _merge(h, w['g_mix'], ys.reshape(N_BRANCH, m, BRANCH_W), w['w_gate'], w['b_gate'], w['w_branch'])
        h = _mm(merged, w['w_out'], res=h, out_dtype=F32, tm=512, tn=1024, name="out_proj", w_resident=True)

        qx = _mm(h, w['w_xq'], g=w['g_x'], gh=w['g_xq'], out_dtype=BF16, tm=512, tn=512, name="xattn_q")
        o = _xattn(qx.reshape(b, l, X_H * X_D), mem_k[li].reshape(b, -1, X_H * X_D),
                   mem_v[li].reshape(b, -1, X_H * X_D), tq=512)
        h = _ffn(h, w['g_ffn2'], w['ffn2_in'], w['ffn2_out'], post_g=w['g_post'],
                 pre=(o.reshape(m, X_H * X_D), w['w_xo']))

        new_k.append(kn.reshape(b, l, ATT_KV, 2, ATT_D))
        new_v.append(proj[:, :, P_V:P_V + ATT_KV * ATT_VD].reshape(b, l, ATT_KV, ATT_VD))
        new_ssm.append(s_ssm)
        new_conv.append(proj[:, l - (CONV_W - 1):, P_XBC:P_XBC + XBC_W])
        new_pool.append(proj[:, l - POOL_BUF:, P_POOL:P_POOL + BRANCH_W])
        new_gv.append(v_gm)
    return (h.reshape(b, l, d), jnp.stack(new_k), jnp.stack(new_v), jnp.stack(new_ssm),
            jnp.stack(new_conv), jnp.stack(new_pool), jnp.stack(new_gv) if cached else None)


def kernel(x_prompt, x_sample, cache_attn_k, cache_attn_v, cache_mem_k, cache_mem_v, state_ssm, state_conv, state_pool, mem_prompt, g_ffn1, w_ffn1_in, w_ffn1_out, g_mix, w_in, conv_w, conv_b, dt_bias, a_log, d_skip, g_ssd, w_pool, pool_scale, g_q, g_k, lam, g_subln, rel_bias, g_gv, w_sp, b_sp, w_gate, b_gate, w_branch, w_out, g_x, w_xq, g_xq, g_mem, w_mk, w_mv, g_xk, w_xo, g_ffn2, w_ffn2_in, w_ffn2_out, g_post):
    p = dict(g_ffn1=g_ffn1, w_ffn1_in=w_ffn1_in, w_ffn1_out=w_ffn1_out, g_mix=g_mix, w_in=w_in,
             conv_w=conv_w, conv_b=conv_b, dt_bias=dt_bias, a_log=a_log, d_skip=d_skip, g_ssd=g_ssd,
             w_pool=w_pool, pool_scale=pool_scale, g_q=g_q, g_k=g_k, lam=lam, g_subln=g_subln,
             g_gv=g_gv, w_sp=w_sp, b_sp=b_sp, w_gate=w_gate, b_gate=b_gate, w_branch=w_branch,
             w_out=w_out, g_x=g_x, w_xq=w_xq, g_xq=g_xq, g_mem=g_mem, w_mk=w_mk, w_mv=w_mv, g_xk=g_xk,
             w_xo=w_xo, g_ffn2=g_ffn2, w_ffn2_in=w_ffn2_in, w_ffn2_out=w_ffn2_out, g_post=g_post)
    depth = w_in.shape[0]
    layers = _prep_weights(p, depth)

    bp, mlen, d = mem_prompt.shape
    mem2 = mem_prompt.reshape(bp * mlen, d)
    mks, mvs = [], []
    for w in layers:
        mk = _mm(mem2, w['w_mk'], g=w['g_mem'], gh=w['g_xk'], out_dtype=F32, tm=512, tn=512, name="mem_k")
        mv = _mm(mem2, w['w_mv'], g=w['g_mem'], out_dtype=F32, tm=512, tn=512, name="mem_v")
        mks.append(mk.reshape(bp, mlen, X_H, X_D))
        mvs.append(mv.reshape(bp, mlen, X_H, X_D))
    p_mem_k = jnp.stack(mks)
    p_mem_v = jnp.stack(mvs)

    y_prompt, p_attn_k, p_attn_v, p_ssm, p_conv, p_pool, _ = _run_trunk(
        x_prompt, None, None, p_mem_k, p_mem_v, None, None, None, layers, rel_bias)
    y_sample, s_attn_k, s_attn_v, s_ssm, s_conv, s_pool, s_gmlp_v = _run_trunk(
        x_sample, cache_attn_k, cache_attn_v, cache_mem_k, cache_mem_v, state_ssm, state_conv,
        state_pool, layers, rel_bias)

    return (y_prompt, y_sample, p_attn_k, p_attn_v, p_mem_k, p_mem_v, p_ssm, p_conv, p_pool,
            s_attn_k, s_attn_v, s_ssm, s_conv, s_pool, s_gmlp_v)
```

```python
import functools
import math

import jax
import jax.numpy as jnp
import numpy as np
from jax import lax
from jax.experimental import pallas as pl
from jax.experimental.pallas import tpu as pltpu

F32 = jnp.float32
BF16 = jnp.bfloat16
EPS = 1e-6
NEG = -1e30

VMEM_LIMIT_BYTES = 56 * 1024 * 1024
LANES = 128

BRANCH_W = 1024
SSM_P = 64
SSM_H = 16
SSM_G = 4
SSM_N = 128
SSM_D = SSM_H * SSM_P
CONV_W = 4
XBC_W = SSM_D + 2 * SSM_G * SSM_N
POOL_WINDOWS = (2, 4, 8, 16)
POOL_GW = BRANCH_W // len(POOL_WINDOWS)
POOL_BUF = max(POOL_WINDOWS) - 1
POOL_PAD = max(POOL_WINDOWS) // 2
ATT_H = 8
ATT_KV = 4
ATT_D = 64
ATT_VD = 128
CHUNK = 64
N_BUCKETS = 32
MAX_DIST = 128
GM_CHUNK = 128
GM_G = 4
GM_GW = BRANCH_W // GM_G
X_H = 4
X_D = 128
VT_ONES = 16

P_XBC = 0
P_Z = 2048
P_POOL = 3072
P_Q = 4096
P_GU = 5120
P_GV = 6144
P_K = 7168
P_V = 7680
P_DT = 8192
P_W = 8320


def _cparams(sem):
    return pltpu.CompilerParams(dimension_semantics=sem, vmem_limit_bytes=VMEM_LIMIT_BYTES)


N_BRANCH = 4


def _skip_ref(fn, pos):
    def wrapped(*refs):
        return fn(*refs[:pos], *refs[pos + 1:])
    return wrapped


def _branch_out(ybuf, slab, b, l, args, specs, kernel_fn, n_prefetch=0):
    shape = jax.ShapeDtypeStruct((N_BRANCH, b, l, BRANCH_W), BF16)
    if ybuf is None:
        return shape, kernel_fn, {}
    assert slab > 0 and ybuf.shape == shape.shape and ybuf.dtype == shape.dtype
    pos = n_prefetch + len(args)
    args.append(ybuf)
    specs.append(pl.BlockSpec(memory_space=pl.ANY))
    return shape, _skip_ref(kernel_fn, pos), {pos: 0}


def _rms(xf, g):
    ms = jnp.mean(xf * xf, axis=-1, keepdims=True)
    return xf * lax.rsqrt(ms + EPS) * g


def _silu(x):
    return x * jax.nn.sigmoid(x)


def _mm_kernel(*refs, norm, head_norm, residual):
    it = iter(refs)
    x_ref = next(it)
    g_ref = next(it) if norm else None
    w_ref = next(it)
    gh_ref = next(it) if head_norm else None
    res_ref = next(it) if residual else None
    o_ref = next(it)
    xn_ref = next(it) if norm else None

    if norm:
        @pl.when(pl.program_id(1) == 0)
        def _():
            xn_ref[...] = _rms(x_ref[...].astype(F32), g_ref[...]).astype(BF16)
        xb = xn_ref[...]
    else:
        xb = x_ref[...].astype(BF16)
    acc = jnp.dot(xb, w_ref[...], preferred_element_type=F32)
    if head_norm:
        parts = []
        for c in range(acc.shape[1] // LANES):
            parts.append(_rms(acc[:, c * LANES:(c + 1) * LANES], gh_ref[...]))
        acc = jnp.concatenate(parts, axis=-1)
    if residual:
        acc = res_ref[...] + acc
    o_ref[...] = acc.astype(o_ref.dtype)


def _mm(x, w, *, g=None, gh=None, res=None, out_dtype, tm, tn, name, w_resident=False):
    m, k = x.shape
    n = w.shape[1]
    tm = min(tm, m)
    tn = min(tn, n)
    assert m % tm == 0 and n % tn == 0
    norm, head_norm, residual = g is not None, gh is not None, res is not None
    assert not (norm and w_resident)
    if w_resident:
        grid = (n // tn, m // tm)
        ij = lambda a, b: (b, a)
    else:
        grid = (m // tm, n // tn)
        ij = lambda a, b: (a, b)
    args, specs = [x], [pl.BlockSpec((tm, k), lambda a, b: (ij(a, b)[0], 0))]
    if norm:
        args.append(g.reshape(1, k))
        specs.append(pl.BlockSpec((1, k), lambda a, b: (0, 0)))
    args.append(w)
    specs.append(pl.BlockSpec((k, tn), lambda a, b: (0, ij(a, b)[1])))
    if head_norm:
        args.append(gh.reshape(1, LANES))
        specs.append(pl.BlockSpec((1, LANES), lambda a, b: (0, 0)))
    if residual:
        args.append(res)
        specs.append(pl.BlockSpec((tm, tn), lambda a, b: ij(a, b)))
    scratch = [pltpu.VMEM((tm, k), BF16)] if norm else []
    return pl.pallas_call(
        functools.partial(_mm_kernel, norm=norm, head_norm=head_norm, residual=residual),
        out_shape=jax.ShapeDtypeStruct((m, n), out_dtype),
        grid=grid,
        in_specs=specs,
        out_specs=pl.BlockSpec((tm, tn), lambda a, b: ij(a, b)),
        scratch_shapes=scratch,
        compiler_params=_cparams(("parallel", "arbitrary")),
        name=name,
    )(*args)


def _ffn_kernel(*refs, post, pre):
    it = iter(refs)
    x_ref, g_ref, wg_ref, wu_ref, wo_ref = (next(it) for _ in range(5))
    gp_ref = next(it) if post else None
    a_ref, wa_ref = (next(it), next(it)) if pre else (None, None)
    o_ref, xn_ref, acc_ref = next(it), next(it), next(it)
    f = pl.program_id(1)

    @pl.when(f == 0)
    def _():
        x = x_ref[...]
        if pre:
            x = x + jnp.dot(a_ref[...], wa_ref[...], preferred_element_type=F32)
            o_ref[...] = x
        xn_ref[...] = _rms(x, g_ref[...]).astype(BF16)
        acc_ref[...] = jnp.zeros_like(acc_ref)

    xb = xn_ref[...]
    gate = jnp.dot(xb, wg_ref[...], preferred_element_type=F32)
    up = jnp.dot(xb, wu_ref[...], preferred_element_type=F32)
    mid = (_silu(gate) * up).astype(BF16)
    acc_ref[...] += jnp.dot(mid, wo_ref[...], preferred_element_type=F32)

    @pl.when(f == pl.num_programs(1) - 1)
    def _():
        h = (o_ref[...] if pre else x_ref[...]) + 0.5 * acc_ref[...]
        if post:
            h = _rms(h, gp_ref[...])
        o_ref[...] = h


def _ffn(x, g, w_in, w_out, *, post_g=None, pre=None, tm=512, tf=512):
    m, d = x.shape
    ff = w_out.shape[0]
    tm = min(tm, m)
    assert m % tm == 0 and ff % tf == 0
    nf = ff // tf
    post = post_g is not None
    args = [x, g.reshape(1, d), w_in, w_in, w_out]
    specs = [
        pl.BlockSpec((tm, d), lambda i, f: (i, 0)),
        pl.BlockSpec((1, d), lambda i, f: (0, 0)),
        pl.BlockSpec((d, tf), lambda i, f: (0, f)),
        pl.BlockSpec((d, tf), lambda i, f: (0, nf + f)),
        pl.BlockSpec((tf, d), lambda i, f: (f, 0)),
    ]
    if post:
        args.append(post_g.reshape(1, d))
        specs.append(pl.BlockSpec((1, d), lambda i, f: (0, 0)))
    if pre is not None:
        a, wa = pre
        ka = a.shape[1]
        args += [a, wa]
        specs += [pl.BlockSpec((tm, ka), lambda i, f: (i, 0)), pl.BlockSpec((ka, d), lambda i, f: (0, 0))]
    return pl.pallas_call(
        functools.partial(_ffn_kernel, post=post, pre=pre is not None),
        out_shape=jax.ShapeDtypeStruct((m, d), F32),
        grid=(m // tm, nf),
        in_specs=specs,
        out_specs=pl.BlockSpec((tm, d), lambda i, f: (i, 0)),
        scratch_shapes=[pltpu.VMEM((tm, d), BF16), pltpu.VMEM((tm, d), F32)],
        compiler_params=_cparams(("parallel", "arbitrary")),
        name="swiglu_ffn",
    )(*args)


def _merge_kernel(h_ref, g_ref, y_ref, wg_ref, bg_ref, wb_ref, o_ref, hn_ref, acc_ref):
    j = pl.program_id(1)
    br = pl.program_id(2)

    @pl.when((j == 0) & (br == 0))
    def _():
        hn_ref[...] = _rms(h_ref[...], g_ref[...]).astype(BF16)

    @pl.when(br == 0)
    def _():
        acc_ref[...] = jnp.zeros_like(acc_ref)

    gate = jax.nn.sigmoid(jnp.dot(hn_ref[...], wg_ref[...], preferred_element_type=F32) + bg_ref[...])
    acc_ref[...] += gate * jnp.dot(y_ref[...], wb_ref[...], preferred_element_type=F32)

    @pl.when(br == pl.num_programs(2) - 1)
    def _():
        o_ref[...] = acc_ref[...].astype(o_ref.dtype)


def _merge(h, g, ys, w_gate, b_gate, w_branch, *, tm=1024, tn=1024):
    m, d = h.shape
    nb, bw, _ = w_branch.shape
    tm = min(tm, m)
    assert m % tm == 0 and d % tn == 0 and ys.shape == (nb, m, bw)
    return pl.pallas_call(
        _merge_kernel,
        out_shape=jax.ShapeDtypeStruct((m, d), BF16),
        grid=(m // tm, d // tn, nb),
        in_specs=[
            pl.BlockSpec((tm, d), lambda i, j, b: (i, 0)),
            pl.BlockSpec((1, d), lambda i, j, b: (0, 0)),
            pl.BlockSpec((None, tm, bw), lambda i, j, b: (b, i, 0)),
            pl.BlockSpec((None, d, tn), lambda i, j, b: (b, 0, j)),
            pl.BlockSpec((None, 1, tn), lambda i, j, b: (b, 0, j)),
            pl.BlockSpec((None, bw, tn), lambda i, j, b: (b, 0, j)),
        ],
        out_specs=pl.BlockSpec((tm, tn), lambda i, j, b: (i, j)),
        scratch_shapes=[pltpu.VMEM((tm, d), BF16), pltpu.VMEM((tm, tn), F32)],
        compiler_params=_cparams(("parallel", "arbitrary", "arbitrary")),
        name="branch_merge",
    )(h, g.reshape(1, d), ys, w_gate, b_gate.reshape(nb, 1, d), w_branch)


def _ssd_kernel(*refs, q, has_state):
    it = iter(refs)
    xbc_ref, z_ref, dt_ref = next(it), next(it), next(it)
    conv0_ref = next(it) if has_state else None
    s0_ref = next(it) if has_state else None
    cw_ref, cb_ref, dtb_ref, alog_ref, dskip_ref, gssd_ref = (next(it) for _ in range(6))
    y_ref, s_ref, cbuf = next(it), next(it), next(it)
    hist = 8

    @pl.when(pl.program_id(1) == 0)
    def _():
        cbuf[0:hist, :] = jnp.zeros((hist, XBC_W), F32)
        if has_state:
            cbuf[hist - (CONV_W - 1):hist, :] = conv0_ref[0]
            s_ref[0] = s0_ref[0]
        else:
            s_ref[0] = jnp.zeros(s_ref.shape[1:], F32)

    cbuf[hist:hist + q, :] = xbc_ref[0]
    conv = cb_ref[...]
    for k in range(CONV_W):
        lo = hist - (CONV_W - 1) + k
        conv = conv + cbuf[lo:lo + q, :] * cw_ref[k:k + 1, :]
    cbuf[0:hist, :] = cbuf[q:q + hist, :]
    xc = _silu(conv)
    xs = xc[:, :SSM_D]
    bm = xc[:, SSM_D:SSM_D + SSM_G * SSM_N]
    cm = xc[:, SSM_D + SSM_G * SSM_N:]

    dtl = dt_ref[0] + dtb_ref[...]
    dt = jnp.maximum(dtl, 0.0) + jnp.log1p(jnp.exp(-jnp.abs(dtl)))
    a = dt * (-jnp.exp(alog_ref[...]))
    ri = lax.broadcasted_iota(jnp.int32, (q, q), 0)
    ci = lax.broadcasted_iota(jnp.int32, (q, q), 1)
    tril = ri >= ci
    cum = jnp.dot(tril.astype(F32), a, preferred_element_type=F32,
                  precision=lax.Precision.HIGHEST)
    cum_t = cum.T
    dt_t = dt.T
    cum_last = cum[q - 1:q, :]
    w_state = jnp.exp(cum_last - cum) * dt
    ecum = jnp.exp(cum)
    ecl = jnp.exp(cum_last)

    lane_lo = lax.broadcasted_iota(jnp.int32, (q, LANES), 1) < SSM_P
    row_lo_t = lax.broadcasted_iota(jnp.int32, (LANES, q), 0) < SSM_P
    row_lo_s = lax.broadcasted_iota(jnp.int32, (LANES, SSM_N), 0) < SSM_P
    dn_t = (((1,), (1,)), ((), ()))

    y_parts = []
    cb = None
    heads_per_group = SSM_H // SSM_G
    for pp in range(SSM_H // 2):
        grp = (2 * pp) // heads_per_group
        xp = xs[:, pp * LANES:(pp + 1) * LANES]
        bg = bm[:, grp * SSM_N:(grp + 1) * SSM_N]
        cg = cm[:, grp * SSM_N:(grp + 1) * SSM_N]
        cg_b = cg.astype(BF16)
        if (2 * pp) % heads_per_group == 0:
            cb = lax.dot_general(cg_b, bg.astype(BF16), dn_t, preferred_element_type=F32)
        xp_t = xp.T
        s_pair = s_ref[0, pp * LANES:(pp + 1) * LANES, :]
        y_pair = None
        s_new = None
        for hh in range(2):
            hd = 2 * pp + hh
            seg = cum[:, hd:hd + 1] - cum_t[hd:hd + 1, :]
            mat = jnp.exp(jnp.where(tril, seg, NEG)) * cb * dt_t[hd:hd + 1, :]
            sel = lane_lo if hh == 0 else jnp.logical_not(lane_lo)
            x_h = jnp.where(sel, xp, 0.0).astype(BF16)
            yd = jnp.dot(mat.astype(BF16), x_h, preferred_element_type=F32)
            y_pair = yd if y_pair is None else y_pair + yd
            sel_t = row_lo_t if hh == 0 else jnp.logical_not(row_lo_t)
            xt_h = jnp.where(sel_t, xp_t, 0.0).astype(BF16)
            bs = (bg * w_state[:, hd:hd + 1]).astype(BF16)
            sn = jnp.dot(xt_h, bs, preferred_element_type=F32)
            s_new = sn if s_new is None else s_new + sn
        y_off = lax.dot_general(cg_b, s_pair.astype(BF16), dn_t, preferred_element_type=F32)
        e_pair = jnp.where(lane_lo, ecum[:, 2 * pp:2 * pp + 1], ecum[:, 2 * pp + 1:2 * pp + 2])
        y_parts.append(y_pair + y_off * e_pair)
        dec = jnp.where(row_lo_s, ecl[:, 2 * pp:2 * pp + 1], ecl[:, 2 * pp + 1:2 * pp + 2])
        s_ref[0, pp * LANES:(pp + 1) * LANES, :] = dec * s_pair + s_new

    y = jnp.concatenate(y_parts, axis=-1)
    y = y + dskip_ref[...] * xs
    y = y * _silu(z_ref[0])
    gw = SSM_D // SSM_G
    outs = [_rms(y[:, gi * gw:(gi + 1) * gw], gssd_ref[:, gi * gw:(gi + 1) * gw]) for gi in range(SSM_G)]
    y_ref[0] = jnp.concatenate(outs, axis=-1).astype(y_ref.dtype)


def _ssd(proj, conv0, s0, cw, cb, dtb, alog, dskip, gssd, *, q, ybuf=None, slab=0):
    b, l, _ = proj.shape
    q = min(q, l)
    assert l % q == 0 and q >= 8
    has_state = s0 is not None
    args = [proj, proj, proj]
    specs = [
        pl.BlockSpec((1, q, XBC_W), lambda i, c: (i, c, P_XBC // XBC_W)),
        pl.BlockSpec((1, q, SSM_D), lambda i, c: (i, c, P_Z // SSM_D)),
        pl.BlockSpec((1, q, LANES), lambda i, c: (i, c, P_DT // LANES)),
    ]
    if has_state:
        args += [conv0, s0.reshape(b, SSM_H * SSM_P, SSM_N)]
        specs += [
            pl.BlockSpec((1, CONV_W - 1, XBC_W), lambda i, c: (i, 0, 0)),
            pl.BlockSpec((1, SSM_H * SSM_P, SSM_N), lambda i, c: (i, 0, 0)),
        ]
    args += [cw, cb, dtb, alog, dskip, gssd]
    specs += [
        pl.BlockSpec((CONV_W, XBC_W), lambda i, c: (0, 0)),
        pl.BlockSpec((1, XBC_W), lambda i, c: (0, 0)),
        pl.BlockSpec((1, LANES), lambda i, c: (0, 0)),
        pl.BlockSpec((1, LANES), lambda i, c: (0, 0)),
        pl.BlockSpec((1, SSM_D), lambda i, c: (0, 0)),
        pl.BlockSpec((1, SSM_D), lambda i, c: (0, 0)),
    ]
    yshape, kern, aliases = _branch_out(ybuf, slab, b, l, args, specs,
                                        functools.partial(_ssd_kernel, q=q, has_state=has_state))
    y, s = pl.pallas_call(
        kern,
        out_shape=(yshape, jax.ShapeDtypeStruct((b, SSM_H * SSM_P, SSM_N), F32)),
        grid=(b, l // q),
        in_specs=specs,
        out_specs=(pl.BlockSpec((None, 1, q, SSM_D), lambda i, c: (slab, i, c, 0)),
                   pl.BlockSpec((1, SSM_H * SSM_P, SSM_N), lambda i, c: (i, 0, 0))),
        scratch_shapes=[pltpu.VMEM((q + 8, XBC_W), F32)],
        input_output_aliases=aliases,
        compiler_params=_cparams(("parallel", "arbitrary")),
        name="ssd_mixer",
    )(*args)
    return y, s.reshape(b, SSM_H, SSM_P, SSM_N)


def _pool_kernel(*refs, t, pos0, has_state):
    it = iter(refs)
    xp_ref = next(it)
    p0_ref = next(it) if has_state else None
    w_ref, sc_ref, y_ref, buf, lvl = next(it), next(it), next(it), next(it), next(it)
    hist = POOL_BUF + 1
    pad = POOL_PAD
    base = pad + hist
    n = hist + t
    ti = pl.program_id(1)

    @pl.when(ti == 0)
    def _():
        buf[0:base, :] = jnp.zeros((base, BRANCH_W), F32)
        lvl[0:pad, :] = jnp.zeros((pad, BRANCH_W), F32)
        if has_state:
            buf[pad + 1:base, :] = p0_ref[0]

    xp = xp_ref[0]
    buf[base:base + t, :] = xp
    pos = pos0 + ti * t + lax.broadcasted_iota(jnp.int32, (t, POOL_GW), 0)
    outs = []
    for gi, w in enumerate(POOL_WINDOWS):
        cs = slice(gi * POOL_GW, (gi + 1) * POOL_GW)
        src, k = buf, 1
        while 2 * k < w:
            lvl[pad:pad + n, cs] = src[pad:pad + n, cs] + src[pad - k:pad - k + n, cs]
            src, k = lvl, 2 * k
        s = src[base:base + t, cs] + src[base - k:base - k + t, cs]
        cnt = jnp.minimum(w, pos + 1).astype(F32)
        pooled = s / cnt - xp[:, cs]
        y = jnp.dot(pooled.astype(BF16), w_ref[gi], preferred_element_type=F32)
        outs.append(y * sc_ref[:, cs])
    y_ref[0] = jnp.concatenate(outs, axis=-1).astype(y_ref.dtype)
    buf[pad:base, :] = buf[pad + t:base + t, :]


def _pool(proj, pool0, pos0, w_pool, scale, *, t, ybuf=None, slab=0):
    b, l, _ = proj.shape
    t = min(t, l)
    assert l % t == 0 and t >= POOL_BUF + 1
    has_state = pool0 is not None
    args = [proj]
    specs = [pl.BlockSpec((1, t, BRANCH_W), lambda i, c: (i, c, P_POOL // BRANCH_W))]
    if has_state:
        args.append(pool0)
        specs.append(pl.BlockSpec((1, POOL_BUF, BRANCH_W), lambda i, c: (i, 0, 0)))
    args += [w_pool, scale]
    specs += [
        pl.BlockSpec((len(POOL_WINDOWS), POOL_GW, POOL_GW), lambda i, c: (0, 0, 0)),
        pl.BlockSpec((1, BRANCH_W), lambda i, c: (0, 0)),
    ]
    yshape, kern, aliases = _branch_out(
        ybuf, slab, b, l, args, specs, functools.partial(_pool_kernel, t=t, pos0=pos0, has_state=has_state))
    return pl.pallas_call(
        kern,
        out_shape=yshape,
        grid=(b, l // t),
        in_specs=specs,
        out_specs=pl.BlockSpec((None, 1, t, BRANCH_W), lambda i, c: (slab, i, c, 0)),
        scratch_shapes=[pltpu.VMEM((t + POOL_BUF + 1 + POOL_PAD, BRANCH_W), F32),
                        pltpu.VMEM((t + POOL_BUF + 1 + POOL_PAD, BRANCH_W), F32)],
        input_output_aliases=aliases,
        compiler_params=_cparams(("parallel", "arbitrary")),
        name="pool_mixer",
    )(*args)


def _gmlp_kernel(u_ref, v_ref, g_ref, w_ref, b_ref, y_ref, *rest, cl, nsub, keep_vn):
    tril = lax.broadcasted_iota(jnp.int32, (cl, cl), 0) >= lax.broadcasted_iota(jnp.int32, (cl, cl), 1)
    ws = [jnp.where(tril, w_ref[gi], 0.0).astype(BF16) for gi in range(GM_G)]
    for sub in range(nsub):
        rows = slice(sub * cl, (sub + 1) * cl)
        u = jax.nn.gelu(u_ref[0, rows, :], approximate=True)
        vn = _rms(jax.nn.gelu(v_ref[0, rows, :], approximate=True), g_ref[...])
        if keep_vn:
            rest[0][0, rows, :] = vn
        outs = []
        for gi in range(GM_G):
            s = jnp.dot(ws[gi], vn[:, gi * GM_GW:(gi + 1) * GM_GW].astype(BF16), preferred_element_type=F32)
            s = s + b_ref[:, gi:gi + 1]
            outs.append(u[:, gi * GM_GW:(gi + 1) * GM_GW] * s)
        y_ref[0, rows, :] = jnp.concatenate(outs, axis=-1).astype(y_ref.dtype)


def _gmlp(proj, g_gv, w_sp, b_sp, *, keep_vn, ybuf=None, slab=0):
    b, l, _ = proj.shape
    cl = min(GM_CHUNK, l)
    assert l % cl == 0
    nsub = math.gcd(l // cl, 4)
    t = cl * nsub
    w = w_sp[:, :cl, :cl]
    bt = b_sp[:, :cl].T
    args = [proj, proj, g_gv, w, bt]
    specs = [
        pl.BlockSpec((1, t, BRANCH_W), lambda i, c: (i, c, P_GU // BRANCH_W)),
        pl.BlockSpec((1, t, BRANCH_W), lambda i, c: (i, c, P_GV // BRANCH_W)),
        pl.BlockSpec((1, BRANCH_W), lambda i, c: (0, 0)),
        pl.BlockSpec((GM_G, cl, cl), lambda i, c: (0, 0, 0)),
        pl.BlockSpec((cl, GM_G), lambda i, c: (0, 0)),
    ]
    yshape, kern, aliases = _branch_out(
        ybuf, slab, b, l, args, specs, functools.partial(_gmlp_kernel, cl=cl, nsub=nsub, keep_vn=keep_vn))
    out_shape = [yshape]
    out_specs = [pl.BlockSpec((None, 1, t, BRANCH_W), lambda i, c: (slab, i, c, 0))]
    if keep_vn:
        out_shape.append(jax.ShapeDtypeStruct((b, l, BRANCH_W), F32))
        out_specs.append(pl.BlockSpec((1, t, BRANCH_W), lambda i, c: (i, c, 0)))
    outs = pl.pallas_call(
        kern,
        out_shape=tuple(out_shape),
        grid=(b, l // t),
        in_specs=specs,
        out_specs=tuple(out_specs),
        input_output_aliases=aliases,
        compiler_params=_cparams(("parallel", "parallel")),
        name="gmlp_mixer",
    )(*args)
    return outs[0], (outs[1] if keep_vn else None)


def _norm_halves(x, g):
    lo = lax.broadcasted_iota(jnp.int32, x.shape, 1) < ATT_D
    xx = x * x
    s_lo = jnp.sum(jnp.where(lo, xx, 0.0), axis=-1, keepdims=True)
    s_hi = jnp.sum(jnp.where(lo, 0.0, xx), axis=-1, keepdims=True)
    ms = jnp.where(lo, s_lo, s_hi) * (1.0 / ATT_D)
    return x * lax.rsqrt(ms + EPS) * g


def _qknorm_kernel(q_ref, k_ref, v_ref, gq_ref, gk_ref, qn_ref, kn_ref, knb_ref, vb_ref):
    scale = ATT_D ** -0.5
    q = q_ref[0]
    qs = [_norm_halves(q[:, c * LANES:(c + 1) * LANES], gq_ref[...]) * scale
          for c in range(q.shape[1] // LANES)]
    qn_ref[0] = jnp.concatenate(qs, axis=-1).astype(qn_ref.dtype)
    k = k_ref[0]
    ks = [_norm_halves(k[:, c * LANES:(c + 1) * LANES], gk_ref[...]) for c in range(k.shape[1] // LANES)]
    kn = jnp.concatenate(ks, axis=-1)
    kn_ref[0] = kn
    knb_ref[0] = kn.astype(BF16)
    vb_ref[0] = v_ref[0].astype(BF16)


def _qknorm(proj, g_q, g_k, *, t):
    b, l, _ = proj.shape
    t = min(t, l)
    assert l % t == 0
    qw, kw = ATT_H * 2 * ATT_D, ATT_KV * 2 * ATT_D
    gq = jnp.tile(g_q, 2).reshape(1, LANES)
    gk = jnp.tile(g_k, 2).reshape(1, LANES)
    return pl.pallas_call(
        _qknorm_kernel,
        out_shape=(jax.ShapeDtypeStruct((b, l, qw), BF16), jax.ShapeDtypeStruct((b, l, kw), F32),
                   jax.ShapeDtypeStruct((b, l, kw), BF16), jax.ShapeDtypeStruct((b, l, kw), BF16)),
        grid=(b, l // t),
        in_specs=[
            pl.BlockSpec((1, t, qw), lambda i, c: (i, c, P_Q // qw)),
            pl.BlockSpec((1, t, kw), lambda i, c: (i, c, P_K // kw)),
            pl.BlockSpec((1, t, kw), lambda i, c: (i, c, P_V // kw)),
            pl.BlockSpec((1, LANES), lambda i, c: (0, 0)),
            pl.BlockSpec((1, LANES), lambda i, c: (0, 0)),
        ],
        out_specs=(pl.BlockSpec((1, t, qw), lambda i, c: (i, c, 0)),
                   pl.BlockSpec((1, t, kw), lambda i, c: (i, c, 0)),
                   pl.BlockSpec((1, t, kw), lambda i, c: (i, c, 0)),
                   pl.BlockSpec((1, t, kw), lambda i, c: (i, c, 0))),
        compiler_params=_cparams(("parallel", "parallel")),
        name="qk_norm",
    )(proj, proj, proj, gq, gk)


def _bias_kernel(off_ref, tbl_ref, o_ref, *, tq, tk, keys_major):
    c = pl.program_id(0)
    h = pl.program_id(1)
    off = off_ref[c]
    shape = (tk, tq) if keys_major else (tq, tk)
    row = lax.broadcasted_iota(jnp.int32, shape, 1 if keys_major else 0)
    col = lax.broadcasted_iota(jnp.int32, shape, 0 if keys_major else 1)
    rel = off + col - row
    nb = N_BUCKETS // 2
    max_exact = nb // 2
    n = jnp.abs(rel)
    large = max_exact + (jnp.log(jnp.maximum(n, 1).astype(F32) / max_exact)
                         / math.log(MAX_DIST / max_exact) * (nb - max_exact)).astype(jnp.int32)
    large = jnp.minimum(large, nb - 1)
    bucket = jnp.where(rel > 0, nb, 0) + jnp.where(n < max_exact, n, large)
    val = jnp.zeros(shape, F32)
    for bk in range(N_BUCKETS):
        val = jnp.where(bucket == bk, tbl_ref[bk * ATT_H + h], val)
    visible = jnp.right_shift(off + col, 6) <= jnp.right_shift(row, 6)
    o_ref[0, 0] = jnp.where(visible, val, NEG)


def _bias_tiles(rel_bias, offsets, tq, tk, keys_major=False):
    assert CHUNK == 64
    offs = jnp.asarray(np.asarray(offsets, np.int32))
    n_cls = len(offsets)
    shape = (tk, tq) if keys_major else (tq, tk)
    return pl.pallas_call(
        functools.partial(_bias_kernel, tq=tq, tk=tk, keys_major=keys_major),
        out_shape=jax.ShapeDtypeStruct((n_cls, ATT_H) + shape, F32),
        grid=(n_cls, ATT_H),
        in_specs=[pl.BlockSpec(memory_space=pltpu.SMEM), pl.BlockSpec(memory_space=pltpu.SMEM)],
        out_specs=pl.BlockSpec((1, 1) + shape, lambda c, h: (c, h, 0, 0)),
        compiler_params=_cparams(("parallel", "parallel")),
        name="rel_bias_tiles",
    )(offs, rel_bias.reshape(-1))


FAR_OFFSET = -(1 << 20)


def _is_far(off, tq, tk):
    nb = N_BUCKETS // 2
    max_rel = off + tk - 1
    if max_rel >= 0:
        return False
    n_min = -max_rel
    sat = (nb // 2) * (MAX_DIST / (nb // 2)) ** ((nb - 1 - nb // 2) / (nb - nb // 2))
    return n_min >= math.ceil(sat) + 1


def _q4(q2):
    lo = lax.broadcasted_iota(jnp.int32, (q2.shape[0], LANES), 1) < ATT_D
    parts = []
    for r in range(2):
        qr = q2[:, r * LANES:(r + 1) * LANES]
        parts.append(jnp.where(lo, qr, jnp.zeros_like(qr)))
        parts.append(jnp.where(lo, jnp.zeros_like(qr), qr))
    return jnp.concatenate(parts, axis=0)


def _osm_softmax(s, b0, b1, m_ref, l_ref):
    s = s + jnp.concatenate([b0, b0, b1, b1], axis=0)
    m_prev = m_ref[...]
    m_new = jnp.maximum(m_prev, jnp.max(s, axis=-1, keepdims=True))
    alpha = jnp.exp(m_prev - m_new)
    p = jnp.exp(s - m_new[:, 0:1])
    l_ref[...] = alpha * l_ref[...] + jnp.sum(p, axis=-1, keepdims=True)
    m_ref[...] = m_new
    return p.astype(BF16), alpha


def _osm_finish(l_ref, acc_ref, lam_ref, gs_ref, lam_init, tq):
    lp = lam_ref[...]
    lam = (jnp.exp(jnp.sum(lp[0:1] * lp[1:2], axis=-1, keepdims=True))
           - jnp.exp(jnp.sum(lp[2:3] * lp[3:4], axis=-1, keepdims=True)) + lam_init)
    o = acc_ref[...] / l_ref[...]
    outs = []
    for r in range(2):
        d = o[(2 * r) * tq:(2 * r + 1) * tq] - lam * o[(2 * r + 1) * tq:(2 * r + 2) * tq]
        outs.append(_rms(d, gs_ref[...]) * (1.0 - lam_init))
    return jnp.concatenate(outs, axis=-1)


def _qknorm_t_kernel(q_ref, k_ref, v_ref, gq_ref, gk_ref, qt_ref, kn_ref, knb_ref, vt_ref):
    scale = ATT_D ** -0.5
    q = q_ref[0]
    t = q.shape[0]
    lo = lax.broadcasted_iota(jnp.int32, (LANES, t), 0) < ATT_D
    for kv in range(ATT_KV):
        cols = []
        for r in range(2):
            c = kv * 2 + r
            qr = q[:, c * LANES:(c + 1) * LANES].T
            xx = qr * qr
            ms = jnp.where(lo, jnp.sum(xx[:ATT_D], axis=0, keepdims=True),
                           jnp.sum(xx[ATT_D:], axis=0, keepdims=True)) * (1.0 / ATT_D)
            qr = qr * lax.rsqrt(ms + EPS) * gq_ref[...] * scale
            cols.append(jnp.where(lo, qr, 0.0))
            cols.append(jnp.where(lo, 0.0, qr))
        qt_ref[0, kv] = jnp.concatenate(cols, axis=1).astype(qt_ref.dtype)
    k = k_ref[0]
    ks = [_norm_halves(k[:, c * LANES:(c + 1) * LANES], gk_ref[...]) for c in range(k.shape[1] // LANES)]
    kn = jnp.concatenate(ks, axis=-1)
    kn_ref[0] = kn
    knb_ref[0] = kn.astype(BF16)
    v = v_ref[0]
    for kv in range(ATT_KV):
        vt_ref[0, kv, 0] = jnp.concatenate(
            [v[:, kv * LANES:(kv + 1) * LANES].T, jnp.ones((VT_ONES, t), F32)], axis=0).astype(vt_ref.dtype)


def _qknorm_t(proj, g_q, g_k, *, t):
    b, l, _ = proj.shape
    assert l % t == 0
    qw, kw = ATT_H * 2 * ATT_D, ATT_KV * 2 * ATT_D
    gq = jnp.tile(g_q, 2).reshape(LANES, 1)
    gk = jnp.tile(g_k, 2).reshape(1, LANES)
    return pl.pallas_call(
        _qknorm_t_kernel,
        out_shape=(jax.ShapeDtypeStruct((b, ATT_KV, LANES, 4 * l), BF16),
                   jax.ShapeDtypeStruct((b, l, kw), F32),
                   jax.ShapeDtypeStruct((b, l, kw), BF16),
                   jax.ShapeDtypeStruct((b, ATT_KV, l // t, ATT_VD + VT_ONES, t), BF16)),
        grid=(b, l // t),
        in_specs=[
            pl.BlockSpec((1, t, qw), lambda i, c: (i, c, P_Q // qw)),
            pl.BlockSpec((1, t, kw), lambda i, c: (i, c, P_K // kw)),
            pl.BlockSpec((1, t, kw), lambda i, c: (i, c, P_V // kw)),
            pl.BlockSpec((LANES, 1), lambda i, c: (0, 0)),
            pl.BlockSpec((1, LANES), lambda i, c: (0, 0)),
        ],
        out_specs=(pl.BlockSpec((1, ATT_KV, LANES, 4 * t), lambda i, c: (i, 0, 0, c)),
                   pl.BlockSpec((1, t, kw), lambda i, c: (i, c, 0)),
                   pl.BlockSpec((1, t, kw), lambda i, c: (i, c, 0)),
                   pl.BlockSpec((1, ATT_KV, 1, ATT_VD + VT_ONES, t), lambda i, c: (i, 0, c, 0, 0))),
        compiler_params=_cparams(("parallel", "parallel")),
        name="qk_norm_t",
    )(proj, proj, proj, gq, gk)


def _attn_self_kernel(qt_ref, k_ref, vt_ref, bias_ref, lam_ref, gs_ref, o_ref, m_ref, acc_ref,
                      *, tq, tk, lam_init, n_cls, group):
    i = pl.program_id(2)
    m_ref[...] = jnp.full(m_ref.shape, NEG, F32)
    acc_ref[...] = jnp.zeros(acc_ref.shape, F32)
    sub = LANES
    assert tq == tk and tq % sub == 0 and sub % CHUNK == 0

    def step(j, mode):
        kt = k_ref[0, pl.ds(pl.multiple_of(j * tk, tk), tk), :]
        vt = vt_ref[0, 0, j]
        cls = jnp.minimum(i - j, n_cls - 1)
        for c0 in range(0, 4, group):
            update(kt, vt, cls, mode, range(c0, c0 + group))

    def softmax_block(s, m_prev, crow):
        m_cur = jnp.max(s, axis=0, keepdims=True)
        if crow is None:
            m_new = jnp.maximum(m_prev, m_cur)
            return jnp.exp(s - m_new).astype(BF16), m_new
        m_new = jnp.maximum(m_prev, m_cur + crow)
        return jnp.exp(s - (m_new - crow)).astype(BF16), m_new

    def update(kt, vt, cls, mode, combos):
        cols = {c: slice(c * tq, (c + 1) * tq) for c in combos}
        ss = {c: jnp.dot(kt, qt_ref[0, 0, :, cs], preferred_element_type=F32) for c, cs in cols.items()}
        ps, alphas = {}, {}
        for c, cs in cols.items():
            s = ss[c]
            m_prev = m_ref[:, cs]
            if mode == "far":
                p, m_new = softmax_block(s, m_prev, bias_ref[cls, c // 2, 0:1, :])
            elif mode == "near":
                p, m_new = softmax_block(s + bias_ref[cls, c // 2], m_prev, None)
            else:
                pb, mb = [], []
                for qb in range(tq // sub):
                    rows, lanes = (qb + 1) * sub, slice(qb * sub, (qb + 1) * sub)
                    p_b, m_b = softmax_block(s[:rows, lanes] + bias_ref[cls, c // 2, :rows, lanes],
                                             m_prev[:, lanes], None)
                    if rows < tk:
                        p_b = jnp.concatenate([p_b, jnp.zeros((tk - rows, sub), BF16)], axis=0)
                    pb.append(p_b)
                    mb.append(m_b)
                p, m_new = jnp.concatenate(pb, axis=1), jnp.concatenate(mb, axis=1)
            alphas[c] = jnp.exp(m_prev - m_new)
            m_ref[:, cs] = m_new
            ps[c] = p
        for c, cs in cols.items():
            acc_ref[:, cs] = alphas[c] * acc_ref[:, cs] + jnp.dot(vt, ps[c], preferred_element_type=F32)

    n_far = jnp.maximum(i - (n_cls - 2), 0) if n_cls == 3 else 0

    def body(mode):
        def run(j, carry):
            step(j, mode)
            return carry
        return run

    if n_cls == 3:
        lax.fori_loop(0, n_far, body("far"), 0)
    lax.fori_loop(n_far, i, body("near"), 0)
    step(i, "diag")

    lp = lam_ref[...]
    lam = (jnp.exp(jnp.sum(lp[0:1] * lp[1:2], axis=-1, keepdims=True))
           - jnp.exp(jnp.sum(lp[2:3] * lp[3:4], axis=-1, keepdims=True)) + lam_init)
    o = acc_ref[0:ATT_VD, :] / acc_ref[ATT_VD:ATT_VD + 1, :]
    outs = []
    for r in range(2):
        d = o[:, (2 * r) * tq:(2 * r + 1) * tq] - lam * o[:, (2 * r + 1) * tq:(2 * r + 2) * tq]
        ms = jnp.mean(d * d, axis=0, keepdims=True)
        y = d * lax.rsqrt(ms + EPS) * gs_ref[...] * (1.0 - lam_init)
        outs.append(y.T)
    o_ref[0] = jnp.concatenate(outs, axis=-1).astype(o_ref.dtype)


def _attn_self(qt, knb, vt, bias, lam_p, g_subln, lam_init, *, tq, ybuf=None, slab=0):
    b, l, _ = knb.shape
    n_cls = bias.shape[0]
    nk = l // tq
    hw = 2 * ATT_VD
    assert ATT_H * ATT_VD == BRANCH_W
    args = [qt, knb, vt, bias, lam_p, g_subln.reshape(ATT_VD, 1)]
    specs = [
        pl.BlockSpec((1, 1, LANES, 4 * tq), lambda bi, kv, i: (bi, kv, 0, i)),
        pl.BlockSpec((1, l, LANES), lambda bi, kv, i: (bi, 0, kv)),
        pl.BlockSpec((1, 1, nk, ATT_VD + VT_ONES, tq), lambda bi, kv, i: (bi, kv, 0, 0, 0)),
        pl.BlockSpec((n_cls, 2, tq, tq), lambda bi, kv, i: (0, kv, 0, 0)),
        pl.BlockSpec((4, ATT_D), lambda bi, kv, i: (0, 0)),
        pl.BlockSpec((ATT_VD, 1), lambda bi, kv, i: (0, 0)),
    ]
    yshape, kern, aliases = _branch_out(
        ybuf, slab, b, l, args, specs,
        functools.partial(_attn_self_kernel, tq=tq, tk=tq, lam_init=lam_init, n_cls=n_cls, group=4))
    return pl.pallas_call(
        kern,
        out_shape=yshape,
        grid=(b, ATT_KV, l // tq),
        in_specs=specs,
        out_specs=pl.BlockSpec((None, 1, tq, hw), lambda bi, kv, i: (slab, bi, i, kv)),
        scratch_shapes=[pltpu.VMEM((1, 4 * tq), F32), pltpu.VMEM((ATT_VD + VT_ONES, 4 * tq), F32)],
        input_output_aliases=aliases,
        compiler_params=_cparams(("parallel", "parallel", "arbitrary")),
        name="diff_attn_self",
    )(*args)


def _attn_cached_kernel(cls_ref, q_ref, ck_ref, cv_ref, kn_ref, vn_ref, bp_ref, bn_ref, lam_ref, gs_ref,
                        o_ref, m_ref, l_ref, acc_ref, *, tq, tk, nt, lam_init):
    del cls_ref
    j = pl.program_id(1)

    @pl.when(j == 0)
    def _():
        m_ref[...] = jnp.full(m_ref.shape, NEG, F32)
        l_ref[...] = jnp.zeros(l_ref.shape, F32)
        acc_ref[...] = jnp.zeros(acc_ref.shape, F32)

    dn_t = (((1,), (1,)), ((), ()))

    def update(k_all, v_of, bias_at):
        kvs = range(ATT_KV)
        ss = [lax.dot_general(_q4(q_ref[0, :, kv * 2 * LANES:(kv + 1) * 2 * LANES]),
                              k_all[:, kv * LANES:(kv + 1) * LANES].astype(BF16), dn_t,
                              preferred_element_type=F32) for kv in kvs]
        pa = [_osm_softmax(ss[kv], bias_at(2 * kv), bias_at(2 * kv + 1), m_ref.at[kv], l_ref.at[kv])
              for kv in kvs]
        for kv in kvs:
            p, alpha = pa[kv]
            acc_ref[kv] = alpha * acc_ref[kv] + jnp.dot(p, v_of(kv).astype(BF16), preferred_element_type=F32)

    @pl.when(j < nt)
    def _():
        update(ck_ref[...], lambda kv: cv_ref[pl.ds(kv, tk, stride=ATT_KV), :], lambda h: bp_ref[0, h])

    @pl.when(j == nt)
    def _():
        update(kn_ref[0], lambda kv: vn_ref[0, :, kv * LANES:(kv + 1) * LANES], lambda h: bn_ref[0, h])
        outs = [_osm_finish(l_ref.at[kv], acc_ref.at[kv], lam_ref, gs_ref, lam_init, tq)
                for kv in range(ATT_KV)]
        o_ref[0] = jnp.concatenate(outs, axis=-1).astype(o_ref.dtype)


def _attn_cached(qn, knb, vb, cache_k, cache_v, layer, bias_past, cls_tbl, bias_new, lam_p, g_subln,
                 lam_init, *, tk, ybuf=None, slab=0):
    b, l, _ = qn.shape
    past = cache_k.shape[2]
    nt = past // tk
    kw = ATT_KV * 2 * ATT_D
    ck = cache_k.reshape(cache_k.shape[0], b, past, kw)
    cv = cache_v.reshape(cache_v.shape[0], b, past * ATT_KV, ATT_VD)
    args = [qn, ck, cv, knb, vb, bias_past, bias_new, lam_p, g_subln.reshape(1, ATT_VD)]
    specs = [
        pl.BlockSpec((1, l, ATT_H * 2 * ATT_D), lambda bi, j, cls: (bi, 0, 0)),
        pl.BlockSpec((None, None, tk, kw), lambda bi, j, cls: (layer, bi, jnp.minimum(j, nt - 1), 0)),
        pl.BlockSpec((None, None, tk * ATT_KV, ATT_VD),
                     lambda bi, j, cls: (layer, bi, jnp.minimum(j, nt - 1), 0)),
        pl.BlockSpec((1, l, kw), lambda bi, j, cls: (bi, 0, 0)),
        pl.BlockSpec((1, l, kw), lambda bi, j, cls: (bi, 0, 0)),
        pl.BlockSpec((1, ATT_H, l, tk), lambda bi, j, cls: (cls[jnp.minimum(j, nt - 1)], 0, 0, 0)),
        pl.BlockSpec((1, ATT_H, l, l), lambda bi, j, cls: (0, 0, 0, 0)),
        pl.BlockSpec((4, ATT_D), lambda bi, j, cls: (0, 0)),
        pl.BlockSpec((1, ATT_VD), lambda bi, j, cls: (0, 0)),
    ]
    yshape, kern, aliases = _branch_out(
        ybuf, slab, b, l, args, specs,
        functools.partial(_attn_cached_kernel, tq=l, tk=tk, nt=nt, lam_init=lam_init), n_prefetch=1)
    grid_spec = pltpu.PrefetchScalarGridSpec(
        num_scalar_prefetch=1,
        grid=(b, nt + 1),
        in_specs=specs,
        out_specs=pl.BlockSpec((None, 1, l, ATT_H * ATT_VD), lambda bi, j, cls: (slab, bi, 0, 0)),
        scratch_shapes=[pltpu.VMEM((ATT_KV, 4 * l, LANES), F32), pltpu.VMEM((ATT_KV, 4 * l, LANES), F32),
                        pltpu.VMEM((ATT_KV, 4 * l, ATT_VD), F32)],
    )
    return pl.pallas_call(
        kern,
        out_shape=yshape,
        grid_spec=grid_spec,
        input_output_aliases=aliases,
        compiler_params=_cparams(("parallel", "arbitrary")),
        name="diff_attn_cached",
    )(cls_tbl, *args)


def _xattn_kernel(q_ref, mk_ref, mv_ref, o_ref):
    dn_t = (((1,), (1,)), ((), ()))
    outs = []
    for h in range(X_H):
        sl = slice(h * X_D, (h + 1) * X_D)
        s = lax.dot_general(q_ref[0, :, sl], mk_ref[0, :, sl].astype(BF16), dn_t,
                            preferred_element_type=F32) * (X_D ** -0.5)
        s = s - jnp.max(s, axis=-1, keepdims=True)
        p = jnp.exp(s)
        p = p / jnp.sum(p, axis=-1, keepdims=True)
        outs.append(jnp.dot(p.astype(BF16), mv_ref[0, :, sl].astype(BF16), preferred_element_type=F32))
    o_ref[0] = jnp.concatenate(outs, axis=-1).astype(o_ref.dtype)


def _xattn(q, mk, mv, *, tq):
    b, l, w = q.shape
    mlen = mk.shape[1]
    tq = min(tq, l)
    assert l % tq == 0
    return pl.pallas_call(
        _xattn_kernel,
        out_shape=jax.ShapeDtypeStruct((b, l, w), BF16),
        grid=(b, l // tq),
        in_specs=[
            pl.BlockSpec((1, tq, w), lambda i, c: (i, c, 0)),
            pl.BlockSpec((1, mlen, w), lambda i, c: (i, 0, 0)),
            pl.BlockSpec((1, mlen, w), lambda i, c: (i, 0, 0)),
        ],
        out_specs=pl.BlockSpec((1, tq, w), lambda i, c: (i, c, 0)),
        compiler_params=_cparams(("parallel", "parallel")),
        name="mem_cross_attn",
    )(q, mk, mv)


def _prep_weights(p, depth):
    layers = []
    off_z, off_xbc = 0, SSM_D
    off_dt = off_xbc + XBC_W
    off_pool = off_dt + SSM_H
    off_q = off_pool + BRANCH_W
    off_k = off_q + ATT_H * 2 * ATT_D
    off_v = off_k + ATT_KV * 2 * ATT_D
    off_gu = off_v + ATT_KV * ATT_VD
    off_gv = off_gu + BRANCH_W
    in_w = off_gv + BRANCH_W
    for l in range(depth):
        w = p['w_in'][l]
        d = w.shape[0]
        w_in = jnp.concatenate([
            w[:, off_xbc:off_dt], w[:, off_z:off_xbc], w[:, off_pool:off_q], w[:, off_q:off_k],
            w[:, off_gu:off_gv], w[:, off_gv:in_w], w[:, off_k:off_v], w[:, off_v:off_gu],
            w[:, off_dt:off_pool], jnp.zeros((d, LANES - SSM_H), w.dtype)], axis=1).astype(BF16)
        pad = lambda v: jnp.pad(v, (0, LANES - SSM_H)).reshape(1, LANES)
        layers.append(dict(
            g_ffn1=p['g_ffn1'][l], ffn1_in=p['w_ffn1_in'][l].astype(BF16), ffn1_out=p['w_ffn1_out'][l].astype(BF16),
            g_mix=p['g_mix'][l], w_in=w_in,
            conv_w=p['conv_w'][l], conv_b=p['conv_b'][l].reshape(1, XBC_W),
            dt_bias=pad(p['dt_bias'][l]), a_log=pad(p['a_log'][l]),
            d_skip=jnp.repeat(p['d_skip'][l], SSM_P).reshape(1, SSM_D), g_ssd=p['g_ssd'][l].reshape(1, SSM_D),
            w_pool=p['w_pool'][l].astype(BF16), pool_scale=p['pool_scale'][l].reshape(1, BRANCH_W),
            g_q=p['g_q'][l], g_k=p['g_k'][l], lam=p['lam'][l], g_subln=p['g_subln'][l],
            g_gv=p['g_gv'][l].reshape(1, BRANCH_W), w_sp=p['w_sp'][l], b_sp=p['b_sp'][l],
            w_gate=p['w_gate'][l].astype(BF16), b_gate=p['b_gate'][l], w_branch=p['w_branch'][l].astype(BF16),
            w_out=p['w_out'][l].astype(BF16),
            g_x=p['g_x'][l], w_xq=p['w_xq'][l].astype(BF16), g_xq=p['g_xq'][l], w_xo=p['w_xo'][l].astype(BF16),
            g_ffn2=p['g_ffn2'][l], ffn2_in=p['w_ffn2_in'][l].astype(BF16), ffn2_out=p['w_ffn2_out'][l].astype(BF16),
            g_post=p['g_post'][l],
            g_mem=p['g_mem'][l], w_mk=p['w_mk'][l].astype(BF16), w_mv=p['w_mv'][l].astype(BF16), g_xk=p['g_xk'][l],
        ))
    return layers


def _run_trunk(x, cache_k, cache_v, mem_k, mem_v, ssm0, conv0, pool0, layers, rel_bias):
    b, l, d = x.shape
    m = b * l
    depth = len(layers)
    cached = cache_k is not None
    assert l >= POOL_BUF and l >= CONV_W - 1
    if cached:
        past = cache_k.shape[2]
        tk = min(1024, past)
        assert past % tk == 0 and past % CHUNK == 0 and l <= CHUNK and (past % GM_CHUNK == 0)
        nt = past // tk
        offs, cls = [FAR_OFFSET], []
        for j in range(nt):
            off = j * tk - past
            if _is_far(off, l, tk):
                cls.append(0)
            else:
                offs.append(off)
                cls.append(len(offs) - 1)
        bias_past = _bias_tiles(rel_bias, offs, l, tk)
        cls_tbl = jnp.asarray(np.asarray(cls, np.int32))
        bias_new = _bias_tiles(rel_bias, [0], l, l)
    else:
        past = 0
        tq = min(512, l)
        assert l % tq == 0 and tq % CHUNK == 0
        offs = [0]
        if l > tq:
            offs.append(-tq)
        if l > 2 * tq:
            assert _is_far(-2 * tq, tq, tq)
            offs.append(FAR_OFFSET)
        bias_self = _bias_tiles(rel_bias, offs, tq, tq, keys_major=True)

    h = x.reshape(m, d)
    new_k, new_v, new_ssm, new_conv, new_pool, new_gv = [], [], [], [], [], []
    for li, w in enumerate(layers):
        lam_init = 0.8 - 0.6 * math.exp(-0.3 * li)
        h = _ffn(h, w['g_ffn1'], w['ffn1_in'], w['ffn1_out'])
        proj = _mm(h, w['w_in'], g=w['g_mix'], out_dtype=F32, tm=1024, tn=1664, name="in_proj")
        proj = proj.reshape(b, l, P_W)

        ys, s_ssm = _ssd(proj, conv0[li] if cached else None, ssm0[li] if cached else None,
                         w['conv_w'], w['conv_b'], w['dt_bias'], w['a_log'], w['d_skip'], w['g_ssd'], q=128)
        ys = _pool(proj, pool0[li] if cached else None, past, w['w_pool'], w['pool_scale'], t=256,
                   ybuf=ys, slab=1)
        if cached:
            qn, kn, knb, vb = _qknorm(proj, w['g_q'], w['g_k'], t=512)
            ys = _attn_cached(qn, knb, vb, cache_k, cache_v, li, bias_past, cls_tbl, bias_new,
                              w['lam'], w['g_subln'], lam_init, tk=tk, ybuf=ys, slab=2)
        else:
            qt, kn, knb, vt = _qknorm_t(proj, w['g_q'], w['g_k'], t=tq)
            ys = _attn_self(qt, knb, vt, bias_self, w['lam'], w['g_subln'], lam_init, tq=tq, ybuf=ys, slab=2)
        ys, v_gm = _gmlp(proj, w['g_gv'], w['w_sp'], w['b_sp'], keep_vn=cached, ybuf=ys, slab=3)

        merged = _merge(h, w['g_mix'], ys.reshape(N_BRANCH, m, BRANCH_W), w['w_gate'], w['b_gate'], w['w_branch'])
        h = _mm(merged, w['w_out'], res=h, out_dtype=F32, tm=512, tn=2048, name="out_proj", w_resident=True)

        qx = _mm(h, w['w_xq'], g=w['g_x'], gh=w['g_xq'], out_dtype=BF16, tm=512, tn=512, name="xattn_q")
        o = _xattn(qx.reshape(b, l, X_H * X_D), mem_k[li].reshape(b, -1, X_H * X_D),
                   mem_v[li].reshape(b, -1, X_H * X_D), tq=512)
        h = _ffn(h, w['g_ffn2'], w['ffn2_in'], w['ffn2_out'], post_g=w['g_post'],
                 pre=(o.reshape(m, X_H * X_D), w['w_xo']))

        new_k.append(kn.reshape(b, l, ATT_KV, 2, ATT_D))
        new_v.append(proj[:, :, P_V:P_V + ATT_KV * ATT_VD].reshape(b, l, ATT_KV, ATT_VD))
        new_ssm.append(s_ssm)
        new_conv.append(proj[:, l - (CONV_W - 1):, P_XBC:P_XBC + XBC_W])
        new_pool.append(proj[:, l - POOL_BUF:, P_POOL:P_POOL + BRANCH_W])
        new_gv.append(v_gm)
    return (h.reshape(b, l, d), jnp.stack(new_k), jnp.stack(new_v), jnp.stack(new_ssm),
            jnp.stack(new_conv), jnp.stack(new_pool), jnp.stack(new_gv) if cached else None)


def kernel(x_prompt, x_sample, cache_attn_k, cache_attn_v, cache_mem_k, cache_mem_v, state_ssm, state_conv, state_pool, mem_prompt, g_ffn1, w_ffn1_in, w_ffn1_out, g_mix, w_in, conv_w, conv_b, dt_bias, a_log, d_skip, g_ssd, w_pool, pool_scale, g_q, g_k, lam, g_subln, rel_bias, g_gv, w_sp, b_sp, w_gate, b_gate, w_branch, w_out, g_x, w_xq, g_xq, g_mem, w_mk, w_mv, g_xk, w_xo, g_ffn2, w_ffn2_in, w_ffn2_out, g_post):
    p = dict(g_ffn1=g_ffn1, w_ffn1_in=w_ffn1_in, w_ffn1_out=w_ffn1_out, g_mix=g_mix, w_in=w_in,
             conv_w=conv_w, conv_b=conv_b, dt_bias=dt_bias, a_log=a_log, d_skip=d_skip, g_ssd=g_ssd,
             w_pool=w_pool, pool_scale=pool_scale, g_q=g_q, g_k=g_k, lam=lam, g_subln=g_subln,
             g_gv=g_gv, w_sp=w_sp, b_sp=b_sp, w_gate=w_gate, b_gate=b_gate, w_branch=w_branch,
             w_out=w_out, g_x=g_x, w_xq=w_xq, g_xq=g_xq, g_mem=g_mem, w_mk=w_mk, w_mv=w_mv, g_xk=g_xk,
             w_xo=w_xo, g_ffn2=g_ffn2, w_ffn2_in=w_ffn2_in, w_ffn2_out=w_ffn2_out, g_post=g_post)
    depth = w_in.shape[0]
    layers = _prep_weights(p, depth)

    bp, mlen, d = mem_prompt.shape
    mem2 = mem_prompt.reshape(bp * mlen, d)
    mks, mvs = [], []
    for w in layers:
        mk = _mm(mem2, w['w_mk'], g=w['g_mem'], gh=w['g_xk'], out_dtype=F32, tm=512, tn=512, name="mem_k")
        mv = _mm(mem2, w['w_mv'], g=w['g_mem'], out_dtype=F32, tm=512, tn=512, name="mem_v")
        mks.append(mk.reshape(bp, mlen, X_H, X_D))
        mvs.append(mv.reshape(bp, mlen, X_H, X_D))
    p_mem_k = jnp.stack(mks)
    p_mem_v = jnp.stack(mvs)

    y_prompt, p_attn_k, p_attn_v, p_ssm, p_conv, p_pool, _ = _run_trunk(
        x_prompt, None, None, p_mem_k, p_mem_v, None, None, None, layers, rel_bias)
    y_sample, s_attn_k, s_attn_v, s_ssm, s_conv, s_pool, s_gmlp_v = _run_trunk(
        x_sample, cache_attn_k, cache_attn_v, cache_mem_k, cache_mem_v, state_ssm, state_conv,
        state_pool, layers, rel_bias)

    return (y_prompt, y_sample, p_attn_k, p_attn_v, p_mem_k, p_mem_v, p_ssm, p_conv, p_pool,
            s_attn_k, s_attn_v, s_ssm, s_conv, s_pool, s_gmlp_v)
```

```python
import functools
import math

import jax
import jax.numpy as jnp
import numpy as np
from jax import lax
from jax.experimental import pallas as pl
from jax.experimental.pallas import tpu as pltpu

F32 = jnp.float32
BF16 = jnp.bfloat16
EPS = 1e-6
NEG = -1e30

VMEM_LIMIT_BYTES = 56 * 1024 * 1024
LANES = 128

BRANCH_W = 1024
SSM_P = 64
SSM_H = 16
SSM_G = 4
SSM_N = 128
SSM_D = SSM_H * SSM_P
CONV_W = 4
XBC_W = SSM_D + 2 * SSM_G * SSM_N
POOL_WINDOWS = (2, 4, 8, 16)
POOL_GW = BRANCH_W // len(POOL_WINDOWS)
POOL_BUF = max(POOL_WINDOWS) - 1
POOL_PAD = max(POOL_WINDOWS) // 2
ATT_H = 8
ATT_KV = 4
ATT_D = 64
ATT_VD = 128
CHUNK = 64
N_BUCKETS = 32
MAX_DIST = 128
GM_CHUNK = 128
GM_G = 4
GM_GW = BRANCH_W // GM_G
X_H = 4
X_D = 128
VT_ONES = 16

P_XBC = 0
P_Z = 2048
P_POOL = 3072
P_Q = 4096
P_GU = 5120
P_GV = 6144
P_K = 7168
P_V = 7680
P_DT = 8192
P_W = 8320


def _cparams(sem):
    return pltpu.CompilerParams(dimension_semantics=sem, vmem_limit_bytes=VMEM_LIMIT_BYTES)


N_BRANCH = 4


def _skip_ref(fn, pos):
    def wrapped(*refs):
        return fn(*refs[:pos], *refs[pos + 1:])
    return wrapped


def _branch_out(ybuf, slab, b, l, args, specs, kernel_fn, n_prefetch=0):
    shape = jax.ShapeDtypeStruct((N_BRANCH, b, l, BRANCH_W), BF16)
    if ybuf is None:
        return shape, kernel_fn, {}
    assert slab > 0 and ybuf.shape == shape.shape and ybuf.dtype == shape.dtype
    pos = n_prefetch + len(args)
    args.append(ybuf)
    specs.append(pl.BlockSpec(memory_space=pl.ANY))
    return shape, _skip_ref(kernel_fn, pos), {pos: 0}


def _rms(xf, g):
    ms = jnp.mean(xf * xf, axis=-1, keepdims=True)
    return xf * lax.rsqrt(ms + EPS) * g


def _silu(x):
    return x * jax.nn.sigmoid(x)


def _mm_kernel(*refs, norm, head_norm, residual):
    it = iter(refs)
    x_ref = next(it)
    g_ref = next(it) if norm else None
    w_ref = next(it)
    gh_ref = next(it) if head_norm else None
    res_ref = next(it) if residual else None
    o_ref = next(it)
    xn_ref = next(it) if norm else None

    if norm:
        @pl.when(pl.program_id(1) == 0)
        def _():
            xn_ref[...] = _rms(x_ref[...].astype(F32), g_ref[...]).astype(BF16)
        xb = xn_ref[...]
    else:
        xb = x_ref[...].astype(BF16)
    acc = jnp.dot(xb, w_ref[...], preferred_element_type=F32)
    if head_norm:
        parts = []
        for c in range(acc.shape[1] // LANES):
            parts.append(_rms(acc[:, c * LANES:(c + 1) * LANES], gh_ref[...]))
        acc = jnp.concatenate(parts, axis=-1)
    if residual:
        acc = res_ref[...] + acc
    o_ref[...] = acc.astype(o_ref.dtype)


def _mm(x, w, *, g=None, gh=None, res=None, out_dtype, tm, tn, name, w_resident=False):
    m, k = x.shape
    n = w.shape[1]
    tm = min(tm, m)
    tn = min(tn, n)
    assert m % tm == 0 and n % tn == 0
    norm, head_norm, residual = g is not None, gh is not None, res is not None
    assert not (norm and w_resident)
    if w_resident:
        grid = (n // tn, m // tm)
        ij = lambda a, b: (b, a)
    else:
        grid = (m // tm, n // tn)
        ij = lambda a, b: (a, b)
    args, specs = [x], [pl.BlockSpec((tm, k), lambda a, b: (ij(a, b)[0], 0))]
    if norm:
        args.append(g.reshape(1, k))
        specs.append(pl.BlockSpec((1, k), lambda a, b: (0, 0)))
    args.append(w)
    specs.append(pl.BlockSpec((k, tn), lambda a, b: (0, ij(a, b)[1])))
    if head_norm:
        args.append(gh.reshape(1, LANES))
        specs.append(pl.BlockSpec((1, LANES), lambda a, b: (0, 0)))
    if residual:
        args.append(res)
        specs.append(pl.BlockSpec((tm, tn), lambda a, b: ij(a, b)))
    scratch = [pltpu.VMEM((tm, k), BF16)] if norm else []
    return pl.pallas_call(
        functools.partial(_mm_kernel, norm=norm, head_norm=head_norm, residual=residual),
        out_shape=jax.ShapeDtypeStruct((m, n), out_dtype),
        grid=grid,
        in_specs=specs,
        out_specs=pl.BlockSpec((tm, tn), lambda a, b: ij(a, b)),
        scratch_shapes=scratch,
        compiler_params=_cparams(("parallel", "arbitrary")),
        name=name,
    )(*args)


def _ffn_kernel(*refs, post, pre):
    it = iter(refs)
    x_ref, g_ref, wg_ref, wu_ref, wo_ref = (next(it) for _ in range(5))
    gp_ref = next(it) if post else None
    a_ref, wa_ref = (next(it), next(it)) if pre else (None, None)
    o_ref, xn_ref = next(it), next(it)
    acc_ref = next(it) if pre else o_ref
    f = pl.program_id(1)

    @pl.when(f == 0)
    def _():
        x = x_ref[...]
        if pre:
            x = x + jnp.dot(a_ref[...], wa_ref[...], preferred_element_type=F32)
            o_ref[...] = x
        xn_ref[...] = _rms(x, g_ref[...]).astype(BF16)
        acc_ref[...] = jnp.zeros_like(acc_ref)

    xb = xn_ref[...]
    gate = jnp.dot(xb, wg_ref[...], preferred_element_type=F32)
    up = jnp.dot(xb, wu_ref[...], preferred_element_type=F32)
    mid = (_silu(gate) * up).astype(BF16)
    acc_ref[...] += jnp.dot(mid, wo_ref[...], preferred_element_type=F32)

    @pl.when(f == pl.num_programs(1) - 1)
    def _():
        h = (o_ref[...] if pre else x_ref[...]) + 0.5 * acc_ref[...]
        if post:
            h = _rms(h, gp_ref[...])
        o_ref[...] = h


def _ffn(x, g, w_in, w_out, *, post_g=None, pre=None, tm=512, tf=512):
    m, d = x.shape
    ff = w_out.shape[0]
    tm = min(tm, m)
    assert m % tm == 0 and ff % tf == 0
    nf = ff // tf
    post = post_g is not None
    args = [x, g.reshape(1, d), w_in, w_in, w_out]
    specs = [
        pl.BlockSpec((tm, d), lambda i, f: (i, 0)),
        pl.BlockSpec((1, d), lambda i, f: (0, 0)),
        pl.BlockSpec((d, tf), lambda i, f: (0, f)),
        pl.BlockSpec((d, tf), lambda i, f: (0, nf + f)),
        pl.BlockSpec((tf, d), lambda i, f: (f, 0)),
    ]
    if post:
        args.append(post_g.reshape(1, d))
        specs.append(pl.BlockSpec((1, d), lambda i, f: (0, 0)))
    if pre is not None:
        a, wa = pre
        ka = a.shape[1]
        args += [a, wa]
        specs += [pl.BlockSpec((tm, ka), lambda i, f: (i, 0)), pl.BlockSpec((ka, d), lambda i, f: (0, 0))]
    return pl.pallas_call(
        functools.partial(_ffn_kernel, post=post, pre=pre is not None),
        out_shape=jax.ShapeDtypeStruct((m, d), F32),
        grid=(m // tm, nf),
        in_specs=specs,
        out_specs=pl.BlockSpec((tm, d), lambda i, f: (i, 0)),
        scratch_shapes=[pltpu.VMEM((tm, d), BF16)] + ([pltpu.VMEM((tm, d), F32)] if pre is not None else []),
        compiler_params=_cparams(("parallel", "arbitrary")),
        name="swiglu_ffn",
    )(*args)


def _merge_kernel(h_ref, g_ref, y_ref, wg_ref, bg_ref, wb_ref, o_ref, hn_ref, acc_ref):
    j = pl.program_id(1)
    br = pl.program_id(2)

    @pl.when((j == 0) & (br == 0))
    def _():
        hn_ref[...] = _rms(h_ref[...], g_ref[...]).astype(BF16)

    @pl.when(br == 0)
    def _():
        acc_ref[...] = jnp.zeros_like(acc_ref)

    gate = jax.nn.sigmoid(jnp.dot(hn_ref[...], wg_ref[...], preferred_element_type=F32) + bg_ref[...])
    acc_ref[...] += gate * jnp.dot(y_ref[...], wb_ref[...], preferred_element_type=F32)

    @pl.when(br == pl.num_programs(2) - 1)
    def _():
        o_ref[...] = acc_ref[...].astype(o_ref.dtype)


def _merge(h, g, ys, w_gate, b_gate, w_branch, *, tm=1024, tn=1024):
    m, d = h.shape
    nb, bw, _ = w_branch.shape
    tm = min(tm, m)
    assert m % tm == 0 and d % tn == 0 and ys.shape == (nb, m, bw)
    return pl.pallas_call(
        _merge_kernel,
        out_shape=jax.ShapeDtypeStruct((m, d), BF16),
        grid=(m // tm, d // tn, nb),
        in_specs=[
            pl.BlockSpec((tm, d), lambda i, j, b: (i, 0)),
            pl.BlockSpec((1, d), lambda i, j, b: (0, 0)),
            pl.BlockSpec((None, tm, bw), lambda i, j, b: (b, i, 0)),
            pl.BlockSpec((None, d, tn), lambda i, j, b: (b, 0, j)),
            pl.BlockSpec((None, 1, tn), lambda i, j, b: (b, 0, j)),
            pl.BlockSpec((None, bw, tn), lambda i, j, b: (b, 0, j)),
        ],
        out_specs=pl.BlockSpec((tm, tn), lambda i, j, b: (i, j)),
        scratch_shapes=[pltpu.VMEM((tm, d), BF16), pltpu.VMEM((tm, tn), F32)],
        compiler_params=_cparams(("parallel", "arbitrary", "arbitrary")),
        name="branch_merge",
    )(h, g.reshape(1, d), ys, w_gate, b_gate.reshape(nb, 1, d), w_branch)


def _ssd_kernel(*refs, q, has_state):
    it = iter(refs)
    xbc_ref, z_ref, dt_ref = next(it), next(it), next(it)
    conv0_ref = next(it) if has_state else None
    s0_ref = next(it) if has_state else None
    cw_ref, cb_ref, dtb_ref, alog_ref, dskip_ref, gssd_ref = (next(it) for _ in range(6))
    y_ref, s_ref, cbuf = next(it), next(it), next(it)
    hist = 8

    @pl.when(pl.program_id(1) == 0)
    def _():
        cbuf[0:hist, :] = jnp.zeros((hist, XBC_W), F32)
        if has_state:
            cbuf[hist - (CONV_W - 1):hist, :] = conv0_ref[0]
            s_ref[0] = s0_ref[0]
        else:
            s_ref[0] = jnp.zeros(s_ref.shape[1:], F32)

    cbuf[hist:hist + q, :] = xbc_ref[0]
    conv = cb_ref[...]
    for k in range(CONV_W):
        lo = hist - (CONV_W - 1) + k
        conv = conv + cbuf[lo:lo + q, :] * cw_ref[k:k + 1, :]
    cbuf[0:hist, :] = cbuf[q:q + hist, :]
    xc = _silu(conv)
    xs = xc[:, :SSM_D]
    bm = xc[:, SSM_D:SSM_D + SSM_G * SSM_N]
    cm = xc[:, SSM_D + SSM_G * SSM_N:]

    dtl = dt_ref[0] + dtb_ref[...]
    dt = jnp.maximum(dtl, 0.0) + jnp.log1p(jnp.exp(-jnp.abs(dtl)))
    a = dt * (-jnp.exp(alog_ref[...]))
    ri = lax.broadcasted_iota(jnp.int32, (q, q), 0)
    ci = lax.broadcasted_iota(jnp.int32, (q, q), 1)
    tril = ri >= ci
    cum = jnp.dot(tril.astype(F32), a, preferred_element_type=F32,
                  precision=lax.Precision.HIGHEST)
    cum_t = cum.T
    dt_t = dt.T
    cum_last = cum[q - 1:q, :]
    w_state = jnp.exp(cum_last - cum) * dt
    ecum = jnp.exp(cum)
    ecl = jnp.exp(cum_last)

    lane_lo = lax.broadcasted_iota(jnp.int32, (q, LANES), 1) < SSM_P
    row_lo_t = lax.broadcasted_iota(jnp.int32, (LANES, q), 0) < SSM_P
    row_lo_s = lax.broadcasted_iota(jnp.int32, (LANES, SSM_N), 0) < SSM_P
    dn_t = (((1,), (1,)), ((), ()))

    y_parts = []
    cb = None
    heads_per_group = SSM_H // SSM_G
    for pp in range(SSM_H // 2):
        grp = (2 * pp) // heads_per_group
        xp = xs[:, pp * LANES:(pp + 1) * LANES]
        bg = bm[:, grp * SSM_N:(grp + 1) * SSM_N]
        cg = cm[:, grp * SSM_N:(grp + 1) * SSM_N]
        cg_b = cg.astype(BF16)
        if (2 * pp) % heads_per_group == 0:
            cb = lax.dot_general(cg_b, bg.astype(BF16), dn_t, preferred_element_type=F32)
        xp_t = xp.T
        s_pair = s_ref[0, pp * LANES:(pp + 1) * LANES, :]
        y_pair = None
        s_new = None
        for hh in range(2):
            hd = 2 * pp + hh
            seg = cum[:, hd:hd + 1] - cum_t[hd:hd + 1, :]
            mat = jnp.exp(jnp.where(tril, seg, NEG)) * cb * dt_t[hd:hd + 1, :]
            sel = lane_lo if hh == 0 else jnp.logical_not(lane_lo)
            x_h = jnp.where(sel, xp, 0.0).astype(BF16)
            yd = jnp.dot(mat.astype(BF16), x_h, preferred_element_type=F32)
            y_pair = yd if y_pair is None else y_pair + yd
            sel_t = row_lo_t if hh == 0 else jnp.logical_not(row_lo_t)
            xt_h = jnp.where(sel_t, xp_t, 0.0).astype(BF16)
            bs = (bg * w_state[:, hd:hd + 1]).astype(BF16)
            sn = jnp.dot(xt_h, bs, preferred_element_type=F32)
            s_new = sn if s_new is None else s_new + sn
        y_off = lax.dot_general(cg_b, s_pair.astype(BF16), dn_t, preferred_element_type=F32)
        e_pair = jnp.where(lane_lo, ecum[:, 2 * pp:2 * pp + 1], ecum[:, 2 * pp + 1:2 * pp + 2])
        y_parts.append(y_pair + y_off * e_pair)
        dec = jnp.where(row_lo_s, ecl[:, 2 * pp:2 * pp + 1], ecl[:, 2 * pp + 1:2 * pp + 2])
        s_ref[0, pp * LANES:(pp + 1) * LANES, :] = dec * s_pair + s_new

    y = jnp.concatenate(y_parts, axis=-1)
    y = y + dskip_ref[...] * xs
    y = y * _silu(z_ref[0])
    gw = SSM_D // SSM_G
    outs = [_rms(y[:, gi * gw:(gi + 1) * gw], gssd_ref[:, gi * gw:(gi + 1) * gw]) for gi in range(SSM_G)]
    y_ref[0] = jnp.concatenate(outs, axis=-1).astype(y_ref.dtype)


def _ssd(proj, conv0, s0, cw, cb, dtb, alog, dskip, gssd, *, q, ybuf=None, slab=0):
    b, l, _ = proj.shape
    q = min(q, l)
    assert l % q == 0 and q >= 8
    has_state = s0 is not None
    args = [proj, proj, proj]
    specs = [
        pl.BlockSpec((1, q, XBC_W), lambda i, c: (i, c, P_XBC // XBC_W)),
        pl.BlockSpec((1, q, SSM_D), lambda i, c: (i, c, P_Z // SSM_D)),
        pl.BlockSpec((1, q, LANES), lambda i, c: (i, c, P_DT // LANES)),
    ]
    if has_state:
        args += [conv0, s0.reshape(b, SSM_H * SSM_P, SSM_N)]
        specs += [
            pl.BlockSpec((1, CONV_W - 1, XBC_W), lambda i, c: (i, 0, 0)),
            pl.BlockSpec((1, SSM_H * SSM_P, SSM_N), lambda i, c: (i, 0, 0)),
        ]
    args += [cw, cb, dtb, alog, dskip, gssd]
    specs += [
        pl.BlockSpec((CONV_W, XBC_W), lambda i, c: (0, 0)),
        pl.BlockSpec((1, XBC_W), lambda i, c: (0, 0)),
        pl.BlockSpec((1, LANES), lambda i, c: (0, 0)),
        pl.BlockSpec((1, LANES), lambda i, c: (0, 0)),
        pl.BlockSpec((1, SSM_D), lambda i, c: (0, 0)),
        pl.BlockSpec((1, SSM_D), lambda i, c: (0, 0)),
    ]
    yshape, kern, aliases = _branch_out(ybuf, slab, b, l, args, specs,
                                        functools.partial(_ssd_kernel, q=q, has_state=has_state))
    y, s = pl.pallas_call(
        kern,
        out_shape=(yshape, jax.ShapeDtypeStruct((b, SSM_H * SSM_P, SSM_N), F32)),
        grid=(b, l // q),
        in_specs=specs,
        out_specs=(pl.BlockSpec((None, 1, q, SSM_D), lambda i, c: (slab, i, c, 0)),
                   pl.BlockSpec((1, SSM_H * SSM_P, SSM_N), lambda i, c: (i, 0, 0))),
        scratch_shapes=[pltpu.VMEM((q + 8, XBC_W), F32)],
        input_output_aliases=aliases,
        compiler_params=_cparams(("parallel", "arbitrary")),
        name="ssd_mixer",
    )(*args)
    return y, s.reshape(b, SSM_H, SSM_P, SSM_N)


def _pool_kernel(*refs, t, pos0, has_state):
    it = iter(refs)
    xp_ref = next(it)
    p0_ref = next(it) if has_state else None
    w_ref, sc_ref, y_ref, buf, lvl = next(it), next(it), next(it), next(it), next(it)
    hist = POOL_BUF + 1
    pad = POOL_PAD
    base = pad + hist
    n = hist + t
    ti = pl.program_id(1)

    @pl.when(ti == 0)
    def _():
        buf[0:base, :] = jnp.zeros((base, BRANCH_W), F32)
        lvl[0:pad, :] = jnp.zeros((pad, BRANCH_W), F32)
        if has_state:
            buf[pad + 1:base, :] = p0_ref[0]

    xp = xp_ref[0]
    buf[base:base + t, :] = xp
    pos = pos0 + ti * t + lax.broadcasted_iota(jnp.int32, (t, POOL_GW), 0)
    outs = []
    for gi, w in enumerate(POOL_WINDOWS):
        cs = slice(gi * POOL_GW, (gi + 1) * POOL_GW)
        src, k = buf, 1
        while 2 * k < w:
            lvl[pad:pad + n, cs] = src[pad:pad + n, cs] + src[pad - k:pad - k + n, cs]
            src, k = lvl, 2 * k
        s = src[base:base + t, cs] + src[base - k:base - k + t, cs]
        cnt = jnp.minimum(w, pos + 1).astype(F32)
        pooled = s / cnt - xp[:, cs]
        y = jnp.dot(pooled.astype(BF16), w_ref[gi], preferred_element_type=F32)
        outs.append(y * sc_ref[:, cs])
    y_ref[0] = jnp.concatenate(outs, axis=-1).astype(y_ref.dtype)
    buf[pad:base, :] = buf[pad + t:base + t, :]


def _pool(proj, pool0, pos0, w_pool, scale, *, t, ybuf=None, slab=0):
    b, l, _ = proj.shape
    t = min(t, l)
    assert l % t == 0 and t >= POOL_BUF + 1
    has_state = pool0 is not None
    args = [proj]
    specs = [pl.BlockSpec((1, t, BRANCH_W), lambda i, c: (i, c, P_POOL // BRANCH_W))]
    if has_state:
        args.append(pool0)
        specs.append(pl.BlockSpec((1, POOL_BUF, BRANCH_W), lambda i, c: (i, 0, 0)))
    args += [w_pool, scale]
    specs += [
        pl.BlockSpec((len(POOL_WINDOWS), POOL_GW, POOL_GW), lambda i, c: (0, 0, 0)),
        pl.BlockSpec((1, BRANCH_W), lambda i, c: (0, 0)),
    ]
    yshape, kern, aliases = _branch_out(
        ybuf, slab, b, l, args, specs, functools.partial(_pool_kernel, t=t, pos0=pos0, has_state=has_state))
    return pl.pallas_call(
        kern,
        out_shape=yshape,
        grid=(b, l // t),
        in_specs=specs,
        out_specs=pl.BlockSpec((None, 1, t, BRANCH_W), lambda i, c: (slab, i, c, 0)),
        scratch_shapes=[pltpu.VMEM((t + POOL_BUF + 1 + POOL_PAD, BRANCH_W), F32),
                        pltpu.VMEM((t + POOL_BUF + 1 + POOL_PAD, BRANCH_W), F32)],
        input_output_aliases=aliases,
        compiler_params=_cparams(("parallel", "arbitrary")),
        name="pool_mixer",
    )(*args)


def _gmlp_kernel(u_ref, v_ref, g_ref, w_ref, b_ref, y_ref, *rest, cl, nsub, keep_vn):
    tril = lax.broadcasted_iota(jnp.int32, (cl, cl), 0) >= lax.broadcasted_iota(jnp.int32, (cl, cl), 1)
    ws = [jnp.where(tril, w_ref[gi], 0.0).astype(BF16) for gi in range(GM_G)]
    for sub in range(nsub):
        rows = slice(sub * cl, (sub + 1) * cl)
        u = jax.nn.gelu(u_ref[0, rows, :], approximate=True)
        vn = _rms(jax.nn.gelu(v_ref[0, rows, :], approximate=True), g_ref[...])
        if keep_vn:
            rest[0][0, rows, :] = vn
        outs = []
        for gi in range(GM_G):
            s = jnp.dot(ws[gi], vn[:, gi * GM_GW:(gi + 1) * GM_GW].astype(BF16), preferred_element_type=F32)
            s = s + b_ref[:, gi:gi + 1]
            outs.append(u[:, gi * GM_GW:(gi + 1) * GM_GW] * s)
        y_ref[0, rows, :] = jnp.concatenate(outs, axis=-1).astype(y_ref.dtype)


def _gmlp(proj, g_gv, w_sp, b_sp, *, keep_vn, ybuf=None, slab=0):
    b, l, _ = proj.shape
    cl = min(GM_CHUNK, l)
    assert l % cl == 0
    nsub = math.gcd(l // cl, 4)
    t = cl * nsub
    w = w_sp[:, :cl, :cl]
    bt = b_sp[:, :cl].T
    args = [proj, proj, g_gv, w, bt]
    specs = [
        pl.BlockSpec((1, t, BRANCH_W), lambda i, c: (i, c, P_GU // BRANCH_W)),
        pl.BlockSpec((1, t, BRANCH_W), lambda i, c: (i, c, P_GV // BRANCH_W)),
        pl.BlockSpec((1, BRANCH_W), lambda i, c: (0, 0)),
        pl.BlockSpec((GM_G, cl, cl), lambda i, c: (0, 0, 0)),
        pl.BlockSpec((cl, GM_G), lambda i, c: (0, 0)),
    ]
    yshape, kern, aliases = _branch_out(
        ybuf, slab, b, l, args, specs, functools.partial(_gmlp_kernel, cl=cl, nsub=nsub, keep_vn=keep_vn))
    out_shape = [yshape]
    out_specs = [pl.BlockSpec((None, 1, t, BRANCH_W), lambda i, c: (slab, i, c, 0))]
    if keep_vn:
        out_shape.append(jax.ShapeDtypeStruct((b, l, BRANCH_W), F32))
        out_specs.append(pl.BlockSpec((1, t, BRANCH_W), lambda i, c: (i, c, 0)))
    outs = pl.pallas_call(
        kern,
        out_shape=tuple(out_shape),
        grid=(b, l // t),
        in_specs=specs,
        out_specs=tuple(out_specs),
        input_output_aliases=aliases,
        compiler_params=_cparams(("parallel", "parallel")),
        name="gmlp_mixer",
    )(*args)
    return outs[0], (outs[1] if keep_vn else None)


def _norm_halves(x, g):
    lo = lax.broadcasted_iota(jnp.int32, x.shape, 1) < ATT_D
    xx = x * x
    s_lo = jnp.sum(jnp.where(lo, xx, 0.0), axis=-1, keepdims=True)
    s_hi = jnp.sum(jnp.where(lo, 0.0, xx), axis=-1, keepdims=True)
    ms = jnp.where(lo, s_lo, s_hi) * (1.0 / ATT_D)
    return x * lax.rsqrt(ms + EPS) * g


def _qknorm_kernel(q_ref, k_ref, v_ref, gq_ref, gk_ref, qn_ref, kn_ref, knb_ref, vb_ref):
    scale = ATT_D ** -0.5
    q = q_ref[0]
    qs = [_norm_halves(q[:, c * LANES:(c + 1) * LANES], gq_ref[...]) * scale
          for c in range(q.shape[1] // LANES)]
    qn_ref[0] = jnp.concatenate(qs, axis=-1).astype(qn_ref.dtype)
    k = k_ref[0]
    ks = [_norm_halves(k[:, c * LANES:(c + 1) * LANES], gk_ref[...]) for c in range(k.shape[1] // LANES)]
    kn = jnp.concatenate(ks, axis=-1)
    kn_ref[0] = kn
    knb_ref[0] = kn.astype(BF16)
    vb_ref[0] = v_ref[0].astype(BF16)


def _qknorm(proj, g_q, g_k, *, t):
    b, l, _ = proj.shape
    t = min(t, l)
    assert l % t == 0
    qw, kw = ATT_H * 2 * ATT_D, ATT_KV * 2 * ATT_D
    gq = jnp.tile(g_q, 2).reshape(1, LANES)
    gk = jnp.tile(g_k, 2).reshape(1, LANES)
    return pl.pallas_call(
        _qknorm_kernel,
        out_shape=(jax.ShapeDtypeStruct((b, l, qw), BF16), jax.ShapeDtypeStruct((b, l, kw), F32),
                   jax.ShapeDtypeStruct((b, l, kw), BF16), jax.ShapeDtypeStruct((b, l, kw), BF16)),
        grid=(b, l // t),
        in_specs=[
            pl.BlockSpec((1, t, qw), lambda i, c: (i, c, P_Q // qw)),
            pl.BlockSpec((1, t, kw), lambda i, c: (i, c, P_K // kw)),
            pl.BlockSpec((1, t, kw), lambda i, c: (i, c, P_V // kw)),
            pl.BlockSpec((1, LANES), lambda i, c: (0, 0)),
            pl.BlockSpec((1, LANES), lambda i, c: (0, 0)),
        ],
        out_specs=(pl.BlockSpec((1, t, qw), lambda i, c: (i, c, 0)),
                   pl.BlockSpec((1, t, kw), lambda i, c: (i, c, 0)),
                   pl.BlockSpec((1, t, kw), lambda i, c: (i, c, 0)),
                   pl.BlockSpec((1, t, kw), lambda i, c: (i, c, 0))),
        compiler_params=_cparams(("parallel", "parallel")),
        name="qk_norm",
    )(proj, proj, proj, gq, gk)


def _bias_kernel(off_ref, tbl_ref, o_ref, *, tq, tk, keys_major):
    c = pl.program_id(0)
    h = pl.program_id(1)
    off = off_ref[c]
    shape = (tk, tq) if keys_major else (tq, tk)
    row = lax.broadcasted_iota(jnp.int32, shape, 1 if keys_major else 0)
    col = lax.broadcasted_iota(jnp.int32, shape, 0 if keys_major else 1)
    rel = off + col - row
    nb = N_BUCKETS // 2
    max_exact = nb // 2
    n = jnp.abs(rel)
    large = max_exact + (jnp.log(jnp.maximum(n, 1).astype(F32) / max_exact)
                         / math.log(MAX_DIST / max_exact) * (nb - max_exact)).astype(jnp.int32)
    large = jnp.minimum(large, nb - 1)
    bucket = jnp.where(rel > 0, nb, 0) + jnp.where(n < max_exact, n, large)
    val = jnp.zeros(shape, F32)
    for bk in range(N_BUCKETS):
        val = jnp.where(bucket == bk, tbl_ref[bk * ATT_H + h], val)
    visible = jnp.right_shift(off + col, 6) <= jnp.right_shift(row, 6)
    o_ref[0, 0] = jnp.where(visible, val, NEG)


def _bias_tiles(rel_bias, offsets, tq, tk, keys_major=False):
    assert CHUNK == 64
    offs = jnp.asarray(np.asarray(offsets, np.int32))
    n_cls = len(offsets)
    shape = (tk, tq) if keys_major else (tq, tk)
    return pl.pallas_call(
        functools.partial(_bias_kernel, tq=tq, tk=tk, keys_major=keys_major),
        out_shape=jax.ShapeDtypeStruct((n_cls, ATT_H) + shape, F32),
        grid=(n_cls, ATT_H),
        in_specs=[pl.BlockSpec(memory_space=pltpu.SMEM), pl.BlockSpec(memory_space=pltpu.SMEM)],
        out_specs=pl.BlockSpec((1, 1) + shape, lambda c, h: (c, h, 0, 0)),
        compiler_params=_cparams(("parallel", "parallel")),
        name="rel_bias_tiles",
    )(offs, rel_bias.reshape(-1))


FAR_OFFSET = -(1 << 20)


def _is_far(off, tq, tk):
    nb = N_BUCKETS // 2
    max_rel = off + tk - 1
    if max_rel >= 0:
        return False
    n_min = -max_rel
    sat = (nb // 2) * (MAX_DIST / (nb // 2)) ** ((nb - 1 - nb // 2) / (nb - nb // 2))
    return n_min >= math.ceil(sat) + 1


def _q4(q2):
    lo = lax.broadcasted_iota(jnp.int32, (q2.shape[0], LANES), 1) < ATT_D
    parts = []
    for r in range(2):
        qr = q2[:, r * LANES:(r + 1) * LANES]
        parts.append(jnp.where(lo, qr, jnp.zeros_like(qr)))
        parts.append(jnp.where(lo, jnp.zeros_like(qr), qr))
    return jnp.concatenate(parts, axis=0)


def _osm_softmax(s, b0, b1, m_ref, l_ref):
    s = s + jnp.concatenate([b0, b0, b1, b1], axis=0)
    m_prev = m_ref[...]
    m_new = jnp.maximum(m_prev, jnp.max(s, axis=-1, keepdims=True))
    alpha = jnp.exp(m_prev - m_new)
    p = jnp.exp(s - m_new[:, 0:1])
    l_ref[...] = alpha * l_ref[...] + jnp.sum(p, axis=-1, keepdims=True)
    m_ref[...] = m_new
    return p.astype(BF16), alpha


def _osm_finish(l_ref, acc_ref, lam_ref, gs_ref, lam_init, tq):
    lp = lam_ref[...]
    lam = (jnp.exp(jnp.sum(lp[0:1] * lp[1:2], axis=-1, keepdims=True))
           - jnp.exp(jnp.sum(lp[2:3] * lp[3:4], axis=-1, keepdims=True)) + lam_init)
    o = acc_ref[...] / l_ref[...]
    outs = []
    for r in range(2):
        d = o[(2 * r) * tq:(2 * r + 1) * tq] - lam * o[(2 * r + 1) * tq:(2 * r + 2) * tq]
        outs.append(_rms(d, gs_ref[...]) * (1.0 - lam_init))
    return jnp.concatenate(outs, axis=-1)


def _qknorm_t_kernel(q_ref, k_ref, v_ref, gq_ref, gk_ref, qt_ref, kn_ref, knb_ref, vt_ref):
    scale = ATT_D ** -0.5
    q = q_ref[0]
    t = q.shape[0]
    lo = lax.broadcasted_iota(jnp.int32, (LANES, t), 0) < ATT_D
    for kv in range(ATT_KV):
        cols = []
        for r in range(2):
            c = kv * 2 + r
            qr = q[:, c * LANES:(c + 1) * LANES].T
            xx = qr * qr
            ms = jnp.where(lo, jnp.sum(xx[:ATT_D], axis=0, keepdims=True),
                           jnp.sum(xx[ATT_D:], axis=0, keepdims=True)) * (1.0 / ATT_D)
            qr = qr * lax.rsqrt(ms + EPS) * gq_ref[...] * scale
            cols.append(jnp.where(lo, qr, 0.0))
            cols.append(jnp.where(lo, 0.0, qr))
        qt_ref[0, kv] = jnp.concatenate(cols, axis=1).astype(qt_ref.dtype)
    k = k_ref[0]
    ks = [_norm_halves(k[:, c * LANES:(c + 1) * LANES], gk_ref[...]) for c in range(k.shape[1] // LANES)]
    kn = jnp.concatenate(ks, axis=-1)
    kn_ref[0] = kn
    knb_ref[0] = kn.astype(BF16)
    v = v_ref[0]
    for kv in range(ATT_KV):
        vt_ref[0, kv, 0] = jnp.concatenate(
            [v[:, kv * LANES:(kv + 1) * LANES].T, jnp.ones((VT_ONES, t), F32)], axis=0).astype(vt_ref.dtype)


def _qknorm_t(proj, g_q, g_k, *, t):
    b, l, _ = proj.shape
    assert l % t == 0
    qw, kw = ATT_H * 2 * ATT_D, ATT_KV * 2 * ATT_D
    gq = jnp.tile(g_q, 2).reshape(LANES, 1)
    gk = jnp.tile(g_k, 2).reshape(1, LANES)
    return pl.pallas_call(
        _qknorm_t_kernel,
        out_shape=(jax.ShapeDtypeStruct((b, ATT_KV, LANES, 4 * l), BF16),
                   jax.ShapeDtypeStruct((b, l, kw), F32),
                   jax.ShapeDtypeStruct((b, l, kw), BF16),
                   jax.ShapeDtypeStruct((b, ATT_KV, l // t, ATT_VD + VT_ONES, t), BF16)),
        grid=(b, l // t),
        in_specs=[
            pl.BlockSpec((1, t, qw), lambda i, c: (i, c, P_Q // qw)),
            pl.BlockSpec((1, t, kw), lambda i, c: (i, c, P_K // kw)),
            pl.BlockSpec((1, t, kw), lambda i, c: (i, c, P_V // kw)),
            pl.BlockSpec((LANES, 1), lambda i, c: (0, 0)),
            pl.BlockSpec((1, LANES), lambda i, c: (0, 0)),
        ],
        out_specs=(pl.BlockSpec((1, ATT_KV, LANES, 4 * t), lambda i, c: (i, 0, 0, c)),
                   pl.BlockSpec((1, t, kw), lambda i, c: (i, c, 0)),
                   pl.BlockSpec((1, t, kw), lambda i, c: (i, c, 0)),
                   pl.BlockSpec((1, ATT_KV, 1, ATT_VD + VT_ONES, t), lambda i, c: (i, 0, c, 0, 0))),
        compiler_params=_cparams(("parallel", "parallel")),
        name="qk_norm_t",
    )(proj, proj, proj, gq, gk)


def _attn_self_kernel(qt_ref, k_ref, vt_ref, bias_ref, lam_ref, gs_ref, o_ref, m_ref, acc_ref,
                      *, tq, tk, lam_init, n_cls, group):
    i = pl.program_id(2)
    m_ref[...] = jnp.full(m_ref.shape, NEG, F32)
    acc_ref[...] = jnp.zeros(acc_ref.shape, F32)
    sub = LANES
    assert tq == tk and tq % sub == 0 and sub % CHUNK == 0

    def step(j, mode):
        kt = k_ref[0, pl.ds(pl.multiple_of(j * tk, tk), tk), :]
        vt = vt_ref[0, 0, j]
        cls = jnp.minimum(i - j, n_cls - 1)
        for c0 in range(0, 4, group):
            update(kt, vt, cls, mode, range(c0, c0 + group))

    def softmax_block(s, m_prev, crow):
        m_cur = jnp.max(s, axis=0, keepdims=True)
        if crow is None:
            m_new = jnp.maximum(m_prev, m_cur)
            return jnp.exp(s - m_new).astype(BF16), m_new
        m_new = jnp.maximum(m_prev, m_cur + crow)
        return jnp.exp(s - (m_new - crow)).astype(BF16), m_new

    def update(kt, vt, cls, mode, combos):
        cols = {c: slice(c * tq, (c + 1) * tq) for c in combos}
        ss = {c: jnp.dot(kt, qt_ref[0, 0, :, cs], preferred_element_type=F32) for c, cs in cols.items()}
        ps, alphas = {}, {}
        for c, cs in cols.items():
            s = ss[c]
            m_prev = m_ref[:, cs]
            if mode == "far":
                p, m_new = softmax_block(s, m_prev, bias_ref[cls, c // 2, 0:1, :])
            elif mode == "near":
                p, m_new = softmax_block(s + bias_ref[cls, c // 2], m_prev, None)
            else:
                pb, mb = [], []
                for qb in range(tq // sub):
                    rows, lanes = (qb + 1) * sub, slice(qb * sub, (qb + 1) * sub)
                    p_b, m_b = softmax_block(s[:rows, lanes] + bias_ref[cls, c // 2, :rows, lanes],
                                             m_prev[:, lanes], None)
                    if rows < tk:
                        p_b = jnp.concatenate([p_b, jnp.zeros((tk - rows, sub), BF16)], axis=0)
                    pb.append(p_b)
                    mb.append(m_b)
                p, m_new = jnp.concatenate(pb, axis=1), jnp.concatenate(mb, axis=1)
            alphas[c] = jnp.exp(m_prev - m_new)
            m_ref[:, cs] = m_new
            ps[c] = p
        for c, cs in cols.items():
            acc_ref[:, cs] = alphas[c] * acc_ref[:, cs] + jnp.dot(vt, ps[c], preferred_element_type=F32)

    n_far = jnp.maximum(i - (n_cls - 2), 0) if n_cls == 3 else 0

    def body(mode):
        def run(j, carry):
            step(j, mode)
            return carry
        return run

    if n_cls == 3:
        lax.fori_loop(0, n_far, body("far"), 0)
    lax.fori_loop(n_far, i, body("near"), 0)
    step(i, "diag")

    lp = lam_ref[...]
    lam = (jnp.exp(jnp.sum(lp[0:1] * lp[1:2], axis=-1, keepdims=True))
           - jnp.exp(jnp.sum(lp[2:3] * lp[3:4], axis=-1, keepdims=True)) + lam_init)
    o = acc_ref[0:ATT_VD, :] / acc_ref[ATT_VD:ATT_VD + 1, :]
    outs = []
    for r in range(2):
        d = o[:, (2 * r) * tq:(2 * r + 1) * tq] - lam * o[:, (2 * r + 1) * tq:(2 * r + 2) * tq]
        ms = jnp.mean(d * d, axis=0, keepdims=True)
        y = d * lax.rsqrt(ms + EPS) * gs_ref[...] * (1.0 - lam_init)
        outs.append(y.T)
    o_ref[0] = jnp.concatenate(outs, axis=-1).astype(o_ref.dtype)


def _attn_self(qt, knb, vt, bias, lam_p, g_subln, lam_init, *, tq, ybuf=None, slab=0):
    b, l, _ = knb.shape
    n_cls = bias.shape[0]
    nk = l // tq
    hw = 2 * ATT_VD
    assert ATT_H * ATT_VD == BRANCH_W
    args = [qt, knb, vt, bias, lam_p, g_subln.reshape(ATT_VD, 1)]
    specs = [
        pl.BlockSpec((1, 1, LANES, 4 * tq), lambda bi, kv, i: (bi, kv, 0, i)),
        pl.BlockSpec((1, l, LANES), lambda bi, kv, i: (bi, 0, kv)),
        pl.BlockSpec((1, 1, nk, ATT_VD + VT_ONES, tq), lambda bi, kv, i: (bi, kv, 0, 0, 0)),
        pl.BlockSpec((n_cls, 2, tq, tq), lambda bi, kv, i: (0, kv, 0, 0)),
        pl.BlockSpec((4, ATT_D), lambda bi, kv, i: (0, 0)),
        pl.BlockSpec((ATT_VD, 1), lambda bi, kv, i: (0, 0)),
    ]
    yshape, kern, aliases = _branch_out(
        ybuf, slab, b, l, args, specs,
        functools.partial(_attn_self_kernel, tq=tq, tk=tq, lam_init=lam_init, n_cls=n_cls, group=4))
    return pl.pallas_call(
        kern,
        out_shape=yshape,
        grid=(b, ATT_KV, l // tq),
        in_specs=specs,
        out_specs=pl.BlockSpec((None, 1, tq, hw), lambda bi, kv, i: (slab, bi, i, kv)),
        scratch_shapes=[pltpu.VMEM((1, 4 * tq), F32), pltpu.VMEM((ATT_VD + VT_ONES, 4 * tq), F32)],
        input_output_aliases=aliases,
        compiler_params=_cparams(("parallel", "parallel", "arbitrary")),
        name="diff_attn_self",
    )(*args)


def _attn_cached_kernel(cls_ref, q_ref, ck_ref, cv_ref, kn_ref, vn_ref, bp_ref, bn_ref, lam_ref, gs_ref,
                        o_ref, m_ref, l_ref, acc_ref, *, tq, tk, nt, lam_init):
    del cls_ref
    j = pl.program_id(1)

    @pl.when(j == 0)
    def _():
        m_ref[...] = jnp.full(m_ref.shape, NEG, F32)
        l_ref[...] = jnp.zeros(l_ref.shape, F32)
        acc_ref[...] = jnp.zeros(acc_ref.shape, F32)

    dn_t = (((1,), (1,)), ((), ()))

    def update(k_all, v_of, bias_at):
        kvs = range(ATT_KV)
        ss = [lax.dot_general(_q4(q_ref[0, :, kv * 2 * LANES:(kv + 1) * 2 * LANES]),
                              k_all[:, kv * LANES:(kv + 1) * LANES].astype(BF16), dn_t,
                              preferred_element_type=F32) for kv in kvs]
        pa = [_osm_softmax(ss[kv], bias_at(2 * kv), bias_at(2 * kv + 1), m_ref.at[kv], l_ref.at[kv])
              for kv in kvs]
        for kv in kvs:
            p, alpha = pa[kv]
            acc_ref[kv] = alpha * acc_ref[kv] + jnp.dot(p, v_of(kv).astype(BF16), preferred_element_type=F32)

    @pl.when(j < nt)
    def _():
        update(ck_ref[...], lambda kv: cv_ref[pl.ds(kv, tk, stride=ATT_KV), :], lambda h: bp_ref[0, h])

    @pl.when(j == nt)
    def _():
        update(kn_ref[0], lambda kv: vn_ref[0, :, kv * LANES:(kv + 1) * LANES], lambda h: bn_ref[0, h])
        outs = [_osm_finish(l_ref.at[kv], acc_ref.at[kv], lam_ref, gs_ref, lam_init, tq)
                for kv in range(ATT_KV)]
        o_ref[0] = jnp.concatenate(outs, axis=-1).astype(o_ref.dtype)


def _attn_cached(qn, knb, vb, cache_k, cache_v, layer, bias_past, cls_tbl, bias_new, lam_p, g_subln,
                 lam_init, *, tk, ybuf=None, slab=0):
    b, l, _ = qn.shape
    past = cache_k.shape[2]
    nt = past // tk
    kw = ATT_KV * 2 * ATT_D
    ck = cache_k.reshape(cache_k.shape[0], b, past, kw)
    cv = cache_v.reshape(cache_v.shape[0], b, past * ATT_KV, ATT_VD)
    args = [qn, ck, cv, knb, vb, bias_past, bias_new, lam_p, g_subln.reshape(1, ATT_VD)]
    specs = [
        pl.BlockSpec((1, l, ATT_H * 2 * ATT_D), lambda bi, j, cls: (bi, 0, 0)),
        pl.BlockSpec((None, None, tk, kw), lambda bi, j, cls: (layer, bi, jnp.minimum(j, nt - 1), 0)),
        pl.BlockSpec((None, None, tk * ATT_KV, ATT_VD),
                     lambda bi, j, cls: (layer, bi, jnp.minimum(j, nt - 1), 0)),
        pl.BlockSpec((1, l, kw), lambda bi, j, cls: (bi, 0, 0)),
        pl.BlockSpec((1, l, kw), lambda bi, j, cls: (bi, 0, 0)),
        pl.BlockSpec((1, ATT_H, l, tk), lambda bi, j, cls: (cls[jnp.minimum(j, nt - 1)], 0, 0, 0)),
        pl.BlockSpec((1, ATT_H, l, l), lambda bi, j, cls: (0, 0, 0, 0)),
        pl.BlockSpec((4, ATT_D), lambda bi, j, cls: (0, 0)),
        pl.BlockSpec((1, ATT_VD), lambda bi, j, cls: (0, 0)),
    ]
    yshape, kern, aliases = _branch_out(
        ybuf, slab, b, l, args, specs,
        functools.partial(_attn_cached_kernel, tq=l, tk=tk, nt=nt, lam_init=lam_init), n_prefetch=1)
    grid_spec = pltpu.PrefetchScalarGridSpec(
        num_scalar_prefetch=1,
        grid=(b, nt + 1),
        in_specs=specs,
        out_specs=pl.BlockSpec((None, 1, l, ATT_H * ATT_VD), lambda bi, j, cls: (slab, bi, 0, 0)),
        scratch_shapes=[pltpu.VMEM((ATT_KV, 4 * l, LANES), F32), pltpu.VMEM((ATT_KV, 4 * l, LANES), F32),
                        pltpu.VMEM((ATT_KV, 4 * l, ATT_VD), F32)],
    )
    return pl.pallas_call(
        kern,
        out_shape=yshape,
        grid_spec=grid_spec,
        input_output_aliases=aliases,
        compiler_params=_cparams(("parallel", "arbitrary")),
        name="diff_attn_cached",
    )(cls_tbl, *args)


def _xattn_kernel(q_ref, mk_ref, mv_ref, o_ref):
    dn_t = (((1,), (1,)), ((), ()))
    outs = []
    for h in range(X_H):
        sl = slice(h * X_D, (h + 1) * X_D)
        s = lax.dot_general(q_ref[0, :, sl], mk_ref[0, :, sl].astype(BF16), dn_t,
                            preferred_element_type=F32) * (X_D ** -0.5)
        s = s - jnp.max(s, axis=-1, keepdims=True)
        p = jnp.exp(s)
        p = p / jnp.sum(p, axis=-1, keepdims=True)
        outs.append(jnp.dot(p.astype(BF16), mv_ref[0, :, sl].astype(BF16), preferred_element_type=F32))
    o_ref[0] = jnp.concatenate(outs, axis=-1).astype(o_ref.dtype)


def _xattn(q, mk, mv, *, tq):
    b, l, w = q.shape
    mlen = mk.shape[1]
    tq = min(tq, l)
    assert l % tq == 0
    return pl.pallas_call(
        _xattn_kernel,
        out_shape=jax.ShapeDtypeStruct((b, l, w), BF16),
        grid=(b, l // tq),
        in_specs=[
            pl.BlockSpec((1, tq, w), lambda i, c: (i, c, 0)),
            pl.BlockSpec((1, mlen, w), lambda i, c: (i, 0, 0)),
            pl.BlockSpec((1, mlen, w), lambda i, c: (i, 0, 0)),
        ],
        out_specs=pl.BlockSpec((1, tq, w), lambda i, c: (i, c, 0)),
        compiler_params=_cparams(("parallel", "parallel")),
        name="mem_cross_attn",
    )(q, mk, mv)


def _prep_weights(p, depth):
    layers = []
    off_z, off_xbc = 0, SSM_D
    off_dt = off_xbc + XBC_W
    off_pool = off_dt + SSM_H
    off_q = off_pool + BRANCH_W
    off_k = off_q + ATT_H * 2 * ATT_D
    off_v = off_k + ATT_KV * 2 * ATT_D
    off_gu = off_v + ATT_KV * ATT_VD
    off_gv = off_gu + BRANCH_W
    in_w = off_gv + BRANCH_W
    for l in range(depth):
        w = p['w_in'][l]
        d = w.shape[0]
        w_in = jnp.concatenate([
            w[:, off_xbc:off_dt], w[:, off_z:off_xbc], w[:, off_pool:off_q], w[:, off_q:off_k],
            w[:, off_gu:off_gv], w[:, off_gv:in_w], w[:, off_k:off_v], w[:, off_v:off_gu],
            w[:, off_dt:off_pool], jnp.zeros((d, LANES - SSM_H), w.dtype)], axis=1).astype(BF16)
        pad = lambda v: jnp.pad(v, (0, LANES - SSM_H)).reshape(1, LANES)
        layers.append(dict(
            g_ffn1=p['g_ffn1'][l], ffn1_in=p['w_ffn1_in'][l].astype(BF16), ffn1_out=p['w_ffn1_out'][l].astype(BF16),
            g_mix=p['g_mix'][l], w_in=w_in,
            conv_w=p['conv_w'][l], conv_b=p['conv_b'][l].reshape(1, XBC_W),
            dt_bias=pad(p['dt_bias'][l]), a_log=pad(p['a_log'][l]),
            d_skip=jnp.repeat(p['d_skip'][l], SSM_P).reshape(1, SSM_D), g_ssd=p['g_ssd'][l].reshape(1, SSM_D),
            w_pool=p['w_pool'][l].astype(BF16), pool_scale=p['pool_scale'][l].reshape(1, BRANCH_W),
            g_q=p['g_q'][l], g_k=p['g_k'][l], lam=p['lam'][l], g_subln=p['g_subln'][l],
            g_gv=p['g_gv'][l].reshape(1, BRANCH_W), w_sp=p['w_sp'][l], b_sp=p['b_sp'][l],
            w_gate=p['w_gate'][l].astype(BF16), b_gate=p['b_gate'][l], w_branch=p['w_branch'][l].astype(BF16),
            w_out=p['w_out'][l].astype(BF16),
            g_x=p['g_x'][l], w_xq=p['w_xq'][l].astype(BF16), g_xq=p['g_xq'][l], w_xo=p['w_xo'][l].astype(BF16),
            g_ffn2=p['g_ffn2'][l], ffn2_in=p['w_ffn2_in'][l].astype(BF16), ffn2_out=p['w_ffn2_out'][l].astype(BF16),
            g_post=p['g_post'][l],
            g_mem=p['g_mem'][l], w_mk=p['w_mk'][l].astype(BF16), w_mv=p['w_mv'][l].astype(BF16), g_xk=p['g_xk'][l],
        ))
    return layers


def _run_trunk(x, cache_k, cache_v, mem_k, mem_v, ssm0, conv0, pool0, layers, rel_bias):
    b, l, d = x.shape
    m = b * l
    depth = len(layers)
    cached = cache_k is not None
    assert l >= POOL_BUF and l >= CONV_W - 1
    if cached:
        past = cache_k.shape[2]
        tk = min(1024, past)
        assert past % tk == 0 and past % CHUNK == 0 and l <= CHUNK and (past % GM_CHUNK == 0)
        nt = past // tk
        offs, cls = [FAR_OFFSET], []
        for j in range(nt):
            off = j * tk - past
            if _is_far(off, l, tk):
                cls.append(0)
            else:
                offs.append(off)
                cls.append(len(offs) - 1)
        bias_past = _bias_tiles(rel_bias, offs, l, tk)
        cls_tbl = jnp.asarray(np.asarray(cls, np.int32))
        bias_new = _bias_tiles(rel_bias, [0], l, l)
    else:
        past = 0
        tq = min(512, l)
        assert l % tq == 0 and tq % CHUNK == 0
        offs = [0]
        if l > tq:
            offs.append(-tq)
        if l > 2 * tq:
            assert _is_far(-2 * tq, tq, tq)
            offs.append(FAR_OFFSET)
        bias_self = _bias_tiles(rel_bias, offs, tq, tq, keys_major=True)

    h = x.reshape(m, d)
    new_k, new_v, new_ssm, new_conv, new_pool, new_gv = [], [], [], [], [], []
    for li, w in enumerate(layers):
        lam_init = 0.8 - 0.6 * math.exp(-0.3 * li)
        h = _ffn(h, w['g_ffn1'], w['ffn1_in'], w['ffn1_out'], tm=1024)
        proj = _mm(h, w['w_in'], g=w['g_mix'], out_dtype=F32, tm=1024, tn=1664, name="in_proj")
        proj = proj.reshape(b, l, P_W)

        ys, s_ssm = _ssd(proj, conv0[li] if cached else None, ssm0[li] if cached else None,
                         w['conv_w'], w['conv_b'], w['dt_bias'], w['a_log'], w['d_skip'], w['g_ssd'], q=128)
        ys = _pool(proj, pool0[li] if cached else None, past, w['w_pool'], w['pool_scale'], t=256,
                   ybuf=ys, slab=1)
        if cached:
            qn, kn, knb, vb = _qknorm(proj, w['g_q'], w['g_k'], t=512)
            ys = _attn_cached(qn, knb, vb, cache_k, cache_v, li, bias_past, cls_tbl, bias_new,
                              w['lam'], w['g_subln'], lam_init, tk=tk, ybuf=ys, slab=2)
        else:
            qt, kn, knb, vt = _qknorm_t(proj, w['g_q'], w['g_k'], t=tq)
            ys = _attn_self(qt, knb, vt, bias_self, w['lam'], w['g_subln'], lam_init, tq=tq, ybuf=ys, slab=2)
        ys, v_gm = _gmlp(proj, w['g_gv'], w['w_sp'], w['b_sp'], keep_vn=cached, ybuf=ys, slab=3)

        merged = _merge(h, w['g_mix'], ys.reshape(N_BRANCH, m, BRANCH_W), w['w_gate'], w['b_gate'], w['w_branch'])
        h = _mm(merged, w['w_out'], res=h, out_dtype=F32, tm=512, tn=2048, name="out_proj", w_resident=True)

        qx = _mm(h, w['w_xq'], g=w['g_x'], gh=w['g_xq'], out_dtype=BF16, tm=512, tn=512, name="xattn_q")
        o = _xattn(qx.reshape(b, l, X_H * X_D), mem_k[li].reshape(b, -1, X_H * X_D),
                   mem_v[li].reshape(b, -1, X_H * X_D), tq=512)
        h = _ffn(h, w['g_ffn2'], w['ffn2_in'], w['ffn2_out'], post_g=w['g_post'],
                 pre=(o.reshape(m, X_H * X_D), w['w_xo']))

        new_k.append(kn.reshape(b, l, ATT_KV, 2, ATT_D))
        new_v.append(proj[:, :, P_V:P_V + ATT_KV * ATT_VD].reshape(b, l, ATT_KV, ATT_VD))
        new_ssm.append(s_ssm)
        new_conv.append(proj[:, l - (CONV_W - 1):, P_XBC:P_XBC + XBC_W])
        new_pool.append(proj[:, l - POOL_BUF:, P_POOL:P_POOL + BRANCH_W])
        new_gv.append(v_gm)
    return (h.reshape(b, l, d), jnp.stack(new_k), jnp.stack(new_v), jnp.stack(new_ssm),
            jnp.stack(new_conv), jnp.stack(new_pool), jnp.stack(new_gv) if cached else None)


def kernel(x_prompt, x_sample, cache_attn_k, cache_attn_v, cache_mem_k, cache_mem_v, state_ssm, state_conv, state_pool, mem_prompt, g_ffn1, w_ffn1_in, w_ffn1_out, g_mix, w_in, conv_w, conv_b, dt_bias, a_log, d_skip, g_ssd, w_pool, pool_scale, g_q, g_k, lam, g_subln, rel_bias, g_gv, w_sp, b_sp, w_gate, b_gate, w_branch, w_out, g_x, w_xq, g_xq, g_mem, w_mk, w_mv, g_xk, w_xo, g_ffn2, w_ffn2_in, w_ffn2_out, g_post):
    p = dict(g_ffn1=g_ffn1, w_ffn1_in=w_ffn1_in, w_ffn1_out=w_ffn1_out, g_mix=g_mix, w_in=w_in,
             conv_w=conv_w, conv_b=conv_b, dt_bias=dt_bias, a_log=a_log, d_skip=d_skip, g_ssd=g_ssd,
             w_pool=w_pool, pool_scale=pool_scale, g_q=g_q, g_k=g_k, lam=lam, g_subln=g_subln,
             g_gv=g_gv, w_sp=w_sp, b_sp=b_sp, w_gate=w_gate, b_gate=b_gate, w_branch=w_branch,
             w_out=w_out, g_x=g_x, w_xq=w_xq, g_xq=g_xq, g_mem=g_mem, w_mk=w_mk, w_mv=w_mv, g_xk=g_xk,
             w_xo=w_xo, g_ffn2=g_ffn2, w_ffn2_in=w_ffn2_in, w_ffn2_out=w_ffn2_out, g_post=g_post)
    depth = w_in.shape[0]
    layers = _prep_weights(p, depth)

    bp, mlen, d = mem_prompt.shape
    mem2 = mem_prompt.reshape(bp * mlen, d)
    mks, mvs = [], []
    for w in layers:
        mk = _mm(mem2, w['w_mk'], g=w['g_mem'], gh=w['g_xk'], out_dtype=F32, tm=512, tn=512, name="mem_k")
        mv = _mm(mem2, w['w_mv'], g=w['g_mem'], out_dtype=F32, tm=512, tn=512, name="mem_v")
        mks.append(mk.reshape(bp, mlen, X_H, X_D))
        mvs.append(mv.reshape(bp, mlen, X_H, X_D))
    p_mem_k = jnp.stack(mks)
    p_mem_v = jnp.stack(mvs)

    y_prompt, p_attn_k, p_attn_v, p_ssm, p_conv, p_pool, _ = _run_trunk(
        x_prompt, None, None, p_mem_k, p_mem_v, None, None, None, layers, rel_bias)
    y_sample, s_attn_k, s_attn_v, s_ssm, s_conv, s_pool, s_gmlp_v = _run_trunk(
        x_sample, cache_attn_k, cache_attn_v, cache_mem_k, cache_mem_v, state_ssm, state_conv,
        state_pool, layers, rel_bias)

    return (y_prompt, y_sample, p_attn_k, p_attn_v, p_mem_k, p_mem_v, p_ssm, p_conv, p_pool,
            s_attn_k, s_attn_v, s_ssm, s_conv, s_pool, s_gmlp_v)
```

```python
import functools
import math

import jax
import jax.numpy as jnp
import numpy as np
from jax import lax
from jax.experimental import pallas as pl
from jax.experimental.pallas import tpu as pltpu

F32 = jnp.float32
BF16 = jnp.bfloat16
EPS = 1e-6
NEG = -1e30

VMEM_LIMIT_BYTES = 56 * 1024 * 1024
LANES = 128

BRANCH_W = 1024
SSM_P = 64
SSM_H = 16
SSM_G = 4
SSM_N = 128
SSM_D = SSM_H * SSM_P
CONV_W = 4
XBC_W = SSM_D + 2 * SSM_G * SSM_N
POOL_WINDOWS = (2, 4, 8, 16)
POOL_GW = BRANCH_W // len(POOL_WINDOWS)
POOL_BUF = max(POOL_WINDOWS) - 1
POOL_PAD = max(POOL_WINDOWS) // 2
ATT_H = 8
ATT_KV = 4
ATT_D = 64
ATT_VD = 128
CHUNK = 64
N_BUCKETS = 32
MAX_DIST = 128
GM_CHUNK = 128
GM_G = 4
GM_GW = BRANCH_W // GM_G
X_H = 4
X_D = 128
VT_ONES = 16

P_XBC = 0
P_Z = 2048
P_POOL = 3072
P_Q = 4096
P_GU = 5120
P_GV = 6144
P_K = 7168
P_V = 7680
P_DT = 8192
P_W = 8320


def _cparams(sem):
    return pltpu.CompilerParams(dimension_semantics=sem, vmem_limit_bytes=VMEM_LIMIT_BYTES)


N_BRANCH = 4


def _skip_ref(fn, pos):
    def wrapped(*refs):
        return fn(*refs[:pos], *refs[pos + 1:])
    return wrapped


def _branch_out(ybuf, slab, b, l, args, specs, kernel_fn, n_prefetch=0):
    shape = jax.ShapeDtypeStruct((N_BRANCH, b, l, BRANCH_W), BF16)
    if ybuf is None:
        return shape, kernel_fn, {}
    assert slab > 0 and ybuf.shape == shape.shape and ybuf.dtype == shape.dtype
    pos = n_prefetch + len(args)
    args.append(ybuf)
    specs.append(pl.BlockSpec(memory_space=pl.ANY))
    return shape, _skip_ref(kernel_fn, pos), {pos: 0}


def _rms(xf, g):
    ms = jnp.mean(xf * xf, axis=-1, keepdims=True)
    return xf * lax.rsqrt(ms + EPS) * g


def _silu(x):
    return x * jax.nn.sigmoid(x)


def _mm_kernel(*refs, norm, head_norm, residual):
    it = iter(refs)
    x_ref = next(it)
    g_ref = next(it) if norm else None
    w_ref = next(it)
    gh_ref = next(it) if head_norm else None
    res_ref = next(it) if residual else None
    o_ref = next(it)
    xn_ref = next(it) if norm else None

    if norm:
        @pl.when(pl.program_id(1) == 0)
        def _():
            xn_ref[...] = _rms(x_ref[...].astype(F32), g_ref[...]).astype(BF16)
        xb = xn_ref[...]
    else:
        xb = x_ref[...].astype(BF16)
    acc = jnp.dot(xb, w_ref[...], preferred_element_type=F32)
    if head_norm:
        parts = []
        for c in range(acc.shape[1] // LANES):
            parts.append(_rms(acc[:, c * LANES:(c + 1) * LANES], gh_ref[...]))
        acc = jnp.concatenate(parts, axis=-1)
    if residual:
        acc = res_ref[...] + acc
    o_ref[...] = acc.astype(o_ref.dtype)


def _mm(x, w, *, g=None, gh=None, res=None, out_dtype, tm, tn, name, w_resident=False):
    m, k = x.shape
    n = w.shape[1]
    tm = min(tm, m)
    tn = min(tn, n)
    assert m % tm == 0 and n % tn == 0
    norm, head_norm, residual = g is not None, gh is not None, res is not None
    assert not (norm and w_resident)
    if w_resident:
        grid = (n // tn, m // tm)
        ij = lambda a, b: (b, a)
    else:
        grid = (m // tm, n // tn)
        ij = lambda a, b: (a, b)
    args, specs = [x], [pl.BlockSpec((tm, k), lambda a, b: (ij(a, b)[0], 0))]
    if norm:
        args.append(g.reshape(1, k))
        specs.append(pl.BlockSpec((1, k), lambda a, b: (0, 0)))
    args.append(w)
    specs.append(pl.BlockSpec((k, tn), lambda a, b: (0, ij(a, b)[1])))
    if head_norm:
        args.append(gh.reshape(1, LANES))
        specs.append(pl.BlockSpec((1, LANES), lambda a, b: (0, 0)))
    if residual:
        args.append(res)
        specs.append(pl.BlockSpec((tm, tn), lambda a, b: ij(a, b)))
    scratch = [pltpu.VMEM((tm, k), BF16)] if norm else []
    return pl.pallas_call(
        functools.partial(_mm_kernel, norm=norm, head_norm=head_norm, residual=residual),
        out_shape=jax.ShapeDtypeStruct((m, n), out_dtype),
        grid=grid,
        in_specs=specs,
        out_specs=pl.BlockSpec((tm, tn), lambda a, b: ij(a, b)),
        scratch_shapes=scratch,
        compiler_params=_cparams(("parallel", "arbitrary")),
        name=name,
    )(*args)


def _ffn_kernel(*refs, post, pre):
    it = iter(refs)
    x_ref, g_ref, wg_ref, wu_ref, wo_ref = (next(it) for _ in range(5))
    gp_ref = next(it) if post else None
    a_ref, wa_ref = (next(it), next(it)) if pre else (None, None)
    o_ref, xn_ref = next(it), next(it)
    acc_ref = next(it) if pre else o_ref
    f = pl.program_id(1)

    @pl.when(f == 0)
    def _():
        x = x_ref[...]
        if pre:
            x = x + jnp.dot(a_ref[...], wa_ref[...], preferred_element_type=F32)
            o_ref[...] = x
        xn_ref[...] = _rms(x, g_ref[...]).astype(BF16)
        acc_ref[...] = jnp.zeros_like(acc_ref)

    xb = xn_ref[...]
    gate = jnp.dot(xb, wg_ref[...], preferred_element_type=F32)
    up = jnp.dot(xb, wu_ref[...], preferred_element_type=F32)
    mid = (_silu(gate) * up).astype(BF16)
    acc_ref[...] += jnp.dot(mid, wo_ref[...], preferred_element_type=F32)

    @pl.when(f == pl.num_programs(1) - 1)
    def _():
        h = (o_ref[...] if pre else x_ref[...]) + 0.5 * acc_ref[...]
        if post:
            h = _rms(h, gp_ref[...])
        o_ref[...] = h


def _ffn(x, g, w_in, w_out, *, post_g=None, pre=None, tm=512, tf=512):
    m, d = x.shape
    ff = w_out.shape[0]
    tm = min(tm, m)
    assert m % tm == 0 and ff % tf == 0
    nf = ff // tf
    post = post_g is not None
    args = [x, g.reshape(1, d), w_in, w_in, w_out]
    specs = [
        pl.BlockSpec((tm, d), lambda i, f: (i, 0)),
        pl.BlockSpec((1, d), lambda i, f: (0, 0)),
        pl.BlockSpec((d, tf), lambda i, f: (0, f)),
        pl.BlockSpec((d, tf), lambda i, f: (0, nf + f)),
        pl.BlockSpec((tf, d), lambda i, f: (f, 0)),
    ]
    if post:
        args.append(post_g.reshape(1, d))
        specs.append(pl.BlockSpec((1, d), lambda i, f: (0, 0)))
    if pre is not None:
        a, wa = pre
        ka = a.shape[1]
        args += [a, wa]
        specs += [pl.BlockSpec((tm, ka), lambda i, f: (i, 0)), pl.BlockSpec((ka, d), lambda i, f: (0, 0))]
    return pl.pallas_call(
        functools.partial(_ffn_kernel, post=post, pre=pre is not None),
        out_shape=jax.ShapeDtypeStruct((m, d), F32),
        grid=(m // tm, nf),
        in_specs=specs,
        out_specs=pl.BlockSpec((tm, d), lambda i, f: (i, 0)),
        scratch_shapes=[pltpu.VMEM((tm, d), BF16)] + ([pltpu.VMEM((tm, d), F32)] if pre is not None else []),
        compiler_params=_cparams(("parallel", "arbitrary")),
        name="swiglu_ffn",
    )(*args)


def _merge_kernel(h_ref, g_ref, y_ref, wg_ref, bg_ref, wb_ref, o_ref, hn_ref, acc_ref):
    j = pl.program_id(1)
    br = pl.program_id(2)

    @pl.when((j == 0) & (br == 0))
    def _():
        hn_ref[...] = _rms(h_ref[...], g_ref[...]).astype(BF16)

    @pl.when(br == 0)
    def _():
        acc_ref[...] = jnp.zeros_like(acc_ref)

    gate = jax.nn.sigmoid(jnp.dot(hn_ref[...], wg_ref[...], preferred_element_type=F32) + bg_ref[...])
    acc_ref[...] += gate * jnp.dot(y_ref[...], wb_ref[...], preferred_element_type=F32)

    @pl.when(br == pl.num_programs(2) - 1)
    def _():
        o_ref[...] = acc_ref[...].astype(o_ref.dtype)


def _merge(h, g, ys, w_gate, b_gate, w_branch, *, tm=1024, tn=1024):
    m, d = h.shape
    nb, bw, _ = w_branch.shape
    tm = min(tm, m)
    assert m % tm == 0 and d % tn == 0 and ys.shape == (nb, m, bw)
    return pl.pallas_call(
        _merge_kernel,
        out_shape=jax.ShapeDtypeStruct((m, d), BF16),
        grid=(m // tm, d // tn, nb),
        in_specs=[
            pl.BlockSpec((tm, d), lambda i, j, b: (i, 0)),
            pl.BlockSpec((1, d), lambda i, j, b: (0, 0)),
            pl.BlockSpec((None, tm, bw), lambda i, j, b: (b, i, 0)),
            pl.BlockSpec((None, d, tn), lambda i, j, b: (b, 0, j)),
            pl.BlockSpec((None, 1, tn), lambda i, j, b: (b, 0, j)),
            pl.BlockSpec((None, bw, tn), lambda i, j, b: (b, 0, j)),
        ],
        out_specs=pl.BlockSpec((tm, tn), lambda i, j, b: (i, j)),
        scratch_shapes=[pltpu.VMEM((tm, d), BF16), pltpu.VMEM((tm, tn), F32)],
        compiler_params=_cparams(("parallel", "arbitrary", "arbitrary")),
        name="branch_merge",
    )(h, g.reshape(1, d), ys, w_gate, b_gate.reshape(nb, 1, d), w_branch)


def _ssd_kernel(*refs, q, has_state):
    it = iter(refs)
    xbc_ref, z_ref, dt_ref = next(it), next(it), next(it)
    conv0_ref = next(it) if has_state else None
    s0_ref = next(it) if has_state else None
    cw_ref, cb_ref, dtb_ref, alog_ref, dskip_ref, gssd_ref = (next(it) for _ in range(6))
    y_ref, s_ref, cbuf = next(it), next(it), next(it)
    hist = 8

    @pl.when(pl.program_id(1) == 0)
    def _():
        cbuf[0:hist, :] = jnp.zeros((hist, XBC_W), F32)
        if has_state:
            cbuf[hist - (CONV_W - 1):hist, :] = conv0_ref[0]
            s_ref[0] = s0_ref[0]
        else:
            s_ref[0] = jnp.zeros(s_ref.shape[1:], F32)

    cbuf[hist:hist + q, :] = xbc_ref[0]
    conv = cb_ref[...]
    for k in range(CONV_W):
        lo = hist - (CONV_W - 1) + k
        conv = conv + cbuf[lo:lo + q, :] * cw_ref[k:k + 1, :]
    cbuf[0:hist, :] = cbuf[q:q + hist, :]
    xc = _silu(conv)
    xs = xc[:, :SSM_D]
    bm = xc[:, SSM_D:SSM_D + SSM_G * SSM_N]
    cm = xc[:, SSM_D + SSM_G * SSM_N:]

    dtl = dt_ref[0] + dtb_ref[...]
    dt = jnp.maximum(dtl, 0.0) + jnp.log1p(jnp.exp(-jnp.abs(dtl)))
    a = dt * (-jnp.exp(alog_ref[...]))
    ri = lax.broadcasted_iota(jnp.int32, (q, q), 0)
    ci = lax.broadcasted_iota(jnp.int32, (q, q), 1)
    tril = ri >= ci
    cum = jnp.dot(tril.astype(F32), a, preferred_element_type=F32,
                  precision=lax.Precision.HIGHEST)
    cum_t = cum.T
    dt_t = dt.T
    cum_last = cum[q - 1:q, :]
    w_state = jnp.exp(cum_last - cum) * dt
    ecum = jnp.exp(cum)
    ecl = jnp.exp(cum_last)

    lane_lo = lax.broadcasted_iota(jnp.int32, (q, LANES), 1) < SSM_P
    row_lo_t = lax.broadcasted_iota(jnp.int32, (LANES, q), 0) < SSM_P
    row_lo_s = lax.broadcasted_iota(jnp.int32, (LANES, SSM_N), 0) < SSM_P
    dn_t = (((1,), (1,)), ((), ()))

    y_parts = []
    cb = None
    heads_per_group = SSM_H // SSM_G
    for pp in range(SSM_H // 2):
        grp = (2 * pp) // heads_per_group
        xp = xs[:, pp * LANES:(pp + 1) * LANES]
        bg = bm[:, grp * SSM_N:(grp + 1) * SSM_N]
        cg = cm[:, grp * SSM_N:(grp + 1) * SSM_N]
        cg_b = cg.astype(BF16)
        if (2 * pp) % heads_per_group == 0:
            cb = lax.dot_general(cg_b, bg.astype(BF16), dn_t, preferred_element_type=F32)
        xp_t = xp.T
        s_pair = s_ref[0, pp * LANES:(pp + 1) * LANES, :]
        y_pair = None
        s_new = None
        for hh in range(2):
            hd = 2 * pp + hh
            seg = cum[:, hd:hd + 1] - cum_t[hd:hd + 1, :]
            mat = jnp.exp(jnp.where(tril, seg, NEG)) * cb * dt_t[hd:hd + 1, :]
            sel = lane_lo if hh == 0 else jnp.logical_not(lane_lo)
            x_h = jnp.where(sel, xp, 0.0).astype(BF16)
            yd = jnp.dot(mat.astype(BF16), x_h, preferred_element_type=F32)
            y_pair = yd if y_pair is None else y_pair + yd
            sel_t = row_lo_t if hh == 0 else jnp.logical_not(row_lo_t)
            xt_h = jnp.where(sel_t, xp_t, 0.0).astype(BF16)
            bs = (bg * w_state[:, hd:hd + 1]).astype(BF16)
            sn = jnp.dot(xt_h, bs, preferred_element_type=F32)
            s_new = sn if s_new is None else s_new + sn
        y_off = lax.dot_general(cg_b, s_pair.astype(BF16), dn_t, preferred_element_type=F32)
        e_pair = jnp.where(lane_lo, ecum[:, 2 * pp:2 * pp + 1], ecum[:, 2 * pp + 1:2 * pp + 2])
        y_parts.append(y_pair + y_off * e_pair)
        dec = jnp.where(row_lo_s, ecl[:, 2 * pp:2 * pp + 1], ecl[:, 2 * pp + 1:2 * pp + 2])
        s_ref[0, pp * LANES:(pp + 1) * LANES, :] = dec * s_pair + s_new

    y = jnp.concatenate(y_parts, axis=-1)
    y = y + dskip_ref[...] * xs
    y = y * _silu(z_ref[0])
    gw = SSM_D // SSM_G
    outs = [_rms(y[:, gi * gw:(gi + 1) * gw], gssd_ref[:, gi * gw:(gi + 1) * gw]) for gi in range(SSM_G)]
    y_ref[0] = jnp.concatenate(outs, axis=-1).astype(y_ref.dtype)


def _ssd(proj, conv0, s0, cw, cb, dtb, alog, dskip, gssd, *, q, ybuf=None, slab=0):
    b, l, _ = proj.shape
    q = min(q, l)
    assert l % q == 0 and q >= 8
    has_state = s0 is not None
    args = [proj, proj, proj]
    specs = [
        pl.BlockSpec((1, q, XBC_W), lambda i, c: (i, c, P_XBC // XBC_W)),
        pl.BlockSpec((1, q, SSM_D), lambda i, c: (i, c, P_Z // SSM_D)),
        pl.BlockSpec((1, q, LANES), lambda i, c: (i, c, P_DT // LANES)),
    ]
    if has_state:
        args += [conv0, s0.reshape(b, SSM_H * SSM_P, SSM_N)]
        specs += [
            pl.BlockSpec((1, CONV_W - 1, XBC_W), lambda i, c: (i, 0, 0)),
            pl.BlockSpec((1, SSM_H * SSM_P, SSM_N), lambda i, c: (i, 0, 0)),
        ]
    args += [cw, cb, dtb, alog, dskip, gssd]
    specs += [
        pl.BlockSpec((CONV_W, XBC_W), lambda i, c: (0, 0)),
        pl.BlockSpec((1, XBC_W), lambda i, c: (0, 0)),
        pl.BlockSpec((1, LANES), lambda i, c: (0, 0)),
        pl.BlockSpec((1, LANES), lambda i, c: (0, 0)),
        pl.BlockSpec((1, SSM_D), lambda i, c: (0, 0)),
        pl.BlockSpec((1, SSM_D), lambda i, c: (0, 0)),
    ]
    yshape, kern, aliases = _branch_out(ybuf, slab, b, l, args, specs,
                                        functools.partial(_ssd_kernel, q=q, has_state=has_state))
    y, s = pl.pallas_call(
        kern,
        out_shape=(yshape, jax.ShapeDtypeStruct((b, SSM_H * SSM_P, SSM_N), F32)),
        grid=(b, l // q),
        in_specs=specs,
        out_specs=(pl.BlockSpec((None, 1, q, SSM_D), lambda i, c: (slab, i, c, 0)),
                   pl.BlockSpec((1, SSM_H * SSM_P, SSM_N), lambda i, c: (i, 0, 0))),
        scratch_shapes=[pltpu.VMEM((q + 8, XBC_W), F32)],
        input_output_aliases=aliases,
        compiler_params=_cparams(("parallel", "arbitrary")),
        name="ssd_mixer",
    )(*args)
    return y, s.reshape(b, SSM_H, SSM_P, SSM_N)


def _pool_kernel(*refs, t, pos0, has_state):
    it = iter(refs)
    xp_ref = next(it)
    p0_ref = next(it) if has_state else None
    w_ref, sc_ref, y_ref, buf, lvl = next(it), next(it), next(it), next(it), next(it)
    hist = POOL_BUF + 1
    pad = POOL_PAD
    base = pad + hist
    n = hist + t
    ti = pl.program_id(1)

    @pl.when(ti == 0)
    def _():
        buf[0:base, :] = jnp.zeros((base, BRANCH_W), F32)
        lvl[0:pad, :] = jnp.zeros((pad, BRANCH_W), F32)
        if has_state:
            buf[pad + 1:base, :] = p0_ref[0]

    xp = xp_ref[0]
    buf[base:base + t, :] = xp
    pos = pos0 + ti * t + lax.broadcasted_iota(jnp.int32, (t, POOL_GW), 0)
    outs = []
    for gi, w in enumerate(POOL_WINDOWS):
        cs = slice(gi * POOL_GW, (gi + 1) * POOL_GW)
        src, k = buf, 1
        while 2 * k < w:
            lvl[pad:pad + n, cs] = src[pad:pad + n, cs] + src[pad - k:pad - k + n, cs]
            src, k = lvl, 2 * k
        s = src[base:base + t, cs] + src[base - k:base - k + t, cs]
        cnt = jnp.minimum(w, pos + 1).astype(F32)
        pooled = s / cnt - xp[:, cs]
        y = jnp.dot(pooled.astype(BF16), w_ref[gi], preferred_element_type=F32)
        outs.append(y * sc_ref[:, cs])
    y_ref[0] = jnp.concatenate(outs, axis=-1).astype(y_ref.dtype)
    buf[pad:base, :] = buf[pad + t:base + t, :]


def _pool(proj, pool0, pos0, w_pool, scale, *, t, ybuf=None, slab=0):
    b, l, _ = proj.shape
    t = min(t, l)
    assert l % t == 0 and t >= POOL_BUF + 1
    has_state = pool0 is not None
    args = [proj]
    specs = [pl.BlockSpec((1, t, BRANCH_W), lambda i, c: (i, c, P_POOL // BRANCH_W))]
    if has_state:
        args.append(pool0)
        specs.append(pl.BlockSpec((1, POOL_BUF, BRANCH_W), lambda i, c: (i, 0, 0)))
    args += [w_pool, scale]
    specs += [
        pl.BlockSpec((len(POOL_WINDOWS), POOL_GW, POOL_GW), lambda i, c: (0, 0, 0)),
        pl.BlockSpec((1, BRANCH_W), lambda i, c: (0, 0)),
    ]
    yshape, kern, aliases = _branch_out(
        ybuf, slab, b, l, args, specs, functools.partial(_pool_kernel, t=t, pos0=pos0, has_state=has_state))
    return pl.pallas_call(
        kern,
        out_shape=yshape,
        grid=(b, l // t),
        in_specs=specs,
        out_specs=pl.BlockSpec((None, 1, t, BRANCH_W), lambda i, c: (slab, i, c, 0)),
        scratch_shapes=[pltpu.VMEM((t + POOL_BUF + 1 + POOL_PAD, BRANCH_W), F32),
                        pltpu.VMEM((t + POOL_BUF + 1 + POOL_PAD, BRANCH_W), F32)],
        input_output_aliases=aliases,
        compiler_params=_cparams(("parallel", "arbitrary")),
        name="pool_mixer",
    )(*args)


def _gmlp_kernel(u_ref, v_ref, g_ref, w_ref, b_ref, y_ref, *rest, cl, nsub, keep_vn):
    tril = lax.broadcasted_iota(jnp.int32, (cl, cl), 0) >= lax.broadcasted_iota(jnp.int32, (cl, cl), 1)
    ws = [jnp.where(tril, w_ref[gi], 0.0).astype(BF16) for gi in range(GM_G)]
    for sub in range(nsub):
        rows = slice(sub * cl, (sub + 1) * cl)
        u = jax.nn.gelu(u_ref[0, rows, :], approximate=True)
        vn = _rms(jax.nn.gelu(v_ref[0, rows, :], approximate=True), g_ref[...])
        if keep_vn:
            rest[0][0, rows, :] = vn
        outs = []
        for gi in range(GM_G):
            s = jnp.dot(ws[gi], vn[:, gi * GM_GW:(gi + 1) * GM_GW].astype(BF16), preferred_element_type=F32)
            s = s + b_ref[:, gi:gi + 1]
            outs.append(u[:, gi * GM_GW:(gi + 1) * GM_GW] * s)
        y_ref[0, rows, :] = jnp.concatenate(outs, axis=-1).astype(y_ref.dtype)


def _gmlp(proj, g_gv, w_sp, b_sp, *, keep_vn, ybuf=None, slab=0):
    b, l, _ = proj.shape
    cl = min(GM_CHUNK, l)
    assert l % cl == 0
    nsub = math.gcd(l // cl, 4)
    t = cl * nsub
    w = w_sp[:, :cl, :cl]
    bt = b_sp[:, :cl].T
    args = [proj, proj, g_gv, w, bt]
    specs = [
        pl.BlockSpec((1, t, BRANCH_W), lambda i, c: (i, c, P_GU // BRANCH_W)),
        pl.BlockSpec((1, t, BRANCH_W), lambda i, c: (i, c, P_GV // BRANCH_W)),
        pl.BlockSpec((1, BRANCH_W), lambda i, c: (0, 0)),
        pl.BlockSpec((GM_G, cl, cl), lambda i, c: (0, 0, 0)),
        pl.BlockSpec((cl, GM_G), lambda i, c: (0, 0)),
    ]
    yshape, kern, aliases = _branch_out(
        ybuf, slab, b, l, args, specs, functools.partial(_gmlp_kernel, cl=cl, nsub=nsub, keep_vn=keep_vn))
    out_shape = [yshape]
    out_specs = [pl.BlockSpec((None, 1, t, BRANCH_W), lambda i, c: (slab, i, c, 0))]
    if keep_vn:
        out_shape.append(jax.ShapeDtypeStruct((b, l, BRANCH_W), F32))
        out_specs.append(pl.BlockSpec((1, t, BRANCH_W), lambda i, c: (i, c, 0)))
    outs = pl.pallas_call(
        kern,
        out_shape=tuple(out_shape),
        grid=(b, l // t),
        in_specs=specs,
        out_specs=tuple(out_specs),
        input_output_aliases=aliases,
        compiler_params=_cparams(("parallel", "parallel")),
        name="gmlp_mixer",
    )(*args)
    return outs[0], (outs[1] if keep_vn else None)


def _norm_halves(x, g):
    lo = lax.broadcasted_iota(jnp.int32, x.shape, 1) < ATT_D
    xx = x * x
    s_lo = jnp.sum(jnp.where(lo, xx, 0.0), axis=-1, keepdims=True)
    s_hi = jnp.sum(jnp.where(lo, 0.0, xx), axis=-1, keepdims=True)
    ms = jnp.where(lo, s_lo, s_hi) * (1.0 / ATT_D)
    return x * lax.rsqrt(ms + EPS) * g


def _qknorm_kernel(q_ref, k_ref, v_ref, gq_ref, gk_ref, qn_ref, kn_ref, knb_ref, vb_ref):
    scale = ATT_D ** -0.5
    q = q_ref[0]
    qs = [_norm_halves(q[:, c * LANES:(c + 1) * LANES], gq_ref[...]) * scale
          for c in range(q.shape[1] // LANES)]
    qn_ref[0] = jnp.concatenate(qs, axis=-1).astype(qn_ref.dtype)
    k = k_ref[0]
    ks = [_norm_halves(k[:, c * LANES:(c + 1) * LANES], gk_ref[...]) for c in range(k.shape[1] // LANES)]
    kn = jnp.concatenate(ks, axis=-1)
    kn_ref[0] = kn
    knb_ref[0] = kn.astype(BF16)
    vb_ref[0] = v_ref[0].astype(BF16)


def _qknorm(proj, g_q, g_k, *, t):
    b, l, _ = proj.shape
    t = min(t, l)
    assert l % t == 0
    qw, kw = ATT_H * 2 * ATT_D, ATT_KV * 2 * ATT_D
    gq = jnp.tile(g_q, 2).reshape(1, LANES)
    gk = jnp.tile(g_k, 2).reshape(1, LANES)
    return pl.pallas_call(
        _qknorm_kernel,
        out_shape=(jax.ShapeDtypeStruct((b, l, qw), BF16), jax.ShapeDtypeStruct((b, l, kw), F32),
                   jax.ShapeDtypeStruct((b, l, kw), BF16), jax.ShapeDtypeStruct((b, l, kw), BF16)),
        grid=(b, l // t),
        in_specs=[
            pl.BlockSpec((1, t, qw), lambda i, c: (i, c, P_Q // qw)),
            pl.BlockSpec((1, t, kw), lambda i, c: (i, c, P_K // kw)),
            pl.BlockSpec((1, t, kw), lambda i, c: (i, c, P_V // kw)),
            pl.BlockSpec((1, LANES), lambda i, c: (0, 0)),
            pl.BlockSpec((1, LANES), lambda i, c: (0, 0)),
        ],
        out_specs=(pl.BlockSpec((1, t, qw), lambda i, c: (i, c, 0)),
                   pl.BlockSpec((1, t, kw), lambda i, c: (i, c, 0)),
                   pl.BlockSpec((1, t, kw), lambda i, c: (i, c, 0)),
                   pl.BlockSpec((1, t, kw), lambda i, c: (i, c, 0))),
        compiler_params=_cparams(("parallel", "parallel")),
        name="qk_norm",
    )(proj, proj, proj, gq, gk)


def _bias_kernel(off_ref, tbl_ref, o_ref, *, tq, tk, keys_major):
    c = pl.program_id(0)
    h = pl.program_id(1)
    off = off_ref[c]
    shape = (tk, tq) if keys_major else (tq, tk)
    row = lax.broadcasted_iota(jnp.int32, shape, 1 if keys_major else 0)
    col = lax.broadcasted_iota(jnp.int32, shape, 0 if keys_major else 1)
    rel = off + col - row
    nb = N_BUCKETS // 2
    max_exact = nb // 2
    n = jnp.abs(rel)
    large = max_exact + (jnp.log(jnp.maximum(n, 1).astype(F32) / max_exact)
                         / math.log(MAX_DIST / max_exact) * (nb - max_exact)).astype(jnp.int32)
    large = jnp.minimum(large, nb - 1)
    bucket = jnp.where(rel > 0, nb, 0) + jnp.where(n < max_exact, n, large)
    val = jnp.zeros(shape, F32)
    for bk in range(N_BUCKETS):
        val = jnp.where(bucket == bk, tbl_ref[bk * ATT_H + h], val)
    visible = jnp.right_shift(off + col, 6) <= jnp.right_shift(row, 6)
    o_ref[0, 0] = jnp.where(visible, val, NEG)


def _bias_tiles(rel_bias, offsets, tq, tk, keys_major=False):
    assert CHUNK == 64
    offs = jnp.asarray(np.asarray(offsets, np.int32))
    n_cls = len(offsets)
    shape = (tk, tq) if keys_major else (tq, tk)
    return pl.pallas_call(
        functools.partial(_bias_kernel, tq=tq, tk=tk, keys_major=keys_major),
        out_shape=jax.ShapeDtypeStruct((n_cls, ATT_H) + shape, F32),
        grid=(n_cls, ATT_H),
        in_specs=[pl.BlockSpec(memory_space=pltpu.SMEM), pl.BlockSpec(memory_space=pltpu.SMEM)],
        out_specs=pl.BlockSpec((1, 1) + shape, lambda c, h: (c, h, 0, 0)),
        compiler_params=_cparams(("parallel", "parallel")),
        name="rel_bias_tiles",
    )(offs, rel_bias.reshape(-1))


FAR_OFFSET = -(1 << 20)


def _is_far(off, tq, tk):
    nb = N_BUCKETS // 2
    max_rel = off + tk - 1
    if max_rel >= 0:
        return False
    n_min = -max_rel
    sat = (nb // 2) * (MAX_DIST / (nb // 2)) ** ((nb - 1 - nb // 2) / (nb - nb // 2))
    return n_min >= math.ceil(sat) + 1


def _q4(q2):
    lo = lax.broadcasted_iota(jnp.int32, (q2.shape[0], LANES), 1) < ATT_D
    parts = []
    for r in range(2):
        qr = q2[:, r * LANES:(r + 1) * LANES]
        parts.append(jnp.where(lo, qr, jnp.zeros_like(qr)))
        parts.append(jnp.where(lo, jnp.zeros_like(qr), qr))
    return jnp.concatenate(parts, axis=0)


def _osm_softmax(s, b0, b1, m_ref, l_ref):
    s = s + jnp.concatenate([b0, b0, b1, b1], axis=0)
    m_prev = m_ref[...]
    m_new = jnp.maximum(m_prev, jnp.max(s, axis=-1, keepdims=True))
    alpha = jnp.exp(m_prev - m_new)
    p = jnp.exp(s - m_new[:, 0:1])
    l_ref[...] = alpha * l_ref[...] + jnp.sum(p, axis=-1, keepdims=True)
    m_ref[...] = m_new
    return p.astype(BF16), alpha


def _osm_finish(l_ref, acc_ref, lam_ref, gs_ref, lam_init, tq):
    lp = lam_ref[...]
    lam = (jnp.exp(jnp.sum(lp[0:1] * lp[1:2], axis=-1, keepdims=True))
           - jnp.exp(jnp.sum(lp[2:3] * lp[3:4], axis=-1, keepdims=True)) + lam_init)
    o = acc_ref[...] / l_ref[...]
    outs = []
    for r in range(2):
        d = o[(2 * r) * tq:(2 * r + 1) * tq] - lam * o[(2 * r + 1) * tq:(2 * r + 2) * tq]
        outs.append(_rms(d, gs_ref[...]) * (1.0 - lam_init))
    return jnp.concatenate(outs, axis=-1)


def _qknorm_t_kernel(q_ref, k_ref, v_ref, gq_ref, gk_ref, qt_ref, kn_ref, knb_ref, vt_ref):
    scale = ATT_D ** -0.5
    q = q_ref[0]
    t = q.shape[0]
    lo = lax.broadcasted_iota(jnp.int32, (LANES, t), 0) < ATT_D
    for kv in range(ATT_KV):
        cols = []
        for r in range(2):
            c = kv * 2 + r
            qr = q[:, c * LANES:(c + 1) * LANES].T
            xx = qr * qr
            ms = jnp.where(lo, jnp.sum(xx[:ATT_D], axis=0, keepdims=True),
                           jnp.sum(xx[ATT_D:], axis=0, keepdims=True)) * (1.0 / ATT_D)
            qr = qr * lax.rsqrt(ms + EPS) * gq_ref[...] * scale
            cols.append(jnp.where(lo, qr, 0.0))
            cols.append(jnp.where(lo, 0.0, qr))
        qt_ref[0, kv] = jnp.concatenate(cols, axis=1).astype(qt_ref.dtype)
    k = k_ref[0]
    ks = [_norm_halves(k[:, c * LANES:(c + 1) * LANES], gk_ref[...]) for c in range(k.shape[1] // LANES)]
    kn = jnp.concatenate(ks, axis=-1)
    kn_ref[0] = kn
    knb_ref[0] = kn.astype(BF16)
    v = v_ref[0]
    for kv in range(ATT_KV):
        vt_ref[0, kv, 0] = jnp.concatenate(
            [v[:, kv * LANES:(kv + 1) * LANES].T, jnp.ones((VT_ONES, t), F32)], axis=0).astype(vt_ref.dtype)


def _qknorm_t(proj, g_q, g_k, *, t):
    b, l, _ = proj.shape
    assert l % t == 0
    qw, kw = ATT_H * 2 * ATT_D, ATT_KV * 2 * ATT_D
    gq = jnp.tile(g_q, 2).reshape(LANES, 1)
    gk = jnp.tile(g_k, 2).reshape(1, LANES)
    return pl.pallas_call(
        _qknorm_t_kernel,
        out_shape=(jax.ShapeDtypeStruct((b, ATT_KV, LANES, 4 * l), BF16),
                   jax.ShapeDtypeStruct((b, l, kw), F32),
                   jax.ShapeDtypeStruct((b, l, kw), BF16),
                   jax.ShapeDtypeStruct((b, ATT_KV, l // t, ATT_VD + VT_ONES, t), BF16)),
        grid=(b, l // t),
        in_specs=[
            pl.BlockSpec((1, t, qw), lambda i, c: (i, c, P_Q // qw)),
            pl.BlockSpec((1, t, kw), lambda i, c: (i, c, P_K // kw)),
            pl.BlockSpec((1, t, kw), lambda i, c: (i, c, P_V // kw)),
            pl.BlockSpec((LANES, 1), lambda i, c: (0, 0)),
            pl.BlockSpec((1, LANES), lambda i, c: (0, 0)),
        ],
        out_specs=(pl.BlockSpec((1, ATT_KV, LANES, 4 * t), lambda i, c: (i, 0, 0, c)),
                   pl.BlockSpec((1, t, kw), lambda i, c: (i, c, 0)),
                   pl.BlockSpec((1, t, kw), lambda i, c: (i, c, 0)),
                   pl.BlockSpec((1, ATT_KV, 1, ATT_VD + VT_ONES, t), lambda i, c: (i, 0, c, 0, 0))),
        compiler_params=_cparams(("parallel", "parallel")),
        name="qk_norm_t",
    )(proj, proj, proj, gq, gk)


def _attn_self_kernel(qt_ref, k_ref, vt_ref, bias_ref, lam_ref, gs_ref, o_ref, m_ref, acc_ref,
                      *, tq, tk, lam_init, n_cls, group):
    i = pl.program_id(2)
    m_ref[...] = jnp.full(m_ref.shape, NEG, F32)
    acc_ref[...] = jnp.zeros(acc_ref.shape, F32)
    sub = LANES
    assert tq == tk and tq % sub == 0 and sub % CHUNK == 0

    def step(j, mode):
        kt = k_ref[0, pl.ds(pl.multiple_of(j * tk, tk), tk), :]
        vt = vt_ref[0, 0, j]
        cls = jnp.minimum(i - j, n_cls - 1)
        for c0 in range(0, 4, group):
            update(kt, vt, cls, mode, range(c0, c0 + group))

    def softmax_block(s, m_prev, crow):
        m_cur = jnp.max(s, axis=0, keepdims=True)
        if crow is None:
            m_new = jnp.maximum(m_prev, m_cur)
            return jnp.exp(s - m_new).astype(BF16), m_new
        m_new = jnp.maximum(m_prev, m_cur + crow)
        return jnp.exp(s - (m_new - crow)).astype(BF16), m_new

    def update(kt, vt, cls, mode, combos):
        cols = {c: slice(c * tq, (c + 1) * tq) for c in combos}
        ss = {c: jnp.dot(kt, qt_ref[0, 0, :, cs], preferred_element_type=F32) for c, cs in cols.items()}
        ps, alphas = {}, {}
        for c, cs in cols.items():
            s = ss[c]
            m_prev = m_ref[:, cs]
            if mode == "far":
                p, m_new = softmax_block(s, m_prev, bias_ref[cls, c // 2, 0:1, :])
            elif mode == "near":
                p, m_new = softmax_block(s + bias_ref[cls, c // 2], m_prev, None)
            else:
                pb, mb = [], []
                for qb in range(tq // sub):
                    rows, lanes = (qb + 1) * sub, slice(qb * sub, (qb + 1) * sub)
                    p_b, m_b = softmax_block(s[:rows, lanes] + bias_ref[cls, c // 2, :rows, lanes],
                                             m_prev[:, lanes], None)
                    if rows < tk:
                        p_b = jnp.concatenate([p_b, jnp.zeros((tk - rows, sub), BF16)], axis=0)
                    pb.append(p_b)
                    mb.append(m_b)
                p, m_new = jnp.concatenate(pb, axis=1), jnp.concatenate(mb, axis=1)
            alphas[c] = jnp.exp(m_prev - m_new)
            m_ref[:, cs] = m_new
            ps[c] = p
        for c, cs in cols.items():
            acc_ref[:, cs] = alphas[c] * acc_ref[:, cs] + jnp.dot(vt, ps[c], preferred_element_type=F32)

    n_far = jnp.maximum(i - (n_cls - 2), 0) if n_cls == 3 else 0

    def body(mode):
        def run(j, carry):
            step(j, mode)
            return carry
        return run

    if n_cls == 3:
        lax.fori_loop(0, n_far, body("far"), 0)
    lax.fori_loop(n_far, i, body("near"), 0)
    step(i, "diag")

    lp = lam_ref[...]
    lam = (jnp.exp(jnp.sum(lp[0:1] * lp[1:2], axis=-1, keepdims=True))
           - jnp.exp(jnp.sum(lp[2:3] * lp[3:4], axis=-1, keepdims=True)) + lam_init)
    o = acc_ref[0:ATT_VD, :] / acc_ref[ATT_VD:ATT_VD + 1, :]
    outs = []
    for r in range(2):
        d = o[:, (2 * r) * tq:(2 * r + 1) * tq] - lam * o[:, (2 * r + 1) * tq:(2 * r + 2) * tq]
        ms = jnp.mean(d * d, axis=0, keepdims=True)
        y = d * lax.rsqrt(ms + EPS) * gs_ref[...] * (1.0 - lam_init)
        outs.append(y.T)
    o_ref[0] = jnp.concatenate(outs, axis=-1).astype(o_ref.dtype)


def _attn_self(qt, knb, vt, bias, lam_p, g_subln, lam_init, *, tq, ybuf=None, slab=0):
    b, l, _ = knb.shape
    n_cls = bias.shape[0]
    nk = l // tq
    hw = 2 * ATT_VD
    assert ATT_H * ATT_VD == BRANCH_W
    args = [qt, knb, vt, bias, lam_p, g_subln.reshape(ATT_VD, 1)]
    specs = [
        pl.BlockSpec((1, 1, LANES, 4 * tq), lambda bi, kv, i: (bi, kv, 0, i)),
        pl.BlockSpec((1, l, LANES), lambda bi, kv, i: (bi, 0, kv)),
        pl.BlockSpec((1, 1, nk, ATT_VD + VT_ONES, tq), lambda bi, kv, i: (bi, kv, 0, 0, 0)),
        pl.BlockSpec((n_cls, 2, tq, tq), lambda bi, kv, i: (0, kv, 0, 0)),
        pl.BlockSpec((4, ATT_D), lambda bi, kv, i: (0, 0)),
        pl.BlockSpec((ATT_VD, 1), lambda bi, kv, i: (0, 0)),
    ]
    yshape, kern, aliases = _branch_out(
        ybuf, slab, b, l, args, specs,
        functools.partial(_attn_self_kernel, tq=tq, tk=tq, lam_init=lam_init, n_cls=n_cls, group=4))
    return pl.pallas_call(
        kern,
        out_shape=yshape,
        grid=(b, ATT_KV, l // tq),
        in_specs=specs,
        out_specs=pl.BlockSpec((None, 1, tq, hw), lambda bi, kv, i: (slab, bi, i, kv)),
        scratch_shapes=[pltpu.VMEM((1, 4 * tq), F32), pltpu.VMEM((ATT_VD + VT_ONES, 4 * tq), F32)],
        input_output_aliases=aliases,
        compiler_params=_cparams(("parallel", "parallel", "arbitrary")),
        name="diff_attn_self",
    )(*args)


def _attn_cached_kernel(cls_ref, q_ref, ck_ref, cv_ref, kn_ref, vn_ref, bp_ref, bn_ref, lam_ref, gs_ref,
                        o_ref, m_ref, l_ref, acc_ref, *, tq, tk, nt, lam_init):
    del cls_ref
    j = pl.program_id(1)

    @pl.when(j == 0)
    def _():
        m_ref[...] = jnp.full(m_ref.shape, NEG, F32)
        l_ref[...] = jnp.zeros(l_ref.shape, F32)
        acc_ref[...] = jnp.zeros(acc_ref.shape, F32)

    dn_t = (((1,), (1,)), ((), ()))

    def update(k_all, v_of, bias_at):
        kvs = range(ATT_KV)
        ss = [lax.dot_general(_q4(q_ref[0, :, kv * 2 * LANES:(kv + 1) * 2 * LANES]),
                              k_all[:, kv * LANES:(kv + 1) * LANES].astype(BF16), dn_t,
                              preferred_element_type=F32) for kv in kvs]
        pa = [_osm_softmax(ss[kv], bias_at(2 * kv), bias_at(2 * kv + 1), m_ref.at[kv], l_ref.at[kv])
              for kv in kvs]
        for kv in kvs:
            p, alpha = pa[kv]
            acc_ref[kv] = alpha * acc_ref[kv] + jnp.dot(p, v_of(kv).astype(BF16), preferred_element_type=F32)

    @pl.when(j < nt)
    def _():
        update(ck_ref[...], lambda kv: cv_ref[pl.ds(kv, tk, stride=ATT_KV), :], lambda h: bp_ref[0, h])

    @pl.when(j == nt)
    def _():
        update(kn_ref[0], lambda kv: vn_ref[0, :, kv * LANES:(kv + 1) * LANES], lambda h: bn_ref[0, h])
        outs = [_osm_finish(l_ref.at[kv], acc_ref.at[kv], lam_ref, gs_ref, lam_init, tq)
                for kv in range(ATT_KV)]
        o_ref[0] = jnp.concatenate(outs, axis=-1).astype(o_ref.dtype)


def _attn_cached(qn, knb, vb, cache_k, cache_v, layer, bias_past, cls_tbl, bias_new, lam_p, g_subln,
                 lam_init, *, tk, ybuf=None, slab=0):
    b, l, _ = qn.shape
    past = cache_k.shape[2]
    nt = past // tk
    kw = ATT_KV * 2 * ATT_D
    ck = cache_k.reshape(cache_k.shape[0], b, past, kw)
    cv = cache_v.reshape(cache_v.shape[0], b, past * ATT_KV, ATT_VD)
    args = [qn, ck, cv, knb, vb, bias_past, bias_new, lam_p, g_subln.reshape(1, ATT_VD)]
    specs = [
        pl.BlockSpec((1, l, ATT_H * 2 * ATT_D), lambda bi, j, cls: (bi, 0, 0)),
        pl.BlockSpec((None, None, tk, kw), lambda bi, j, cls: (layer, bi, jnp.minimum(j, nt - 1), 0)),
        pl.BlockSpec((None, None, tk * ATT_KV, ATT_VD),
                     lambda bi, j, cls: (layer, bi, jnp.minimum(j, nt - 1), 0)),
        pl.BlockSpec((1, l, kw), lambda bi, j, cls: (bi, 0, 0)),
        pl.BlockSpec((1, l, kw), lambda bi, j, cls: (bi, 0, 0)),
        pl.BlockSpec((1, ATT_H, l, tk), lambda bi, j, cls: (cls[jnp.minimum(j, nt - 1)], 0, 0, 0)),
        pl.BlockSpec((1, ATT_H, l, l), lambda bi, j, cls: (0, 0, 0, 0)),
        pl.BlockSpec((4, ATT_D), lambda bi, j, cls: (0, 0)),
        pl.BlockSpec((1, ATT_VD), lambda bi, j, cls: (0, 0)),
    ]
    yshape, kern, aliases = _branch_out(
        ybuf, slab, b, l, args, specs,
        functools.partial(_attn_cached_kernel, tq=l, tk=tk, nt=nt, lam_init=lam_init), n_prefetch=1)
    grid_spec = pltpu.PrefetchScalarGridSpec(
        num_scalar_prefetch=1,
        grid=(b, nt + 1),
        in_specs=specs,
        out_specs=pl.BlockSpec((None, 1, l, ATT_H * ATT_VD), lambda bi, j, cls: (slab, bi, 0, 0)),
        scratch_shapes=[pltpu.VMEM((ATT_KV, 4 * l, LANES), F32), pltpu.VMEM((ATT_KV, 4 * l, LANES), F32),
                        pltpu.VMEM((ATT_KV, 4 * l, ATT_VD), F32)],
    )
    return pl.pallas_call(
        kern,
        out_shape=yshape,
        grid_spec=grid_spec,
        input_output_aliases=aliases,
        compiler_params=_cparams(("parallel", "arbitrary")),
        name="diff_attn_cached",
    )(cls_tbl, *args)


def _xattn_kernel(q_ref, mk_ref, mv_ref, o_ref):
    dn_t = (((1,), (1,)), ((), ()))
    sls = [slice(h * X_D, (h + 1) * X_D) for h in range(X_H)]
    ss = [lax.dot_general(q_ref[0, :, sl], mk_ref[0, :, sl].astype(BF16), dn_t,
                          preferred_element_type=F32) * (X_D ** -0.5) for sl in sls]
    ps = []
    for s in ss:
        s = s - jnp.max(s, axis=-1, keepdims=True)
        p = jnp.exp(s)
        ps.append((p / jnp.sum(p, axis=-1, keepdims=True)).astype(BF16))
    outs = [jnp.dot(p, mv_ref[0, :, sl].astype(BF16), preferred_element_type=F32) for p, sl in zip(ps, sls)]
    o_ref[0] = jnp.concatenate(outs, axis=-1).astype(o_ref.dtype)


def _xattn(q, mk, mv, *, tq):
    b, l, w = q.shape
    mlen = mk.shape[1]
    tq = min(tq, l)
    assert l % tq == 0
    return pl.pallas_call(
        _xattn_kernel,
        out_shape=jax.ShapeDtypeStruct((b, l, w), BF16),
        grid=(b, l // tq),
        in_specs=[
            pl.BlockSpec((1, tq, w), lambda i, c: (i, c, 0)),
            pl.BlockSpec((1, mlen, w), lambda i, c: (i, 0, 0)),
            pl.BlockSpec((1, mlen, w), lambda i, c: (i, 0, 0)),
        ],
        out_specs=pl.BlockSpec((1, tq, w), lambda i, c: (i, c, 0)),
        compiler_params=_cparams(("parallel", "parallel")),
        name="mem_cross_attn",
    )(q, mk, mv)


def _prep_weights(p, depth):
    layers = []
    off_z, off_xbc = 0, SSM_D
    off_dt = off_xbc + XBC_W
    off_pool = off_dt + SSM_H
    off_q = off_pool + BRANCH_W
    off_k = off_q + ATT_H * 2 * ATT_D
    off_v = off_k + ATT_KV * 2 * ATT_D
    off_gu = off_v + ATT_KV * ATT_VD
    off_gv = off_gu + BRANCH_W
    in_w = off_gv + BRANCH_W
    for l in range(depth):
        w = p['w_in'][l]
        d = w.shape[0]
        w_in = jnp.concatenate([
            w[:, off_xbc:off_dt], w[:, off_z:off_xbc], w[:, off_pool:off_q], w[:, off_q:off_k],
            w[:, off_gu:off_gv], w[:, off_gv:in_w], w[:, off_k:off_v], w[:, off_v:off_gu],
            w[:, off_dt:off_pool], jnp.zeros((d, LANES - SSM_H), w.dtype)], axis=1).astype(BF16)
        pad = lambda v: jnp.pad(v, (0, LANES - SSM_H)).reshape(1, LANES)
        layers.append(dict(
            g_ffn1=p['g_ffn1'][l], ffn1_in=p['w_ffn1_in'][l].astype(BF16), ffn1_out=p['w_ffn1_out'][l].astype(BF16),
            g_mix=p['g_mix'][l], w_in=w_in,
            conv_w=p['conv_w'][l], conv_b=p['conv_b'][l].reshape(1, XBC_W),
            dt_bias=pad(p['dt_bias'][l]), a_log=pad(p['a_log'][l]),
            d_skip=jnp.repeat(p['d_skip'][l], SSM_P).reshape(1, SSM_D), g_ssd=p['g_ssd'][l].reshape(1, SSM_D),
            w_pool=p['w_pool'][l].astype(BF16), pool_scale=p['pool_scale'][l].reshape(1, BRANCH_W),
            g_q=p['g_q'][l], g_k=p['g_k'][l], lam=p['lam'][l], g_subln=p['g_subln'][l],
            g_gv=p['g_gv'][l].reshape(1, BRANCH_W), w_sp=p['w_sp'][l], b_sp=p['b_sp'][l],
            w_gate=p['w_gate'][l].astype(BF16), b_gate=p['b_gate'][l], w_branch=p['w_branch'][l].astype(BF16),
            w_out=p['w_out'][l].astype(BF16),
            g_x=p['g_x'][l], w_xq=p['w_xq'][l].astype(BF16), g_xq=p['g_xq'][l], w_xo=p['w_xo'][l].astype(BF16),
            g_ffn2=p['g_ffn2'][l], ffn2_in=p['w_ffn2_in'][l].astype(BF16), ffn2_out=p['w_ffn2_out'][l].astype(BF16),
            g_post=p['g_post'][l],
            g_mem=p['g_mem'][l], w_mk=p['w_mk'][l].astype(BF16), w_mv=p['w_mv'][l].astype(BF16), g_xk=p['g_xk'][l],
        ))
    return layers


def _run_trunk(x, cache_k, cache_v, mem_k, mem_v, ssm0, conv0, pool0, layers, rel_bias):
    b, l, d = x.shape
    m = b * l
    depth = len(layers)
    cached = cache_k is not None
    assert l >= POOL_BUF and l >= CONV_W - 1
    if cached:
        past = cache_k.shape[2]
        tk = min(1024, past)
        assert past % tk == 0 and past % CHUNK == 0 and l <= CHUNK and (past % GM_CHUNK == 0)
        nt = past // tk
        offs, cls = [FAR_OFFSET], []
        for j in range(nt):
            off = j * tk - past
            if _is_far(off, l, tk):
                cls.append(0)
            else:
                offs.append(off)
                cls.append(len(offs) - 1)
        bias_past = _bias_tiles(rel_bias, offs, l, tk)
        cls_tbl = jnp.asarray(np.asarray(cls, np.int32))
        bias_new = _bias_tiles(rel_bias, [0], l, l)
    else:
        past = 0
        tq = min(512, l)
        assert l % tq == 0 and tq % CHUNK == 0
        offs = [0]
        if l > tq:
            offs.append(-tq)
        if l > 2 * tq:
            assert _is_far(-2 * tq, tq, tq)
            offs.append(FAR_OFFSET)
        bias_self = _bias_tiles(rel_bias, offs, tq, tq, keys_major=True)

    h = x.reshape(m, d)
    new_k, new_v, new_ssm, new_conv, new_pool, new_gv = [], [], [], [], [], []
    for li, w in enumerate(layers):
        lam_init = 0.8 - 0.6 * math.exp(-0.3 * li)
        h = _ffn(h, w['g_ffn1'], w['ffn1_in'], w['ffn1_out'], tm=1024)
        proj = _mm(h, w['w_in'], g=w['g_mix'], out_dtype=F32, tm=1024, tn=1664, name="in_proj")
        proj = proj.reshape(b, l, P_W)

        ys, s_ssm = _ssd(proj, conv0[li] if cached else None, ssm0[li] if cached else None,
                         w['conv_w'], w['conv_b'], w['dt_bias'], w['a_log'], w['d_skip'], w['g_ssd'], q=128)
        ys = _pool(proj, pool0[li] if cached else None, past, w['w_pool'], w['pool_scale'], t=256,
                   ybuf=ys, slab=1)
        if cached:
            qn, kn, knb, vb = _qknorm(proj, w['g_q'], w['g_k'], t=512)
            ys = _attn_cached(qn, knb, vb, cache_k, cache_v, li, bias_past, cls_tbl, bias_new,
                              w['lam'], w['g_subln'], lam_init, tk=tk, ybuf=ys, slab=2)
        else:
            qt, kn, knb, vt = _qknorm_t(proj, w['g_q'], w['g_k'], t=tq)
            ys = _attn_self(qt, knb, vt, bias_self, w['lam'], w['g_subln'], lam_init, tq=tq, ybuf=ys, slab=2)
        ys, v_gm = _gmlp(proj, w['g_gv'], w['w_sp'], w['b_sp'], keep_vn=cached, ybuf=ys, slab=3)

        merged = _merge(h, w['g_mix'], ys.reshape(N_BRANCH, m, BRANCH_W), w['w_gate'], w['b_gate'], w['w_branch'])
        h = _mm(merged, w['w_out'], res=h, out_dtype=F32, tm=512, tn=2048, name="out_proj", w_resident=True)

        qx = _mm(h, w['w_xq'], g=w['g_x'], gh=w['g_xq'], out_dtype=BF16, tm=512, tn=512, name="xattn_q")
        o = _xattn(qx.reshape(b, l, X_H * X_D), mem_k[li].reshape(b, -1, X_H * X_D),
                   mem_v[li].reshape(b, -1, X_H * X_D), tq=512)
        h = _ffn(h, w['g_ffn2'], w['ffn2_in'], w['ffn2_out'], post_g=w['g_post'],
                 pre=(o.reshape(m, X_H * X_D), w['w_xo']))

        new_k.append(kn.reshape(b, l, ATT_KV, 2, ATT_D))
        new_v.append(proj[:, :, P_V:P_V + ATT_KV * ATT_VD].reshape(b, l, ATT_KV, ATT_VD))
        new_ssm.append(s_ssm)
        new_conv.append(proj[:, l - (CONV_W - 1):, P_XBC:P_XBC + XBC_W])
        new_pool.append(proj[:, l - POOL_BUF:, P_POOL:P_POOL + BRANCH_W])
        new_gv.append(v_gm)
    return (h.reshape(b, l, d), jnp.stack(new_k), jnp.stack(new_v), jnp.stack(new_ssm),
            jnp.stack(new_conv), jnp.stack(new_pool), jnp.stack(new_gv) if cached else None)


def kernel(x_prompt, x_sample, cache_attn_k, cache_attn_v, cache_mem_k, cache_mem_v, state_ssm, state_conv, state_pool, mem_prompt, g_ffn1, w_ffn1_in, w_ffn1_out, g_mix, w_in, conv_w, conv_b, dt_bias, a_log, d_skip, g_ssd, w_pool, pool_scale, g_q, g_k, lam, g_subln, rel_bias, g_gv, w_sp, b_sp, w_gate, b_gate, w_branch, w_out, g_x, w_xq, g_xq, g_mem, w_mk, w_mv, g_xk, w_xo, g_ffn2, w_ffn2_in, w_ffn2_out, g_post):
    p = dict(g_ffn1=g_ffn1, w_ffn1_in=w_ffn1_in, w_ffn1_out=w_ffn1_out, g_mix=g_mix, w_in=w_in,
             conv_w=conv_w, conv_b=conv_b, dt_bias=dt_bias, a_log=a_log, d_skip=d_skip, g_ssd=g_ssd,
             w_pool=w_pool, pool_scale=pool_scale, g_q=g_q, g_k=g_k, lam=lam, g_subln=g_subln,
             g_gv=g_gv, w_sp=w_sp, b_sp=b_sp, w_gate=w_gate, b_gate=b_gate, w_branch=w_branch,
             w_out=w_out, g_x=g_x, w_xq=w_xq, g_xq=g_xq, g_mem=g_mem, w_mk=w_mk, w_mv=w_mv, g_xk=g_xk,
             w_xo=w_xo, g_ffn2=g_ffn2, w_ffn2_in=w_ffn2_in, w_ffn2_out=w_ffn2_out, g_post=g_post)
    depth = w_in.shape[0]
    layers = _prep_weights(p, depth)

    bp, mlen, d = mem_prompt.shape
    mem2 = mem_prompt.reshape(bp * mlen, d)
    mks, mvs = [], []
    for w in layers:
        mk = _mm(mem2, w['w_mk'], g=w['g_mem'], gh=w['g_xk'], out_dtype=F32, tm=512, tn=512, name="mem_k")
        mv = _mm(mem2, w['w_mv'], g=w['g_mem'], out_dtype=F32, tm=512, tn=512, name="mem_v")
        mks.append(mk.reshape(bp, mlen, X_H, X_D))
        mvs.append(mv.reshape(bp, mlen, X_H, X_D))
    p_mem_k = jnp.stack(mks)
    p_mem_v = jnp.stack(mvs)

    y_prompt, p_attn_k, p_attn_v, p_ssm, p_conv, p_pool, _ = _run_trunk(
        x_prompt, None, None, p_mem_k, p_mem_v, None, None, None, layers, rel_bias)
    y_sample, s_attn_k, s_attn_v, s_ssm, s_conv, s_pool, s_gmlp_v = _run_trunk(
        x_sample, cache_attn_k, cache_attn_v, cache_mem_k, cache_mem_v, state_ssm, state_conv,
        state_pool, layers, rel_bias)

    return (y_prompt, y_sample, p_attn_k, p_attn_v, p_mem_k, p_mem_v, p_ssm, p_conv, p_pool,
            s_attn_k, s_attn_v, s_ssm, s_conv, s_pool, s_gmlp_v)
```

```python
import functools
import math

import jax
import jax.numpy as jnp
import numpy as np
from jax import lax
from jax.experimental import pallas as pl
from jax.experimental.pallas import tpu as pltpu

F32 = jnp.float32
BF16 = jnp.bfloat16
EPS = 1e-6
NEG = -1e30

VMEM_LIMIT_BYTES = 56 * 1024 * 1024
LANES = 128

BRANCH_W = 1024
SSM_P = 64
SSM_H = 16
SSM_G = 4
SSM_N = 128
SSM_D = SSM_H * SSM_P
CONV_W = 4
XBC_W = SSM_D + 2 * SSM_G * SSM_N
POOL_WINDOWS = (2, 4, 8, 16)
POOL_GW = BRANCH_W // len(POOL_WINDOWS)
POOL_BUF = max(POOL_WINDOWS) - 1
POOL_PAD = max(POOL_WINDOWS) // 2
ATT_H = 8
ATT_KV = 4
ATT_D = 64
ATT_VD = 128
CHUNK = 64
N_BUCKETS = 32
MAX_DIST = 128
GM_CHUNK = 128
GM_G = 4
GM_GW = BRANCH_W // GM_G
X_H = 4
X_D = 128
VT_ONES = 16

P_XBC = 0
P_Z = 2048
P_POOL = 3072
P_Q = 4096
P_GU = 5120
P_GV = 6144
P_K = 7168
P_V = 7680
P_DT = 8192
P_W = 8320


def _cparams(sem):
    return pltpu.CompilerParams(dimension_semantics=sem, vmem_limit_bytes=VMEM_LIMIT_BYTES)


N_BRANCH = 4


def _skip_ref(fn, pos):
    def wrapped(*refs):
        return fn(*refs[:pos], *refs[pos + 1:])
    return wrapped


def _branch_out(ybuf, slab, b, l, args, specs, kernel_fn, n_prefetch=0):
    shape = jax.ShapeDtypeStruct((N_BRANCH, b, l, BRANCH_W), BF16)
    if ybuf is None:
        return shape, kernel_fn, {}
    assert slab > 0 and ybuf.shape == shape.shape and ybuf.dtype == shape.dtype
    pos = n_prefetch + len(args)
    args.append(ybuf)
    specs.append(pl.BlockSpec(memory_space=pl.ANY))
    return shape, _skip_ref(kernel_fn, pos), {pos: 0}


def _rms(xf, g):
    ms = jnp.mean(xf * xf, axis=-1, keepdims=True)
    return xf * lax.rsqrt(ms + EPS) * g


def _silu(x):
    return x * jax.nn.sigmoid(x)


def _mm_kernel(*refs, norm, head_norm, residual, emit_xn):
    it = iter(refs)
    x_ref = next(it)
    g_ref = next(it) if norm else None
    w_ref = next(it)
    gh_ref = next(it) if head_norm else None
    res_ref = next(it) if residual else None
    o_ref = next(it)
    xn_ref = next(it) if norm else None

    if norm:
        @pl.when(pl.program_id(1) == 0)
        def _():
            xn_ref[...] = _rms(x_ref[...].astype(F32), g_ref[...]).astype(BF16)
        xb = xn_ref[...]
    else:
        xb = x_ref[...].astype(BF16)
    acc = jnp.dot(xb, w_ref[...], preferred_element_type=F32)
    if head_norm:
        parts = []
        for c in range(acc.shape[1] // LANES):
            parts.append(_rms(acc[:, c * LANES:(c + 1) * LANES], gh_ref[...]))
        acc = jnp.concatenate(parts, axis=-1)
    if residual:
        acc = res_ref[...] + acc
    o_ref[...] = acc.astype(o_ref.dtype)


def _mm(x, w, *, g=None, gh=None, res=None, out_dtype, tm, tn, name, w_resident=False, emit_xn=False):
    m, k = x.shape
    n = w.shape[1]
    tm = min(tm, m)
    tn = min(tn, n)
    assert m % tm == 0 and n % tn == 0
    norm, head_norm, residual = g is not None, gh is not None, res is not None
    assert not (norm and w_resident)
    if w_resident:
        grid = (n // tn, m // tm)
        ij = lambda a, b: (b, a)
    else:
        grid = (m // tm, n // tn)
        ij = lambda a, b: (a, b)
    args, specs = [x], [pl.BlockSpec((tm, k), lambda a, b: (ij(a, b)[0], 0))]
    if norm:
        args.append(g.reshape(1, k))
        specs.append(pl.BlockSpec((1, k), lambda a, b: (0, 0)))
    args.append(w)
    specs.append(pl.BlockSpec((k, tn), lambda a, b: (0, ij(a, b)[1])))
    if head_norm:
        args.append(gh.reshape(1, LANES))
        specs.append(pl.BlockSpec((1, LANES), lambda a, b: (0, 0)))
    if residual:
        args.append(res)
        specs.append(pl.BlockSpec((tm, tn), lambda a, b: ij(a, b)))
    assert not emit_xn or norm
    scratch = [pltpu.VMEM((tm, k), BF16)] if norm and not emit_xn else []
    out_shape = [jax.ShapeDtypeStruct((m, n), out_dtype)]
    out_specs = [pl.BlockSpec((tm, tn), lambda a, b: ij(a, b))]
    if emit_xn:
        out_shape.append(jax.ShapeDtypeStruct((m, k), BF16))
        out_specs.append(pl.BlockSpec((tm, k), lambda a, b: (ij(a, b)[0], 0)))
    outs = pl.pallas_call(
        functools.partial(_mm_kernel, norm=norm, head_norm=head_norm, residual=residual, emit_xn=emit_xn),
        out_shape=tuple(out_shape),
        grid=grid,
        in_specs=specs,
        out_specs=tuple(out_specs),
        scratch_shapes=scratch,
        compiler_params=_cparams(("parallel", "arbitrary")),
        name=name,
    )(*args)
    return outs if emit_xn else outs[0]


def _ffn_kernel(*refs, post, pre):
    it = iter(refs)
    x_ref, g_ref, wg_ref, wu_ref, wo_ref = (next(it) for _ in range(5))
    gp_ref = next(it) if post else None
    a_ref, wa_ref = (next(it), next(it)) if pre else (None, None)
    o_ref, xn_ref = next(it), next(it)
    acc_ref = next(it) if pre else o_ref
    f = pl.program_id(1)

    @pl.when(f == 0)
    def _():
        x = x_ref[...]
        if pre:
            x = x + jnp.dot(a_ref[...], wa_ref[...], preferred_element_type=F32)
            o_ref[...] = x
        xn_ref[...] = _rms(x, g_ref[...]).astype(BF16)
        acc_ref[...] = jnp.zeros_like(acc_ref)

    xb = xn_ref[...]
    gate = jnp.dot(xb, wg_ref[...], preferred_element_type=F32)
    up = jnp.dot(xb, wu_ref[...], preferred_element_type=F32)
    mid = (_silu(gate) * up).astype(BF16)
    acc_ref[...] += jnp.dot(mid, wo_ref[...], preferred_element_type=F32)

    @pl.when(f == pl.num_programs(1) - 1)
    def _():
        h = (o_ref[...] if pre else x_ref[...]) + 0.5 * acc_ref[...]
        if post:
            h = _rms(h, gp_ref[...])
        o_ref[...] = h


def _ffn(x, g, w_in, w_out, *, post_g=None, pre=None, tm=512, tf=512):
    m, d = x.shape
    ff = w_out.shape[0]
    tm = min(tm, m)
    assert m % tm == 0 and ff % tf == 0
    nf = ff // tf
    post = post_g is not None
    args = [x, g.reshape(1, d), w_in, w_in, w_out]
    specs = [
        pl.BlockSpec((tm, d), lambda i, f: (i, 0)),
        pl.BlockSpec((1, d), lambda i, f: (0, 0)),
        pl.BlockSpec((d, tf), lambda i, f: (0, f)),
        pl.BlockSpec((d, tf), lambda i, f: (0, nf + f)),
        pl.BlockSpec((tf, d), lambda i, f: (f, 0)),
    ]
    if post:
        args.append(post_g.reshape(1, d))
        specs.append(pl.BlockSpec((1, d), lambda i, f: (0, 0)))
    if pre is not None:
        a, wa = pre
        ka = a.shape[1]
        args += [a, wa]
        specs += [pl.BlockSpec((tm, ka), lambda i, f: (i, 0)), pl.BlockSpec((ka, d), lambda i, f: (0, 0))]
    return pl.pallas_call(
        functools.partial(_ffn_kernel, post=post, pre=pre is not None),
        out_shape=jax.ShapeDtypeStruct((m, d), F32),
        grid=(m // tm, nf),
        in_specs=specs,
        out_specs=pl.BlockSpec((tm, d), lambda i, f: (i, 0)),
        scratch_shapes=[pltpu.VMEM((tm, d), BF16)] + ([pltpu.VMEM((tm, d), F32)] if pre is not None else []),
        compiler_params=_cparams(("parallel", "arbitrary")),
        name="swiglu_ffn",
    )(*args)


def _merge_kernel(hn_ref, y_ref, wg_ref, bg_ref, wb_ref, o_ref, acc_ref):
    br = pl.program_id(2)

    @pl.when(br == 0)
    def _():
        acc_ref[...] = jnp.zeros_like(acc_ref)

    gate = jax.nn.sigmoid(jnp.dot(hn_ref[...], wg_ref[...], preferred_element_type=F32) + bg_ref[...])
    acc_ref[...] += gate * jnp.dot(y_ref[...], wb_ref[...], preferred_element_type=F32)

    @pl.when(br == pl.num_programs(2) - 1)
    def _():
        o_ref[...] = acc_ref[...].astype(o_ref.dtype)


def _merge(hn, ys, w_gate, b_gate, w_branch, *, tm=1024, tn=1024):
    m, d = hn.shape
    nb, bw, _ = w_branch.shape
    tm = min(tm, m)
    assert m % tm == 0 and d % tn == 0 and ys.shape == (nb, m, bw)
    return pl.pallas_call(
        _merge_kernel,
        out_shape=jax.ShapeDtypeStruct((m, d), BF16),
        grid=(m // tm, d // tn, nb),
        in_specs=[
            pl.BlockSpec((tm, d), lambda i, j, b: (i, 0)),
            pl.BlockSpec((None, tm, bw), lambda i, j, b: (b, i, 0)),
            pl.BlockSpec((None, d, tn), lambda i, j, b: (b, 0, j)),
            pl.BlockSpec((None, 1, tn), lambda i, j, b: (b, 0, j)),
            pl.BlockSpec((None, bw, tn), lambda i, j, b: (b, 0, j)),
        ],
        out_specs=pl.BlockSpec((tm, tn), lambda i, j, b: (i, j)),
        scratch_shapes=[pltpu.VMEM((tm, tn), F32)],
        compiler_params=_cparams(("parallel", "arbitrary", "arbitrary")),
        name="branch_merge",
    )(hn, ys, w_gate, b_gate.reshape(nb, 1, d), w_branch)


def _ssd_kernel(*refs, q, has_state):
    it = iter(refs)
    xbc_ref, z_ref, dt_ref = next(it), next(it), next(it)
    conv0_ref = next(it) if has_state else None
    s0_ref = next(it) if has_state else None
    cw_ref, cb_ref, dtb_ref, alog_ref, dskip_ref, gssd_ref = (next(it) for _ in range(6))
    y_ref, s_ref, cbuf = next(it), next(it), next(it)
    hist = 8

    @pl.when(pl.program_id(1) == 0)
    def _():
        cbuf[0:hist, :] = jnp.zeros((hist, XBC_W), F32)
        if has_state:
            cbuf[hist - (CONV_W - 1):hist, :] = conv0_ref[0]
            s_ref[0] = s0_ref[0]
        else:
            s_ref[0] = jnp.zeros(s_ref.shape[1:], F32)

    cbuf[hist:hist + q, :] = xbc_ref[0]
    conv = cb_ref[...]
    for k in range(CONV_W):
        lo = hist - (CONV_W - 1) + k
        conv = conv + cbuf[lo:lo + q, :] * cw_ref[k:k + 1, :]
    cbuf[0:hist, :] = cbuf[q:q + hist, :]
    xc = _silu(conv)
    xs = xc[:, :SSM_D]
    bm = xc[:, SSM_D:SSM_D + SSM_G * SSM_N]
    cm = xc[:, SSM_D + SSM_G * SSM_N:]

    dtl = dt_ref[0] + dtb_ref[...]
    dt = jnp.maximum(dtl, 0.0) + jnp.log1p(jnp.exp(-jnp.abs(dtl)))
    a = dt * (-jnp.exp(alog_ref[...]))
    ri = lax.broadcasted_iota(jnp.int32, (q, q), 0)
    ci = lax.broadcasted_iota(jnp.int32, (q, q), 1)
    tril = ri >= ci
    cum = jnp.dot(tril.astype(F32), a, preferred_element_type=F32,
                  precision=lax.Precision.HIGHEST)
    cum_t = cum.T
    dt_t = dt.T
    cum_last = cum[q - 1:q, :]
    w_state = jnp.exp(cum_last - cum) * dt
    ecum = jnp.exp(cum)
    ecl = jnp.exp(cum_last)

    lane_lo = lax.broadcasted_iota(jnp.int32, (q, LANES), 1) < SSM_P
    row_lo_t = lax.broadcasted_iota(jnp.int32, (LANES, q), 0) < SSM_P
    row_lo_s = lax.broadcasted_iota(jnp.int32, (LANES, SSM_N), 0) < SSM_P
    dn_t = (((1,), (1,)), ((), ()))

    y_parts = []
    cb = None
    heads_per_group = SSM_H // SSM_G
    for pp in range(SSM_H // 2):
        grp = (2 * pp) // heads_per_group
        xp = xs[:, pp * LANES:(pp + 1) * LANES]
        bg = bm[:, grp * SSM_N:(grp + 1) * SSM_N]
        cg = cm[:, grp * SSM_N:(grp + 1) * SSM_N]
        cg_b = cg.astype(BF16)
        if (2 * pp) % heads_per_group == 0:
            cb = lax.dot_general(cg_b, bg.astype(BF16), dn_t, preferred_element_type=F32)
        xp_t = xp.T
        s_pair = s_ref[0, pp * LANES:(pp + 1) * LANES, :]
        y_pair = None
        s_new = None
        for hh in range(2):
            hd = 2 * pp + hh
            seg = cum[:, hd:hd + 1] - cum_t[hd:hd + 1, :]
            mat = jnp.exp(jnp.where(tril, seg, NEG)) * cb * dt_t[hd:hd + 1, :]
            sel = lane_lo if hh == 0 else jnp.logical_not(lane_lo)
            x_h = jnp.where(sel, xp, 0.0).astype(BF16)
            yd = jnp.dot(mat.astype(BF16), x_h, preferred_element_type=F32)
            y_pair = yd if y_pair is None else y_pair + yd
            sel_t = row_lo_t if hh == 0 else jnp.logical_not(row_lo_t)
            xt_h = jnp.where(sel_t, xp_t, 0.0).astype(BF16)
            bs = (bg * w_state[:, hd:hd + 1]).astype(BF16)
            sn = jnp.dot(xt_h, bs, preferred_element_type=F32)
            s_new = sn if s_new is None else s_new + sn
        y_off = lax.dot_general(cg_b, s_pair.astype(BF16), dn_t, preferred_element_type=F32)
        e_pair = jnp.where(lane_lo, ecum[:, 2 * pp:2 * pp + 1], ecum[:, 2 * pp + 1:2 * pp + 2])
        y_parts.append(y_pair + y_off * e_pair)
        dec = jnp.where(row_lo_s, ecl[:, 2 * pp:2 * pp + 1], ecl[:, 2 * pp + 1:2 * pp + 2])
        s_ref[0, pp * LANES:(pp + 1) * LANES, :] = dec * s_pair + s_new

    y = jnp.concatenate(y_parts, axis=-1)
    y = y + dskip_ref[...] * xs
    y = y * _silu(z_ref[0])
    gw = SSM_D // SSM_G
    outs = [_rms(y[:, gi * gw:(gi + 1) * gw], gssd_ref[:, gi * gw:(gi + 1) * gw]) for gi in range(SSM_G)]
    y_ref[0] = jnp.concatenate(outs, axis=-1).astype(y_ref.dtype)


def _ssd(proj, conv0, s0, cw, cb, dtb, alog, dskip, gssd, *, q, ybuf=None, slab=0):
    b, l, _ = proj.shape
    q = min(q, l)
    assert l % q == 0 and q >= 8
    has_state = s0 is not None
    args = [proj, proj, proj]
    specs = [
        pl.BlockSpec((1, q, XBC_W), lambda i, c: (i, c, P_XBC // XBC_W)),
        pl.BlockSpec((1, q, SSM_D), lambda i, c: (i, c, P_Z // SSM_D)),
        pl.BlockSpec((1, q, LANES), lambda i, c: (i, c, P_DT // LANES)),
    ]
    if has_state:
        args += [conv0, s0.reshape(b, SSM_H * SSM_P, SSM_N)]
        specs += [
            pl.BlockSpec((1, CONV_W - 1, XBC_W), lambda i, c: (i, 0, 0)),
            pl.BlockSpec((1, SSM_H * SSM_P, SSM_N), lambda i, c: (i, 0, 0)),
        ]
    args += [cw, cb, dtb, alog, dskip, gssd]
    specs += [
        pl.BlockSpec((CONV_W, XBC_W), lambda i, c: (0, 0)),
        pl.BlockSpec((1, XBC_W), lambda i, c: (0, 0)),
        pl.BlockSpec((1, LANES), lambda i, c: (0, 0)),
        pl.BlockSpec((1, LANES), lambda i, c: (0, 0)),
        pl.BlockSpec((1, SSM_D), lambda i, c: (0, 0)),
        pl.BlockSpec((1, SSM_D), lambda i, c: (0, 0)),
    ]
    yshape, kern, aliases = _branch_out(ybuf, slab, b, l, args, specs,
                                        functools.partial(_ssd_kernel, q=q, has_state=has_state))
    y, s = pl.pallas_call(
        kern,
        out_shape=(yshape, jax.ShapeDtypeStruct((b, SSM_H * SSM_P, SSM_N), F32)),
        grid=(b, l // q),
        in_specs=specs,
        out_specs=(pl.BlockSpec((None, 1, q, SSM_D), lambda i, c: (slab, i, c, 0)),
                   pl.BlockSpec((1, SSM_H * SSM_P, SSM_N), lambda i, c: (i, 0, 0))),
        scratch_shapes=[pltpu.VMEM((q + 8, XBC_W), F32)],
        input_output_aliases=aliases,
        compiler_params=_cparams(("parallel", "arbitrary")),
        name="ssd_mixer",
    )(*args)
    return y, s.reshape(b, SSM_H, SSM_P, SSM_N)


def _pool_kernel(*refs, t, pos0, has_state):
    it = iter(refs)
    xp_ref = next(it)
    p0_ref = next(it) if has_state else None
    w_ref, sc_ref, y_ref, buf, lvl = next(it), next(it), next(it), next(it), next(it)
    hist = POOL_BUF + 1
    pad = POOL_PAD
    base = pad + hist
    n = hist + t
    ti = pl.program_id(1)

    @pl.when(ti == 0)
    def _():
        buf[0:base, :] = jnp.zeros((base, BRANCH_W), F32)
        lvl[0:pad, :] = jnp.zeros((pad, BRANCH_W), F32)
        if has_state:
            buf[pad + 1:base, :] = p0_ref[0]

    xp = xp_ref[0]
    buf[base:base + t, :] = xp
    pos = pos0 + ti * t + lax.broadcasted_iota(jnp.int32, (t, POOL_GW), 0)
    outs = []
    for gi, w in enumerate(POOL_WINDOWS):
        cs = slice(gi * POOL_GW, (gi + 1) * POOL_GW)
        src, k = buf, 1
        while 2 * k < w:
            lvl[pad:pad + n, cs] = src[pad:pad + n, cs] + src[pad - k:pad - k + n, cs]
            src, k = lvl, 2 * k
        s = src[base:base + t, cs] + src[base - k:base - k + t, cs]
        cnt = jnp.minimum(w, pos + 1).astype(F32)
        pooled = s / cnt - xp[:, cs]
        y = jnp.dot(pooled.astype(BF16), w_ref[gi], preferred_element_type=F32)
        outs.append(y * sc_ref[:, cs])
    y_ref[0] = jnp.concatenate(outs, axis=-1).astype(y_ref.dtype)
    buf[pad:base, :] = buf[pad + t:base + t, :]


def _pool(proj, pool0, pos0, w_pool, scale, *, t, ybuf=None, slab=0):
    b, l, _ = proj.shape
    t = min(t, l)
    assert l % t == 0 and t >= POOL_BUF + 1
    has_state = pool0 is not None
    args = [proj]
    specs = [pl.BlockSpec((1, t, BRANCH_W), lambda i, c: (i, c, P_POOL // BRANCH_W))]
    if has_state:
        args.append(pool0)
        specs.append(pl.BlockSpec((1, POOL_BUF, BRANCH_W), lambda i, c: (i, 0, 0)))
    args += [w_pool, scale]
    specs += [
        pl.BlockSpec((len(POOL_WINDOWS), POOL_GW, POOL_GW), lambda i, c: (0, 0, 0)),
        pl.BlockSpec((1, BRANCH_W), lambda i, c: (0, 0)),
    ]
    yshape, kern, aliases = _branch_out(
        ybuf, slab, b, l, args, specs, functools.partial(_pool_kernel, t=t, pos0=pos0, has_state=has_state))
    return pl.pallas_call(
        kern,
        out_shape=yshape,
        grid=(b, l // t),
        in_specs=specs,
        out_specs=pl.BlockSpec((None, 1, t, BRANCH_W), lambda i, c: (slab, i, c, 0)),
        scratch_shapes=[pltpu.VMEM((t + POOL_BUF + 1 + POOL_PAD, BRANCH_W), F32),
                        pltpu.VMEM((t + POOL_BUF + 1 + POOL_PAD, BRANCH_W), F32)],
        input_output_aliases=aliases,
        compiler_params=_cparams(("parallel", "arbitrary")),
        name="pool_mixer",
    )(*args)


def _gmlp_kernel(u_ref, v_ref, g_ref, w_ref, b_ref, y_ref, *rest, cl, nsub, keep_vn):
    tril = lax.broadcasted_iota(jnp.int32, (cl, cl), 0) >= lax.broadcasted_iota(jnp.int32, (cl, cl), 1)
    ws = [jnp.where(tril, w_ref[gi], 0.0).astype(BF16) for gi in range(GM_G)]
    for sub in range(nsub):
        rows = slice(sub * cl, (sub + 1) * cl)
        u = jax.nn.gelu(u_ref[0, rows, :], approximate=True)
        vn = _rms(jax.nn.gelu(v_ref[0, rows, :], approximate=True), g_ref[...])
        if keep_vn:
            rest[0][0, rows, :] = vn
        outs = []
        for gi in range(GM_G):
            s = jnp.dot(ws[gi], vn[:, gi * GM_GW:(gi + 1) * GM_GW].astype(BF16), preferred_element_type=F32)
            s = s + b_ref[:, gi:gi + 1]
            outs.append(u[:, gi * GM_GW:(gi + 1) * GM_GW] * s)
        y_ref[0, rows, :] = jnp.concatenate(outs, axis=-1).astype(y_ref.dtype)


def _gmlp(proj, g_gv, w_sp, b_sp, *, keep_vn, ybuf=None, slab=0):
    b, l, _ = proj.shape
    cl = min(GM_CHUNK, l)
    assert l % cl == 0
    nsub = math.gcd(l // cl, 4)
    t = cl * nsub
    w = w_sp[:, :cl, :cl]
    bt = b_sp[:, :cl].T
    args = [proj, proj, g_gv, w, bt]
    specs = [
        pl.BlockSpec((1, t, BRANCH_W), lambda i, c: (i, c, P_GU // BRANCH_W)),
        pl.BlockSpec((1, t, BRANCH_W), lambda i, c: (i, c, P_GV // BRANCH_W)),
        pl.BlockSpec((1, BRANCH_W), lambda i, c: (0, 0)),
        pl.BlockSpec((GM_G, cl, cl), lambda i, c: (0, 0, 0)),
        pl.BlockSpec((cl, GM_G), lambda i, c: (0, 0)),
    ]
    yshape, kern, aliases = _branch_out(
        ybuf, slab, b, l, args, specs, functools.partial(_gmlp_kernel, cl=cl, nsub=nsub, keep_vn=keep_vn))
    out_shape = [yshape]
    out_specs = [pl.BlockSpec((None, 1, t, BRANCH_W), lambda i, c: (slab, i, c, 0))]
    if keep_vn:
        out_shape.append(jax.ShapeDtypeStruct((b, l, BRANCH_W), F32))
        out_specs.append(pl.BlockSpec((1, t, BRANCH_W), lambda i, c: (i, c, 0)))
    outs = pl.pallas_call(
        kern,
        out_shape=tuple(out_shape),
        grid=(b, l // t),
        in_specs=specs,
        out_specs=tuple(out_specs),
        input_output_aliases=aliases,
        compiler_params=_cparams(("parallel", "parallel")),
        name="gmlp_mixer",
    )(*args)
    return outs[0], (outs[1] if keep_vn else None)


def _norm_halves(x, g):
    lo = lax.broadcasted_iota(jnp.int32, x.shape, 1) < ATT_D
    xx = x * x
    s_lo = jnp.sum(jnp.where(lo, xx, 0.0), axis=-1, keepdims=True)
    s_hi = jnp.sum(jnp.where(lo, 0.0, xx), axis=-1, keepdims=True)
    ms = jnp.where(lo, s_lo, s_hi) * (1.0 / ATT_D)
    return x * lax.rsqrt(ms + EPS) * g


def _qknorm_kernel(q_ref, k_ref, v_ref, gq_ref, gk_ref, qn_ref, kn_ref, knb_ref, vb_ref):
    scale = ATT_D ** -0.5
    q = q_ref[0]
    qs = [_norm_halves(q[:, c * LANES:(c + 1) * LANES], gq_ref[...]) * scale
          for c in range(q.shape[1] // LANES)]
    qn_ref[0] = jnp.concatenate(qs, axis=-1).astype(qn_ref.dtype)
    k = k_ref[0]
    ks = [_norm_halves(k[:, c * LANES:(c + 1) * LANES], gk_ref[...]) for c in range(k.shape[1] // LANES)]
    kn = jnp.concatenate(ks, axis=-1)
    kn_ref[0] = kn
    knb_ref[0] = kn.astype(BF16)
    vb_ref[0] = v_ref[0].astype(BF16)


def _qknorm(proj, g_q, g_k, *, t):
    b, l, _ = proj.shape
    t = min(t, l)
    assert l % t == 0
    qw, kw = ATT_H * 2 * ATT_D, ATT_KV * 2 * ATT_D
    gq = jnp.tile(g_q, 2).reshape(1, LANES)
    gk = jnp.tile(g_k, 2).reshape(1, LANES)
    return pl.pallas_call(
        _qknorm_kernel,
        out_shape=(jax.ShapeDtypeStruct((b, l, qw), BF16), jax.ShapeDtypeStruct((b, l, kw), F32),
                   jax.ShapeDtypeStruct((b, l, kw), BF16), jax.ShapeDtypeStruct((b, l, kw), BF16)),
        grid=(b, l // t),
        in_specs=[
            pl.BlockSpec((1, t, qw), lambda i, c: (i, c, P_Q // qw)),
            pl.BlockSpec((1, t, kw), lambda i, c: (i, c, P_K // kw)),
            pl.BlockSpec((1, t, kw), lambda i, c: (i, c, P_V // kw)),
            pl.BlockSpec((1, LANES), lambda i, c: (0, 0)),
            pl.BlockSpec((1, LANES), lambda i, c: (0, 0)),
        ],
        out_specs=(pl.BlockSpec((1, t, qw), lambda i, c: (i, c, 0)),
                   pl.BlockSpec((1, t, kw), lambda i, c: (i, c, 0)),
                   pl.BlockSpec((1, t, kw), lambda i, c: (i, c, 0)),
                   pl.BlockSpec((1, t, kw), lambda i, c: (i, c, 0))),
        compiler_params=_cparams(("parallel", "parallel")),
        name="qk_norm",
    )(proj, proj, proj, gq, gk)


def _bias_kernel(off_ref, tbl_ref, o_ref, *, tq, tk, keys_major):
    c = pl.program_id(0)
    h = pl.program_id(1)
    off = off_ref[c]
    shape = (tk, tq) if keys_major else (tq, tk)
    row = lax.broadcasted_iota(jnp.int32, shape, 1 if keys_major else 0)
    col = lax.broadcasted_iota(jnp.int32, shape, 0 if keys_major else 1)
    rel = off + col - row
    nb = N_BUCKETS // 2
    max_exact = nb // 2
    n = jnp.abs(rel)
    large = max_exact + (jnp.log(jnp.maximum(n, 1).astype(F32) / max_exact)
                         / math.log(MAX_DIST / max_exact) * (nb - max_exact)).astype(jnp.int32)
    large = jnp.minimum(large, nb - 1)
    bucket = jnp.where(rel > 0, nb, 0) + jnp.where(n < max_exact, n, large)
    val = jnp.zeros(shape, F32)
    for bk in range(N_BUCKETS):
        val = jnp.where(bucket == bk, tbl_ref[bk * ATT_H + h], val)
    visible = jnp.right_shift(off + col, 6) <= jnp.right_shift(row, 6)
    o_ref[0, 0] = jnp.where(visible, val, NEG)


def _bias_tiles(rel_bias, offsets, tq, tk, keys_major=False):
    assert CHUNK == 64
    offs = jnp.asarray(np.asarray(offsets, np.int32))
    n_cls = len(offsets)
    shape = (tk, tq) if keys_major else (tq, tk)
    return pl.pallas_call(
        functools.partial(_bias_kernel, tq=tq, tk=tk, keys_major=keys_major),
        out_shape=jax.ShapeDtypeStruct((n_cls, ATT_H) + shape, F32),
        grid=(n_cls, ATT_H),
        in_specs=[pl.BlockSpec(memory_space=pltpu.SMEM), pl.BlockSpec(memory_space=pltpu.SMEM)],
        out_specs=pl.BlockSpec((1, 1) + shape, lambda c, h: (c, h, 0, 0)),
        compiler_params=_cparams(("parallel", "parallel")),
        name="rel_bias_tiles",
    )(offs, rel_bias.reshape(-1))


FAR_OFFSET = -(1 << 20)


def _is_far(off, tq, tk):
    nb = N_BUCKETS // 2
    max_rel = off + tk - 1
    if max_rel >= 0:
        return False
    n_min = -max_rel
    sat = (nb // 2) * (MAX_DIST / (nb // 2)) ** ((nb - 1 - nb // 2) / (nb - nb // 2))
    return n_min >= math.ceil(sat) + 1


def _q4(q2):
    lo = lax.broadcasted_iota(jnp.int32, (q2.shape[0], LANES), 1) < ATT_D
    parts = []
    for r in range(2):
        qr = q2[:, r * LANES:(r + 1) * LANES]
        parts.append(jnp.where(lo, qr, jnp.zeros_like(qr)))
        parts.append(jnp.where(lo, jnp.zeros_like(qr), qr))
    return jnp.concatenate(parts, axis=0)


def _osm_softmax(s, b0, b1, m_ref, l_ref):
    s = s + jnp.concatenate([b0, b0, b1, b1], axis=0)
    m_prev = m_ref[...]
    m_new = jnp.maximum(m_prev, jnp.max(s, axis=-1, keepdims=True))
    alpha = jnp.exp(m_prev - m_new)
    p = jnp.exp(s - m_new[:, 0:1])
    l_ref[...] = alpha * l_ref[...] + jnp.sum(p, axis=-1, keepdims=True)
    m_ref[...] = m_new
    return p.astype(BF16), alpha


def _osm_finish(l_ref, acc_ref, lam_ref, gs_ref, lam_init, tq):
    lp = lam_ref[...]
    lam = (jnp.exp(jnp.sum(lp[0:1] * lp[1:2], axis=-1, keepdims=True))
           - jnp.exp(jnp.sum(lp[2:3] * lp[3:4], axis=-1, keepdims=True)) + lam_init)
    o = acc_ref[...] / l_ref[...]
    outs = []
    for r in range(2):
        d = o[(2 * r) * tq:(2 * r + 1) * tq] - lam * o[(2 * r + 1) * tq:(2 * r + 2) * tq]
        outs.append(_rms(d, gs_ref[...]) * (1.0 - lam_init))
    return jnp.concatenate(outs, axis=-1)


def _qknorm_t_kernel(q_ref, k_ref, v_ref, gq_ref, gk_ref, qt_ref, kn_ref, knb_ref, vt_ref):
    scale = ATT_D ** -0.5
    q = q_ref[0]
    t = q.shape[0]
    lo = lax.broadcasted_iota(jnp.int32, (LANES, t), 0) < ATT_D
    for kv in range(ATT_KV):
        cols = []
        for r in range(2):
            c = kv * 2 + r
            qr = q[:, c * LANES:(c + 1) * LANES].T
            xx = qr * qr
            ms = jnp.where(lo, jnp.sum(xx[:ATT_D], axis=0, keepdims=True),
                           jnp.sum(xx[ATT_D:], axis=0, keepdims=True)) * (1.0 / ATT_D)
            qr = qr * lax.rsqrt(ms + EPS) * gq_ref[...] * scale
            cols.append(jnp.where(lo, qr, 0.0))
            cols.append(jnp.where(lo, 0.0, qr))
        qt_ref[0, kv] = jnp.concatenate(cols, axis=1).astype(qt_ref.dtype)
    k = k_ref[0]
    ks = [_norm_halves(k[:, c * LANES:(c + 1) * LANES], gk_ref[...]) for c in range(k.shape[1] // LANES)]
    kn = jnp.concatenate(ks, axis=-1)
    kn_ref[0] = kn
    knb_ref[0] = kn.astype(BF16)
    v = v_ref[0]
    for kv in range(ATT_KV):
        vt_ref[0, kv, 0] = jnp.concatenate(
            [v[:, kv * LANES:(kv + 1) * LANES].T, jnp.ones((VT_ONES, t), F32)], axis=0).astype(vt_ref.dtype)


def _qknorm_t(proj, g_q, g_k, *, t):
    b, l, _ = proj.shape
    assert l % t == 0
    qw, kw = ATT_H * 2 * ATT_D, ATT_KV * 2 * ATT_D
    gq = jnp.tile(g_q, 2).reshape(LANES, 1)
    gk = jnp.tile(g_k, 2).reshape(1, LANES)
    return pl.pallas_call(
        _qknorm_t_kernel,
        out_shape=(jax.ShapeDtypeStruct((b, ATT_KV, LANES, 4 * l), BF16),
                   jax.ShapeDtypeStruct((b, l, kw), F32),
                   jax.ShapeDtypeStruct((b, l, kw), BF16),
                   jax.ShapeDtypeStruct((b, ATT_KV, l // t, ATT_VD + VT_ONES, t), BF16)),
        grid=(b, l // t),
        in_specs=[
            pl.BlockSpec((1, t, qw), lambda i, c: (i, c, P_Q // qw)),
            pl.BlockSpec((1, t, kw), lambda i, c: (i, c, P_K // kw)),
            pl.BlockSpec((1, t, kw), lambda i, c: (i, c, P_V // kw)),
            pl.BlockSpec((LANES, 1), lambda i, c: (0, 0)),
            pl.BlockSpec((1, LANES), lambda i, c: (0, 0)),
        ],
        out_specs=(pl.BlockSpec((1, ATT_KV, LANES, 4 * t), lambda i, c: (i, 0, 0, c)),
                   pl.BlockSpec((1, t, kw), lambda i, c: (i, c, 0)),
                   pl.BlockSpec((1, t, kw), lambda i, c: (i, c, 0)),
                   pl.BlockSpec((1, ATT_KV, 1, ATT_VD + VT_ONES, t), lambda i, c: (i, 0, c, 0, 0))),
        compiler_params=_cparams(("parallel", "parallel")),
        name="qk_norm_t",
    )(proj, proj, proj, gq, gk)


def _attn_self_kernel(qt_ref, k_ref, vt_ref, bias_ref, lam_ref, gs_ref, o_ref, m_ref, acc_ref,
                      *, tq, tk, lam_init, n_cls, group):
    i = pl.program_id(2)
    m_ref[...] = jnp.full(m_ref.shape, NEG, F32)
    acc_ref[...] = jnp.zeros(acc_ref.shape, F32)
    sub = LANES
    assert tq == tk and tq % sub == 0 and sub % CHUNK == 0

    def step(j, mode):
        kt = k_ref[0, pl.ds(pl.multiple_of(j * tk, tk), tk), :]
        vt = vt_ref[0, 0, j]
        cls = jnp.minimum(i - j, n_cls - 1)
        for c0 in range(0, 4, group):
            update(kt, vt, cls, mode, range(c0, c0 + group))

    def softmax_block(s, m_prev, crow):
        m_cur = jnp.max(s, axis=0, keepdims=True)
        if crow is None:
            m_new = jnp.maximum(m_prev, m_cur)
            return jnp.exp(s - m_new).astype(BF16), m_new
        m_new = jnp.maximum(m_prev, m_cur + crow)
        return jnp.exp(s - (m_new - crow)).astype(BF16), m_new

    def update(kt, vt, cls, mode, combos):
        cols = {c: slice(c * tq, (c + 1) * tq) for c in combos}
        ss = {c: jnp.dot(kt, qt_ref[0, 0, :, cs], preferred_element_type=F32) for c, cs in cols.items()}
        ps, alphas = {}, {}
        for c, cs in cols.items():
            s = ss[c]
            m_prev = m_ref[:, cs]
            if mode == "far":
                p, m_new = softmax_block(s, m_prev, bias_ref[cls, c // 2, 0:1, :])
            elif mode == "near":
                p, m_new = softmax_block(s + bias_ref[cls, c // 2], m_prev, None)
            else:
                pb, mb = [], []
                for qb in range(tq // sub):
                    rows, lanes = (qb + 1) * sub, slice(qb * sub, (qb + 1) * sub)
                    p_b, m_b = softmax_block(s[:rows, lanes] + bias_ref[cls, c // 2, :rows, lanes],
                                             m_prev[:, lanes], None)
                    if rows < tk:
                        p_b = jnp.concatenate([p_b, jnp.zeros((tk - rows, sub), BF16)], axis=0)
                    pb.append(p_b)
                    mb.append(m_b)
                p, m_new = jnp.concatenate(pb, axis=1), jnp.concatenate(mb, axis=1)
            alphas[c] = jnp.exp(m_prev - m_new)
            m_ref[:, cs] = m_new
            ps[c] = p
        for c, cs in cols.items():
            acc_ref[:, cs] = alphas[c] * acc_ref[:, cs] + jnp.dot(vt, ps[c], preferred_element_type=F32)

    n_far = jnp.maximum(i - (n_cls - 2), 0) if n_cls == 3 else 0

    def body(mode):
        def run(j, carry):
            step(j, mode)
            return carry
        return run

    if n_cls == 3:
        lax.fori_loop(0, n_far, body("far"), 0)
    lax.fori_loop(n_far, i, body("near"), 0)
    step(i, "diag")

    lp = lam_ref[...]
    lam = (jnp.exp(jnp.sum(lp[0:1] * lp[1:2], axis=-1, keepdims=True))
           - jnp.exp(jnp.sum(lp[2:3] * lp[3:4], axis=-1, keepdims=True)) + lam_init)
    o = acc_ref[0:ATT_VD, :] / acc_ref[ATT_VD:ATT_VD + 1, :]
    outs = []
    for r in range(2):
        d = o[:, (2 * r) * tq:(2 * r + 1) * tq] - lam * o[:, (2 * r + 1) * tq:(2 * r + 2) * tq]
        ms = jnp.mean(d * d, axis=0, keepdims=True)
        y = d * lax.rsqrt(ms + EPS) * gs_ref[...] * (1.0 - lam_init)
        outs.append(y.T)
    o_ref[0] = jnp.concatenate(outs, axis=-1).astype(o_ref.dtype)


def _attn_self(qt, knb, vt, bias, lam_p, g_subln, lam_init, *, tq, ybuf=None, slab=0):
    b, l, _ = knb.shape
    n_cls = bias.shape[0]
    nk = l // tq
    hw = 2 * ATT_VD
    assert ATT_H * ATT_VD == BRANCH_W
    args = [qt, knb, vt, bias, lam_p, g_subln.reshape(ATT_VD, 1)]
    specs = [
        pl.BlockSpec((1, 1, LANES, 4 * tq), lambda bi, kv, i: (bi, kv, 0, i)),
        pl.BlockSpec((1, l, LANES), lambda bi, kv, i: (bi, 0, kv)),
        pl.BlockSpec((1, 1, nk, ATT_VD + VT_ONES, tq), lambda bi, kv, i: (bi, kv, 0, 0, 0)),
        pl.BlockSpec((n_cls, 2, tq, tq), lambda bi, kv, i: (0, kv, 0, 0)),
        pl.BlockSpec((4, ATT_D), lambda bi, kv, i: (0, 0)),
        pl.BlockSpec((ATT_VD, 1), lambda bi, kv, i: (0, 0)),
    ]
    yshape, kern, aliases = _branch_out(
        ybuf, slab, b, l, args, specs,
        functools.partial(_attn_self_kernel, tq=tq, tk=tq, lam_init=lam_init, n_cls=n_cls, group=4))
    return pl.pallas_call(
        kern,
        out_shape=yshape,
        grid=(b, ATT_KV, l // tq),
        in_specs=specs,
        out_specs=pl.BlockSpec((None, 1, tq, hw), lambda bi, kv, i: (slab, bi, i, kv)),
        scratch_shapes=[pltpu.VMEM((1, 4 * tq), F32), pltpu.VMEM((ATT_VD + VT_ONES, 4 * tq), F32)],
        input_output_aliases=aliases,
        compiler_params=_cparams(("parallel", "parallel", "arbitrary")),
        name="diff_attn_self",
    )(*args)


def _attn_cached_kernel(cls_ref, q_ref, ck_ref, cv_ref, kn_ref, vn_ref, bp_ref, bn_ref, lam_ref, gs_ref,
                        o_ref, m_ref, l_ref, acc_ref, *, tq, tk, nt, lam_init):
    del cls_ref
    j = pl.program_id(1)

    @pl.when(j == 0)
    def _():
        m_ref[...] = jnp.full(m_ref.shape, NEG, F32)
        l_ref[...] = jnp.zeros(l_ref.shape, F32)
        acc_ref[...] = jnp.zeros(acc_ref.shape, F32)

    dn_t = (((1,), (1,)), ((), ()))

    def update(k_all, v_of, bias_at):
        kvs = range(ATT_KV)
        ss = [lax.dot_general(_q4(q_ref[0, :, kv * 2 * LANES:(kv + 1) * 2 * LANES]),
                              k_all[:, kv * LANES:(kv + 1) * LANES].astype(BF16), dn_t,
                              preferred_element_type=F32) for kv in kvs]
        pa = [_osm_softmax(ss[kv], bias_at(2 * kv), bias_at(2 * kv + 1), m_ref.at[kv], l_ref.at[kv])
              for kv in kvs]
        for kv in kvs:
            p, alpha = pa[kv]
            acc_ref[kv] = alpha * acc_ref[kv] + jnp.dot(p, v_of(kv).astype(BF16), preferred_element_type=F32)

    @pl.when(j < nt)
    def _():
        update(ck_ref[...], lambda kv: cv_ref[pl.ds(kv, tk, stride=ATT_KV), :], lambda h: bp_ref[0, h])

    @pl.when(j == nt)
    def _():
        update(kn_ref[0], lambda kv: vn_ref[0, :, kv * LANES:(kv + 1) * LANES], lambda h: bn_ref[0, h])
        outs = [_osm_finish(l_ref.at[kv], acc_ref.at[kv], lam_ref, gs_ref, lam_init, tq)
                for kv in range(ATT_KV)]
        o_ref[0] = jnp.concatenate(outs, axis=-1).astype(o_ref.dtype)


def _attn_cached(qn, knb, vb, cache_k, cache_v, layer, bias_past, cls_tbl, bias_new, lam_p, g_subln,
                 lam_init, *, tk, ybuf=None, slab=0):
    b, l, _ = qn.shape
    past = cache_k.shape[2]
    nt = past // tk
    kw = ATT_KV * 2 * ATT_D
    ck = cache_k.reshape(cache_k.shape[0], b, past, kw)
    cv = cache_v.reshape(cache_v.shape[0], b, past * ATT_KV, ATT_VD)
    args = [qn, ck, cv, knb, vb, bias_past, bias_new, lam_p, g_subln.reshape(1, ATT_VD)]
    specs = [
        pl.BlockSpec((1, l, ATT_H * 2 * ATT_D), lambda bi, j, cls: (bi, 0, 0)),
        pl.BlockSpec((None, None, tk, kw), lambda bi, j, cls: (layer, bi, jnp.minimum(j, nt - 1), 0)),
        pl.BlockSpec((None, None, tk * ATT_KV, ATT_VD),
                     lambda bi, j, cls: (layer, bi, jnp.minimum(j, nt - 1), 0)),
        pl.BlockSpec((1, l, kw), lambda bi, j, cls: (bi, 0, 0)),
        pl.BlockSpec((1, l, kw), lambda bi, j, cls: (bi, 0, 0)),
        pl.BlockSpec((1, ATT_H, l, tk), lambda bi, j, cls: (cls[jnp.minimum(j, nt - 1)], 0, 0, 0)),
        pl.BlockSpec((1, ATT_H, l, l), lambda bi, j, cls: (0, 0, 0, 0)),
        pl.BlockSpec((4, ATT_D), lambda bi, j, cls: (0, 0)),
        pl.BlockSpec((1, ATT_VD), lambda bi, j, cls: (0, 0)),
    ]
    yshape, kern, aliases = _branch_out(
        ybuf, slab, b, l, args, specs,
        functools.partial(_attn_cached_kernel, tq=l, tk=tk, nt=nt, lam_init=lam_init), n_prefetch=1)
    grid_spec = pltpu.PrefetchScalarGridSpec(
        num_scalar_prefetch=1,
        grid=(b, nt + 1),
        in_specs=specs,
        out_specs=pl.BlockSpec((None, 1, l, ATT_H * ATT_VD), lambda bi, j, cls: (slab, bi, 0, 0)),
        scratch_shapes=[pltpu.VMEM((ATT_KV, 4 * l, LANES), F32), pltpu.VMEM((ATT_KV, 4 * l, LANES), F32),
                        pltpu.VMEM((ATT_KV, 4 * l, ATT_VD), F32)],
    )
    return pl.pallas_call(
        kern,
        out_shape=yshape,
        grid_spec=grid_spec,
        input_output_aliases=aliases,
        compiler_params=_cparams(("parallel", "arbitrary")),
        name="diff_attn_cached",
    )(cls_tbl, *args)


def _xattn_kernel(q_ref, mk_ref, mv_ref, o_ref):
    dn_t = (((1,), (1,)), ((), ()))
    sls = [slice(h * X_D, (h + 1) * X_D) for h in range(X_H)]
    ss = [lax.dot_general(q_ref[0, :, sl], mk_ref[0, :, sl].astype(BF16), dn_t,
                          preferred_element_type=F32) * (X_D ** -0.5) for sl in sls]
    ps = []
    for s in ss:
        s = s - jnp.max(s, axis=-1, keepdims=True)
        p = jnp.exp(s)
        ps.append((p / jnp.sum(p, axis=-1, keepdims=True)).astype(BF16))
    outs = [jnp.dot(p, mv_ref[0, :, sl].astype(BF16), preferred_element_type=F32) for p, sl in zip(ps, sls)]
    o_ref[0] = jnp.concatenate(outs, axis=-1).astype(o_ref.dtype)


def _xattn(q, mk, mv, *, tq):
    b, l, w = q.shape
    mlen = mk.shape[1]
    tq = min(tq, l)
    assert l % tq == 0
    return pl.pallas_call(
        _xattn_kernel,
        out_shape=jax.ShapeDtypeStruct((b, l, w), BF16),
        grid=(b, l // tq),
        in_specs=[
            pl.BlockSpec((1, tq, w), lambda i, c: (i, c, 0)),
            pl.BlockSpec((1, mlen, w), lambda i, c: (i, 0, 0)),
            pl.BlockSpec((1, mlen, w), lambda i, c: (i, 0, 0)),
        ],
        out_specs=pl.BlockSpec((1, tq, w), lambda i, c: (i, c, 0)),
        compiler_params=_cparams(("parallel", "parallel")),
        name="mem_cross_attn",
    )(q, mk, mv)


def _prep_weights(p, depth):
    layers = []
    off_z, off_xbc = 0, SSM_D
    off_dt = off_xbc + XBC_W
    off_pool = off_dt + SSM_H
    off_q = off_pool + BRANCH_W
    off_k = off_q + ATT_H * 2 * ATT_D
    off_v = off_k + ATT_KV * 2 * ATT_D
    off_gu = off_v + ATT_KV * ATT_VD
    off_gv = off_gu + BRANCH_W
    in_w = off_gv + BRANCH_W
    for l in range(depth):
        w = p['w_in'][l]
        d = w.shape[0]
        w_in = jnp.concatenate([
            w[:, off_xbc:off_dt], w[:, off_z:off_xbc], w[:, off_pool:off_q], w[:, off_q:off_k],
            w[:, off_gu:off_gv], w[:, off_gv:in_w], w[:, off_k:off_v], w[:, off_v:off_gu],
            w[:, off_dt:off_pool], jnp.zeros((d, LANES - SSM_H), w.dtype)], axis=1).astype(BF16)
        pad = lambda v: jnp.pad(v, (0, LANES - SSM_H)).reshape(1, LANES)
        layers.append(dict(
            g_ffn1=p['g_ffn1'][l], ffn1_in=p['w_ffn1_in'][l].astype(BF16), ffn1_out=p['w_ffn1_out'][l].astype(BF16),
            g_mix=p['g_mix'][l], w_in=w_in,
            conv_w=p['conv_w'][l], conv_b=p['conv_b'][l].reshape(1, XBC_W),
            dt_bias=pad(p['dt_bias'][l]), a_log=pad(p['a_log'][l]),
            d_skip=jnp.repeat(p['d_skip'][l], SSM_P).reshape(1, SSM_D), g_ssd=p['g_ssd'][l].reshape(1, SSM_D),
            w_pool=p['w_pool'][l].astype(BF16), pool_scale=p['pool_scale'][l].reshape(1, BRANCH_W),
            g_q=p['g_q'][l], g_k=p['g_k'][l], lam=p['lam'][l], g_subln=p['g_subln'][l],
            g_gv=p['g_gv'][l].reshape(1, BRANCH_W), w_sp=p['w_sp'][l], b_sp=p['b_sp'][l],
            w_gate=p['w_gate'][l].astype(BF16), b_gate=p['b_gate'][l], w_branch=p['w_branch'][l].astype(BF16),
            w_out=p['w_out'][l].astype(BF16),
            g_x=p['g_x'][l], w_xq=p['w_xq'][l].astype(BF16), g_xq=p['g_xq'][l], w_xo=p['w_xo'][l].astype(BF16),
            g_ffn2=p['g_ffn2'][l], ffn2_in=p['w_ffn2_in'][l].astype(BF16), ffn2_out=p['w_ffn2_out'][l].astype(BF16),
            g_post=p['g_post'][l],
            g_mem=p['g_mem'][l], w_mk=p['w_mk'][l].astype(BF16), w_mv=p['w_mv'][l].astype(BF16), g_xk=p['g_xk'][l],
        ))
    return layers


def _run_trunk(x, cache_k, cache_v, mem_k, mem_v, ssm0, conv0, pool0, layers, rel_bias):
    b, l, d = x.shape
    m = b * l
    depth = len(layers)
    cached = cache_k is not None
    assert l >= POOL_BUF and l >= CONV_W - 1
    if cached:
        past = cache_k.shape[2]
        tk = min(1024, past)
        assert past % tk == 0 and past % CHUNK == 0 and l <= CHUNK and (past % GM_CHUNK == 0)
        nt = past // tk
        offs, cls = [FAR_OFFSET], []
        for j in range(nt):
            off = j * tk - past
            if _is_far(off, l, tk):
                cls.append(0)
            else:
                offs.append(off)
                cls.append(len(offs) - 1)
        bias_past = _bias_tiles(rel_bias, offs, l, tk)
        cls_tbl = jnp.asarray(np.asarray(cls, np.int32))
        bias_new = _bias_tiles(rel_bias, [0], l, l)
    else:
        past = 0
        tq = min(512, l)
        assert l % tq == 0 and tq % CHUNK == 0
        offs = [0]
        if l > tq:
            offs.append(-tq)
        if l > 2 * tq:
            assert _is_far(-2 * tq, tq, tq)
            offs.append(FAR_OFFSET)
        bias_self = _bias_tiles(rel_bias, offs, tq, tq, keys_major=True)

    h = x.reshape(m, d)
    new_k, new_v, new_ssm, new_conv, new_pool, new_gv = [], [], [], [], [], []
    for li, w in enumerate(layers):
        lam_init = 0.8 - 0.6 * math.exp(-0.3 * li)
        h = _ffn(h, w['g_ffn1'], w['ffn1_in'], w['ffn1_out'], tm=1024)
        proj, hn = _mm(h, w['w_in'], g=w['g_mix'], out_dtype=F32, tm=1024, tn=1664, name="in_proj", emit_xn=True)
        proj = proj.reshape(b, l, P_W)

        ys, s_ssm = _ssd(proj, conv0[li] if cached else None, ssm0[li] if cached else None,
                         w['conv_w'], w['conv_b'], w['dt_bias'], w['a_log'], w['d_skip'], w['g_ssd'], q=128)
        ys = _pool(proj, pool0[li] if cached else None, past, w['w_pool'], w['pool_scale'], t=256,
                   ybuf=ys, slab=1)
        if cached:
            qn, kn, knb, vb = _qknorm(proj, w['g_q'], w['g_k'], t=512)
            ys = _attn_cached(qn, knb, vb, cache_k, cache_v, li, bias_past, cls_tbl, bias_new,
                              w['lam'], w['g_subln'], lam_init, tk=tk, ybuf=ys, slab=2)
        else:
            qt, kn, knb, vt = _qknorm_t(proj, w['g_q'], w['g_k'], t=tq)
            ys = _attn_self(qt, knb, vt, bias_self, w['lam'], w['g_subln'], lam_init, tq=tq, ybuf=ys, slab=2)
        ys, v_gm = _gmlp(proj, w['g_gv'], w['w_sp'], w['b_sp'], keep_vn=cached, ybuf=ys, slab=3)

        merged = _merge(hn, ys.reshape(N_BRANCH, m, BRANCH_W), w['w_gate'], w['b_gate'], w['w_branch'])
        h = _mm(merged, w['w_out'], res=h, out_dtype=F32, tm=512, tn=2048, name="out_proj", w_resident=True)

        qx = _mm(h, w['w_xq'], g=w['g_x'], gh=w['g_xq'], out_dtype=BF16, tm=512, tn=512, name="xattn_q")
        o = _xattn(qx.reshape(b, l, X_H * X_D), mem_k[li].reshape(b, -1, X_H * X_D),
                   mem_v[li].reshape(b, -1, X_H * X_D), tq=512)
        h = _ffn(h, w['g_ffn2'], w['ffn2_in'], w['ffn2_out'], post_g=w['g_post'],
                 pre=(o.reshape(m, X_H * X_D), w['w_xo']))

        new_k.append(kn.reshape(b, l, ATT_KV, 2, ATT_D))
        new_v.append(proj[:, :, P_V:P_V + ATT_KV * ATT_VD].reshape(b, l, ATT_KV, ATT_VD))
        new_ssm.append(s_ssm)
        new_conv.append(proj[:, l - (CONV_W - 1):, P_XBC:P_XBC + XBC_W])
        new_pool.append(proj[:, l - POOL_BUF:, P_POOL:P_POOL + BRANCH_W])
        new_gv.append(v_gm)
    return (h.reshape(b, l, d), jnp.stack(new_k), jnp.stack(new_v), jnp.stack(new_ssm),
            jnp.stack(new_conv), jnp.stack(new_pool), jnp.stack(new_gv) if cached else None)


def kernel(x_prompt, x_sample, cache_attn_k, cache_attn_v, cache_mem_k, cache_mem_v, state_ssm, state_conv, state_pool, mem_prompt, g_ffn1, w_ffn1_in, w_ffn1_out, g_mix, w_in, conv_w, conv_b, dt_bias, a_log, d_skip, g_ssd, w_pool, pool_scale, g_q, g_k, lam, g_subln, rel_bias, g_gv, w_sp, b_sp, w_gate, b_gate, w_branch, w_out, g_x, w_xq, g_xq, g_mem, w_mk, w_mv, g_xk, w_xo, g_ffn2, w_ffn2_in, w_ffn2_out, g_post):
    p = dict(g_ffn1=g_ffn1, w_ffn1_in=w_ffn1_in, w_ffn1_out=w_ffn1_out, g_mix=g_mix, w_in=w_in,
             conv_w=conv_w, conv_b=conv_b, dt_bias=dt_bias, a_log=a_log, d_skip=d_skip, g_ssd=g_ssd,
             w_pool=w_pool, pool_scale=pool_scale, g_q=g_q, g_k=g_k, lam=lam, g_subln=g_subln,
             g_gv=g_gv, w_sp=w_sp, b_sp=b_sp, w_gate=w_gate, b_gate=b_gate, w_branch=w_branch,
             w_out=w_out, g_x=g_x, w_xq=w_xq, g_xq=g_xq, g_mem=g_mem, w_mk=w_mk, w_mv=w_mv, g_xk=g_xk,
             w_xo=w_xo, g_ffn2=g_ffn2, w_ffn2_in=w_ffn2_in, w_ffn2_out=w_ffn2_out, g_post=g_post)
    depth = w_in.shape[0]
    layers = _prep_weights(p, depth)

    bp, mlen, d = mem_prompt.shape
    mem2 = mem_prompt.reshape(bp * mlen, d)
    mks, mvs = [], []
    for w in layers:
        mk = _mm(mem2, w['w_mk'], g=w['g_mem'], gh=w['g_xk'], out_dtype=F32, tm=512, tn=512, name="mem_k")
        mv = _mm(mem2, w['w_mv'], g=w['g_mem'], out_dtype=F32, tm=512, tn=512, name="mem_v")
        mks.append(mk.reshape(bp, mlen, X_H, X_D))
        mvs.append(mv.reshape(bp, mlen, X_H, X_D))
    p_mem_k = jnp.stack(mks)
    p_mem_v = jnp.stack(mvs)

    y_prompt, p_attn_k, p_attn_v, p_ssm, p_conv, p_pool, _ = _run_trunk(
        x_prompt, None, None, p_mem_k, p_mem_v, None, None, None, layers, rel_bias)
    y_sample, s_attn_k, s_attn_v, s_ssm, s_conv, s_pool, s_gmlp_v = _run_trunk(
        x_sample, cache_attn_k, cache_attn_v, cache_mem_k, cache_mem_v, state_ssm, state_conv,
        state_pool, layers, rel_bias)

    return (y_prompt, y_sample, p_attn_k, p_attn_v, p_mem_k, p_mem_v, p_ssm, p_conv, p_pool,
            s_attn_k, s_attn_v, s_ssm, s_conv, s_pool, s_gmlp_v)
```
